```python
import jax, jax.numpy as jnp
from jax import lax
import numpy as np

D_MODEL = 2048
BATCH = 4
SEQ = 2048
DEPTH = 1
DEC_BATCH = 128
DEC_SEQ = 1
PAST_LEN = 16384
PAGE_SIZE = 128

GDN_HEADS = 8
GDN_DK = 128
GDN_DV = 128
CONV_K = 4
GDN_CHUNK = 64
HG_HEADS = 8
HG_DF = 128
HG_DV = 128
HG_CHUNK = 32
N_EXPERTS = 32
TOP_K = 4
D_FF = D_MODEL
SWIGLU_ALPHA = 1.702
SWIGLU_LIMIT = 7.0
EPS = 1e-6

GDN_QK = GDN_HEADS * GDN_DK
GDN_V = GDN_HEADS * GDN_DV
CONV_DIM = 2 * GDN_QK + GDN_V
HG_F = HG_HEADS * HG_DF
HG_V = HG_HEADS * HG_DV
MIX_WIDTH = GDN_V + HG_V
IN_SIZES = (CONV_DIM, GDN_V, GDN_HEADS, GDN_HEADS, HG_F, HG_F, HG_V, HG_V)
IN_DIM = sum(IN_SIZES)

kernel_name = "hybrid_gdn_hgrn2_moe_step"


def _rmsnorm(x, w):
    xf = x.astype(jnp.float32)
    y = xf * lax.rsqrt(jnp.mean(xf * xf, axis=-1, keepdims=True) + EPS)
    return (y * w.astype(jnp.float32)).astype(x.dtype)


def _l2norm(x):
    return x * lax.rsqrt(jnp.sum(x * x, axis=-1, keepdims=True) + EPS)


def _short_conv(u, buf, w):
    L = u.shape[1]
    full = jnp.concatenate([buf.astype(u.dtype), u], axis=1)
    out = full[:, 0:L] * w[0]
    for j in range(1, CONV_K):
        out = out + full[:, j:j + L] * w[j]
    return jax.nn.silu(out), full[:, full.shape[1] - (CONV_K - 1):]


def _chunking(L, chunk):
    C = min(chunk, L)
    return C, -(-L // C)


def _to_chunks(t, C, n):
    pad = n * C - t.shape[2]
    t = jnp.pad(t, [(0, 0), (0, 0), (0, pad)] + [(0, 0)] * (t.ndim - 3))
    t = t.reshape(t.shape[:2] + (n, C) + t.shape[3:])
    return jnp.moveaxis(t, 2, 0)


def _from_chunks(o, L):
    n, B, H, C, D = o.shape
    o = jnp.moveaxis(o, 0, 2).reshape(B, H, n * C, D)[:, :, :L]
    return jnp.swapaxes(o, 1, 2)


def _gated_delta_rule(q, k, v, beta, g, S0):
    B, L = q.shape[:2]
    C, n = _chunking(L, GDN_CHUNK)
    tr = lambda t: jnp.swapaxes(t, 1, 2)
    qc, kc, vc, bc, gc = (_to_chunks(tr(t), C, n) for t in (q, k, v, beta, g))
    causal = jnp.tril(jnp.ones((C, C), bool))
    strict = jnp.tril(jnp.ones((C, C), bool), -1)
    eye = jnp.eye(C, dtype=jnp.float32)

    def step(S, inp):
        qi, ki, vi, bi, gi = inp
        gcum = jnp.cumsum(gi, axis=-1)
        decay = jnp.exp(jnp.where(causal, gcum[..., :, None] - gcum[..., None, :], -jnp.inf))
        kb = ki * bi[..., None]
        A = jnp.where(strict, jnp.einsum('bhik,bhjk->bhij', kb, ki) * decay, 0.0)
        rhs = jnp.concatenate([vi * bi[..., None], kb * jnp.exp(gcum)[..., None]], axis=-1)
        sol = lax.linalg.triangular_solve(eye + A, rhs, left_side=True, lower=True, unit_diagonal=True)
        u, w = sol[..., :GDN_DV], sol[..., GDN_DV:]
        v_new = u - jnp.einsum('bhik,bhkv->bhiv', w, S)
        scores = jnp.einsum('bhik,bhjk->bhij', qi, ki) * decay
        o = (jnp.einsum('bhik,bhkv->bhiv', qi * jnp.exp(gcum)[..., None], S)
             + jnp.einsum('bhij,bhjv->bhiv', scores, v_new))
        g_last = gcum[..., -1]
        S = (S * jnp.exp(g_last)[..., None, None]
             + jnp.einsum('bhjk,bhjv->bhkv', ki * jnp.exp(g_last[..., None] - gcum)[..., None], v_new))
        return S, o

    S, o = lax.scan(step, S0, (qc, kc, vc, bc, gc))
    return _from_chunks(o, L), S


def _gla_recurrence(q, k, v, logf, S0):
    B, L = q.shape[:2]
    C, n = _chunking(L, HG_CHUNK)
    tr = lambda t: jnp.swapaxes(t, 1, 2)
    qc, kc, vc, fc = (_to_chunks(tr(t), C, n) for t in (q, k, v, logf))
    causal = jnp.tril(jnp.ones((C, C), bool))[:, :, None]

    def step(S, inp):
        qi, ki, vi, lf = inp
        gcum = jnp.cumsum(lf, axis=2)
        decay = jnp.exp(jnp.where(causal, gcum[:, :, :, None, :] - gcum[:, :, None, :, :], -jnp.inf))
        scores = jnp.einsum('bhie,bhije,bhje->bhij', qi, decay, ki)
        o = (jnp.einsum('bhie,bhev->bhiv', qi * jnp.exp(gcum), S)
             + jnp.einsum('bhij,bhjv->bhiv', scores, vi))
        g_last = gcum[:, :, -1]
        S = (S * jnp.exp(g_last)[..., None]
             + jnp.einsum('bhje,bhjv->bhev', ki * jnp.exp(g_last[:, :, None] - gcum), vi))
        return S, o

    S, o = lax.scan(step, S0, (qc, kc, vc, fc))
    return _from_chunks(o, L), S


def _moe(h, w_router, b_router, w_gate_up, b_gate_up, w_down, b_down):
    B, L, D = h.shape
    t = h.reshape(B * L, D)
    logits = (t @ w_router + b_router).astype(jnp.float32)
    top_val, top_idx = lax.top_k(logits, TOP_K)
    top_w = jax.nn.softmax(top_val, axis=-1)
    gates = jnp.einsum('tk,tke->te', top_w, jax.nn.one_hot(top_idx, N_EXPERTS, dtype=jnp.float32))
    out = jnp.zeros((B * L, D), jnp.float32)
    for e in range(N_EXPERTS):
        gu = t @ w_gate_up[e] + b_gate_up[e]
        gate = jnp.minimum(gu[:, 0::2], SWIGLU_LIMIT)
        up = jnp.clip(gu[:, 1::2], -SWIGLU_LIMIT, SWIGLU_LIMIT)
        act = (up + 1.0) * (gate * jax.nn.sigmoid(gate * SWIGLU_ALPHA))
        out = out + gates[:, e:e + 1] * (act @ w_down[e] + b_down[e]).astype(jnp.float32)
    return out.astype(h.dtype).reshape(B, L, D)


def _layer(x, conv_buf, S_gdn, S_hg, lb, norm1, w_in, conv_w, A_log, dt_bias, gdn_norm, hg_norm, w_out,
           norm2, w_router, b_router, w_gate_up, b_gate_up, w_down, b_down):
    B, L, _ = x.shape
    f32 = jnp.float32
    h = _rmsnorm(x, norm1)
    proj = h @ w_in
    qkv, z, b, a, hq, hf, hi, hg = jnp.split(proj, list(np.cumsum(IN_SIZES)[:-1]), axis=-1)

    qkv, conv_new = _short_conv(qkv, conv_buf, conv_w)
    qkv = qkv.astype(f32)
    q, k, v = jnp.split(qkv, [GDN_QK, 2 * GDN_QK], axis=-1)
    q = _l2norm(q.reshape(B, L, GDN_HEADS, GDN_DK)) * (GDN_DK ** -0.5)
    k = _l2norm(k.reshape(B, L, GDN_HEADS, GDN_DK))
    v = v.reshape(B, L, GDN_HEADS, GDN_DV)
    beta = jax.nn.sigmoid(b.astype(f32))
    g = -jnp.exp(A_log.astype(f32)) * jax.nn.softplus(a.astype(f32) + dt_bias.astype(f32))
    o_a, S_gdn_new = _gated_delta_rule(q, k, v, beta, g, S_gdn.astype(f32))
    o_a = _rmsnorm(o_a, gdn_norm) * jax.nn.silu(z.astype(f32).reshape(B, L, GDN_HEADS, GDN_DV))

    hf = hf.astype(f32)
    lbf = lb.astype(f32)
    forget = lbf + (1.0 - lbf) * jax.nn.sigmoid(hf)
    inp_gate = (1.0 - lbf) * jax.nn.sigmoid(-hf)
    qh = jax.nn.silu(hq.astype(f32)).reshape(B, L, HG_HEADS, HG_DF)
    kh = inp_gate.reshape(B, L, HG_HEADS, HG_DF)
    logf = jnp.log(forget).reshape(B, L, HG_HEADS, HG_DF)
    vh = hi.astype(f32).reshape(B, L, HG_HEADS, HG_DV)
    o_b, S_hg_new = _gla_recurrence(qh, kh, vh, logf, S_hg.astype(f32))
    o_b = _rmsnorm(o_b, hg_norm) * jax.nn.silu(hg.astype(f32).reshape(B, L, HG_HEADS, HG_DV))

    mix = jnp.concatenate([o_a.reshape(B, L, GDN_V), o_b.reshape(B, L, HG_V)], axis=-1).astype(x.dtype)
    x = x + mix @ w_out
    x = x + _moe(_rmsnorm(x, norm2), w_router, b_router, w_gate_up, b_gate_up, w_down, b_down)
    return x, S_gdn_new, conv_new, S_hg_new


def setup_inputs(seed: int = 0) -> dict:
    key = jax.random.key(seed)
    ks = jax.random.split(key, 24)
    nrm = lambda k, s, sc: jax.random.normal(k, s, jnp.float32) * sc
    gain = lambda k, s: 1.0 + 0.02 * jax.random.normal(k, s, jnp.float32)
    dt = jnp.exp(jax.random.uniform(ks[10], (DEPTH, GDN_HEADS), jnp.float32, np.log(1e-3), np.log(1e-1)))
    return {
        "x_prompt": nrm(ks[0], (BATCH, SEQ, D_MODEL), 1.0),
        "x_sample": nrm(ks[1], (DEC_BATCH, DEC_SEQ, D_MODEL), 1.0),
        "state_gdn": nrm(ks[2], (DEPTH, DEC_BATCH, GDN_HEADS, GDN_DK, GDN_DV), 1.0),
        "state_conv": nrm(ks[3], (DEPTH, DEC_BATCH, CONV_K - 1, CONV_DIM), 1.0),
        "state_hgrn": nrm(ks[4], (DEPTH, DEC_BATCH, HG_HEADS, HG_DF, HG_DV), 1.0),
        "lb_table": nrm(ks[5], (DEPTH + 1, HG_F), 1.0),
        "norm1": gain(ks[6], (DEPTH, D_MODEL)),
        "w_in": nrm(ks[7], (DEPTH, D_MODEL, IN_DIM), D_MODEL ** -0.5),
        "conv_w": nrm(ks[8], (DEPTH, CONV_K, CONV_DIM), CONV_K ** -0.5),
        "A_log": jnp.log(jax.random.uniform(ks[9], (DEPTH, GDN_HEADS), jnp.float32, 1.0, 16.0)),
        "dt_bias": dt + jnp.log(-jnp.expm1(-dt)),
        "gdn_norm": gain(ks[11], (DEPTH, GDN_DV)),
        "hg_norm": gain(ks[12], (DEPTH, HG_DV)),
        "w_out": nrm(ks[13], (DEPTH, MIX_WIDTH, D_MODEL), MIX_WIDTH ** -0.5),
        "norm2": gain(ks[14], (DEPTH, D_MODEL)),
        "w_router": nrm(ks[15], (DEPTH, D_MODEL, N_EXPERTS), D_MODEL ** -0.5),
        "b_router": nrm(ks[16], (DEPTH, N_EXPERTS), 0.01),
        "w_gate_up": nrm(ks[17], (DEPTH, N_EXPERTS, D_MODEL, 2 * D_FF), D_MODEL ** -0.5),
        "b_gate_up": nrm(ks[18], (DEPTH, N_EXPERTS, 2 * D_FF), 0.01),
        "w_down": nrm(ks[19], (DEPTH, N_EXPERTS, D_FF, D_MODEL), D_FF ** -0.5),
        "b_down": nrm(ks[20], (DEPTH, N_EXPERTS, D_MODEL), 0.01),
        "final_norm": gain(ks[21], (D_MODEL,)),
    }


def reference(x_prompt, x_sample, state_gdn, state_conv, state_hgrn, lb_table, norm1, w_in, conv_w, A_log,
              dt_bias, gdn_norm, hg_norm, w_out, norm2, w_router, b_router, w_gate_up, b_gate_up, w_down,
              b_down, final_norm):
    lbs = jnp.cumsum(jax.nn.softmax(lb_table.astype(jnp.float32), axis=0), axis=0)
    xp, xs = x_prompt, x_sample
    gp, cp, hp, gs, cs, hs = [], [], [], [], [], []
    for l in range(DEPTH):
        w = (norm1[l], w_in[l], conv_w[l], A_log[l], dt_bias[l], gdn_norm[l], hg_norm[l], w_out[l],
             norm2[l], w_router[l], b_router[l], w_gate_up[l], b_gate_up[l], w_down[l], b_down[l])
        conv0 = jnp.zeros((BATCH, CONV_K - 1, CONV_DIM), xp.dtype)
        sg0 = jnp.zeros((BATCH, GDN_HEADS, GDN_DK, GDN_DV), jnp.float32)
        sh0 = jnp.zeros((BATCH, HG_HEADS, HG_DF, HG_DV), jnp.float32)
        xp, sg, sc, sh = _layer(xp, conv0, sg0, sh0, lbs[l], *w)
        gp.append(sg); cp.append(sc); hp.append(sh)
        xs, sg, sc, sh = _layer(xs, state_conv[l], state_gdn[l], state_hgrn[l], lbs[l], *w)
        gs.append(sg); cs.append(sc); hs.append(sh)
    y_prompt = _rmsnorm(xp, final_norm)
    y_sample = _rmsnorm(xs, final_norm)
    state_gdn_prompt = jnp.stack(gp).astype(state_gdn.dtype)
    state_conv_prompt = jnp.stack(cp).astype(state_conv.dtype)
    state_hgrn_prompt = jnp.stack(hp).astype(state_hgrn.dtype)
    state_gdn_sample = jnp.stack(gs).astype(state_gdn.dtype)
    state_conv_sample = jnp.stack(cs).astype(state_conv.dtype)
    state_hgrn_sample = jnp.stack(hs).astype(state_hgrn.dtype)
    return (y_prompt, y_sample, state_gdn_prompt, state_conv_prompt, state_hgrn_prompt,
            state_gdn_sample, state_conv_sample, state_hgrn_sample)
```

```python
import functools

import jax
import jax.numpy as jnp
from jax import lax
from jax.experimental import pallas as pl
from jax.experimental.pallas import tpu as pltpu

F32 = jnp.float32
BF16 = jnp.bfloat16
I32 = jnp.int32

EPS = 1e-6
HEADS = 8
HD = 128
CONV_K = 4
N_EXPERTS = 32
TOP_K = 4
SWIGLU_ALPHA = 1.702
SWIGLU_LIMIT = 7.0
LANES = 128
NEG = -1e30

VMEM_LIMIT = 56 * 1024 * 1024

QB, KB, VB, ZB, HQB, HFB, HIB, HGB = (i * HEADS for i in range(8))
MAIN_COLS = 8 * HEADS * HD

GDN_CHUNK = 128
GDN_TB = 512
HG_CHUNK = 16
HG_TB = 256
DEC_TB = 8
MOE_TM = 256
MOE_TN = 512


def _cparams(sem):
    return pltpu.CompilerParams(dimension_semantics=sem, vmem_limit_bytes=VMEM_LIMIT)


def _mm(a, b):
    return jnp.dot(a.astype(BF16), b.astype(BF16), preferred_element_type=F32)


def _mm_nt(a, b):
    return lax.dot_general(a.astype(BF16), b.astype(BF16), (((1,), (1,)), ((), ())),
                           preferred_element_type=F32)


def _mm_tn(a, b):
    return lax.dot_general(a.astype(BF16), b.astype(BF16), (((0,), (0,)), ((), ())),
                           preferred_element_type=F32)


def _mmh(a, b):
    return jnp.dot(a, b, precision=lax.Precision.HIGHEST, preferred_element_type=F32)


def _sigmoid(x):
    return 1.0 / (1.0 + jnp.exp(-x))


def _silu(x):
    return x * _sigmoid(x)


def _softplus(x):
    return jnp.maximum(x, 0.0) + jnp.log1p(jnp.exp(-jnp.abs(x)))


def _rms(x, w):
    return x * lax.rsqrt(jnp.mean(x * x, axis=-1, keepdims=True) + EPS) * w


def _l2n(x):
    return x * lax.rsqrt(jnp.sum(x * x, axis=-1, keepdims=True) + EPS)


def _in_proj_kernel(x_ref, nw_ref, w_ref, wba_ref, o_ref, ba_ref, h_scr):
    @pl.when(pl.program_id(1) == 0)
    def _():
        h = _rms(x_ref[...], nw_ref[...]).astype(BF16)
        h_scr[...] = h
        ba_ref[...] = jnp.dot(h, wba_ref[...], preferred_element_type=F32)

    o_ref[...] = jnp.dot(h_scr[...], w_ref[...], preferred_element_type=F32)


def _in_proj(x, norm_w, w_main, w_ba, tm, tn):
    T, D = x.shape
    N = w_main.shape[1]
    return pl.pallas_call(
        _in_proj_kernel,
        grid=(T // tm, N // tn),
        in_specs=[
            pl.BlockSpec((tm, D), lambda i, j: (i, 0)),
            pl.BlockSpec((1, D), lambda i, j: (0, 0)),
            pl.BlockSpec((D, tn), lambda i, j: (0, j)),
            pl.BlockSpec((D, LANES), lambda i, j: (0, 0)),
        ],
        out_specs=[
            pl.BlockSpec((tm, tn), lambda i, j: (i, j)),
            pl.BlockSpec((tm, LANES), lambda i, j: (i, 0)),
        ],
        out_shape=[jax.ShapeDtypeStruct((T, N), F32), jax.ShapeDtypeStruct((T, LANES), F32)],
        scratch_shapes=[pltpu.VMEM((tm, D), BF16)],
        compiler_params=_cparams(("parallel", "arbitrary")),
        name="in_proj",
    )(x, norm_w, w_main, w_ba)


def _unit_lower_inverse(A, ri, ci):
    n = A.shape[0]
    eye = (ri == ci).astype(F32)
    same16 = (ri // 16) == (ci // 16)
    N1 = jnp.where(same16, -A, 0.0)
    N2 = _mmh(N1, N1)
    N4 = _mmh(N2, N2)
    N8 = _mmh(N4, N4)
    T = eye + N1
    T = T + _mmh(T, N2)
    T = T + _mmh(T, N4)
    T = T + _mmh(T, N8)
    size = 32
    while size <= n:
        same = (ri // size) == (ci // size)
        same_half = (ri // (size // 2)) == (ci // (size // 2))
        L = jnp.where(same & (~same_half), A, 0.0)
        T = T - _mmh(_mmh(T, L), T)
        size *= 2
    return T


def _gdn_prompt_kernel(q_ref, k_ref, v_ref, z_ref, ba_ref, cwq_ref, cwk_ref, cwv_ref, alog_ref, dtb_ref,
                       gn_ref, o_ref, s_ref, ubuf, s_scr, *, tb, chunk):
    h = pl.program_id(1)
    t = pl.program_id(2)

    @pl.when(t == 0)
    def _():
        ubuf[:, 0:8, :] = jnp.zeros((3, 8, HD), F32)
        s_scr[...] = jnp.zeros((HD, HD), F32)

    ubuf[0, 8:8 + tb, :] = q_ref[...]
    ubuf[1, 8:8 + tb, :] = k_ref[...]
    ubuf[2, 8:8 + tb, :] = v_ref[...]

    ba = ba_ref[...]
    lane = lax.broadcasted_iota(I32, ba.shape, 1)
    b_col = jnp.sum(jnp.where(lane == h, ba, 0.0), axis=-1, keepdims=True)
    a_col = jnp.sum(jnp.where(lane == h + HEADS, ba, 0.0), axis=-1, keepdims=True)
    beta_col = _sigmoid(b_col)
    g_col = -jnp.exp(alog_ref[0][:, 0:1]) * _softplus(a_col + dtb_ref[0][:, 0:1])

    ri_t = lax.broadcasted_iota(I32, (tb, tb), 0)
    ci_t = lax.broadcasted_iota(I32, (tb, tb), 1)
    lblk = (((ri_t // chunk) == (ci_t // chunk)) & (ci_t <= ri_t)).astype(F32)
    gcb = _mmh(lblk, jnp.broadcast_to(g_col, (tb, HD)))
    gcb_t = gcb.T

    ri = lax.broadcasted_iota(I32, (chunk, chunk), 0)
    ci = lax.broadcasted_iota(I32, (chunk, chunk), 1)
    causal = ci <= ri
    strict = ci < ri

    def conv(idx, w_ref, r0):
        w = w_ref[...]
        acc = ubuf[idx, r0 + 5:r0 + 5 + chunk, :] * w[0:1, :]
        for j in range(1, CONV_K):
            acc = acc + ubuf[idx, r0 + 5 + j:r0 + 5 + j + chunk, :] * w[j:j + 1, :]
        return _silu(acc)

    S = s_scr[...]
    for c in range(tb // chunk):
        r0 = c * chunk
        q = _l2n(conv(0, cwq_ref, r0)) * (HD ** -0.5)
        k = _l2n(conv(1, cwk_ref, r0))
        v = conv(2, cwv_ref, r0)
        beta = beta_col[r0:r0 + chunk, :]
        gcol = gcb[r0:r0 + chunk, :]
        grow = gcb_t[:, r0:r0 + chunk]
        decay = jnp.exp(jnp.where(causal, gcol - grow, NEG))
        kb = k * beta
        A = jnp.where(strict, _mm_nt(kb, k) * decay, 0.0)
        T = _unit_lower_inverse(A, ri, ci)
        eg = jnp.exp(gcol)
        u = _mmh(T, v * beta)
        w = _mmh(T, kb * eg)
        v_new = u - _mm(w, S)
        scores = _mm_nt(q, k) * decay
        o = _mm(q * eg, S) + _mm(scores, v_new)
        glast = gcol[chunk - 1:chunk, :]
        S = S * jnp.exp(glast) + _mm_tn(k * jnp.exp(glast - gcol), v_new)
        o_ref[r0:r0 + chunk, :] = (_rms(o, gn_ref[...]) * _silu(z_ref[r0:r0 + chunk, :])).astype(o_ref.dtype)
    s_scr[...] = S

    ubuf[:, 0:8, :] = ubuf[:, tb:tb + 8, :]

    @pl.when(t == pl.num_programs(2) - 1)
    def _():
        s_ref[0, 0] = S


def _gdn_prompt(proj, ba, conv_w, alog_b, dtb_b, gdn_norm, B, L):
    tb, chunk = GDN_TB, GDN_CHUNK
    nt = L // tb
    row = lambda b, h, t: b * nt + t
    colspec = lambda base: pl.BlockSpec((tb, HD), lambda b, h, t: (row(b, h, t), base + h))
    cwspec = lambda base: pl.BlockSpec((CONV_K, HD), lambda b, h, t: (0, base + h))
    hvec = pl.BlockSpec((1, 1, HD), lambda b, h, t: (h, 0, 0))
    return pl.pallas_call(
        functools.partial(_gdn_prompt_kernel, tb=tb, chunk=chunk),
        grid=(B, HEADS, nt),
        in_specs=[
            colspec(QB), colspec(KB), colspec(VB), colspec(ZB),
            pl.BlockSpec((tb, LANES), lambda b, h, t: (row(b, h, t), 0)),
            cwspec(QB), cwspec(KB), cwspec(VB),
            hvec, hvec,
            pl.BlockSpec((1, HD), lambda b, h, t: (0, 0)),
        ],
        out_specs=[
            pl.BlockSpec((tb, HD), lambda b, h, t: (row(b, h, t), h)),
            pl.BlockSpec((1, 1, HD, HD), lambda b, h, t: (b, h, 0, 0)),
        ],
        out_shape=[jax.ShapeDtypeStruct((B * L, HEADS * HD), BF16),
                   jax.ShapeDtypeStruct((B, HEADS, HD, HD), F32)],
        scratch_shapes=[pltpu.VMEM((3, tb + 8, HD), F32), pltpu.VMEM((HD, HD), F32)],
        compiler_params=_cparams(("parallel", "parallel", "arbitrary")),
        name="gdn_prompt",
    )(proj, proj, proj, proj, ba, conv_w, conv_w, conv_w, alog_b, dtb_b, gdn_norm)


def _hgrn_prompt_kernel(hq_ref, hf_ref, hi_ref, hg_ref, lb_ref, hn_ref, o_ref, s_ref,
                        g_scr, q_scr, k_scr, st_scr, *, tb, chunk):
    t = pl.program_id(2)

    @pl.when(t == 0)
    def _():
        st_scr[...] = jnp.zeros((HD, HD), F32)

    lb = lb_ref[...]
    hf = hf_ref[...]
    forget = lb + (1.0 - lb) * _sigmoid(hf)
    k_scr[...] = (1.0 - lb) * _sigmoid(-hf)
    q_scr[...] = _silu(hq_ref[...])
    ri_t = lax.broadcasted_iota(I32, (tb, tb), 0)
    ci_t = lax.broadcasted_iota(I32, (tb, tb), 1)
    lblk = (((ri_t // chunk) == (ci_t // chunk)) & (ci_t <= ri_t)).astype(F32)
    g_scr[...] = _mmh(lblk, jnp.log(forget))

    row = lax.broadcasted_iota(I32, (chunk, HD), 0)
    lane16 = lax.broadcasted_iota(I32, (chunk, chunk), 1)
    hn = hn_ref[...]

    def body(c, st):
        r0 = pl.multiple_of(c * chunk, chunk)
        G = g_scr[pl.ds(r0, chunk), :]
        q = q_scr[pl.ds(r0, chunk), :]
        k = k_scr[pl.ds(r0, chunk), :]
        v = hi_ref[pl.ds(r0, chunk), :]
        sT = jnp.zeros((chunk, chunk), F32)
        for i in range(chunk):
            e = jnp.exp(jnp.where(row <= i, G[i:i + 1, :] - G, NEG))
            col = jnp.sum(e * k * q[i:i + 1, :], axis=-1, keepdims=True)
            sT = jnp.where(lane16 == i, col, sT)
        o = _mm_nt(q * jnp.exp(G), st) + _mm_tn(sT, v)
        glast = G[chunk - 1:chunk, :]
        st = st * jnp.exp(glast) + _mm_tn(v, k * jnp.exp(glast - G))
        o_ref[pl.ds(r0, chunk), :] = (_rms(o, hn) * _silu(hg_ref[pl.ds(r0, chunk), :])).astype(o_ref.dtype)
        return st

    st = lax.fori_loop(0, tb // chunk, body, st_scr[...])
    st_scr[...] = st

    @pl.when(t == pl.num_programs(2) - 1)
    def _():
        s_ref[0, 0] = st.T


def _hgrn_prompt(proj, lb, hg_norm, B, L):
    tb, chunk = HG_TB, HG_CHUNK
    nt = L // tb
    row = lambda b, h, t: b * nt + t
    colspec = lambda base: pl.BlockSpec((tb, HD), lambda b, h, t: (row(b, h, t), base + h))
    return pl.pallas_call(
        functools.partial(_hgrn_prompt_kernel, tb=tb, chunk=chunk),
        grid=(B, HEADS, nt),
        in_specs=[
            colspec(HQB), colspec(HFB), colspec(HIB), colspec(HGB),
            pl.BlockSpec((1, HD), lambda b, h, t: (0, h)),
            pl.BlockSpec((1, HD), lambda b, h, t: (0, 0)),
        ],
        out_specs=[
            pl.BlockSpec((tb, HD), lambda b, h, t: (row(b, h, t), h)),
            pl.BlockSpec((1, 1, HD, HD), lambda b, h, t: (b, h, 0, 0)),
        ],
        out_shape=[jax.ShapeDtypeStruct((B * L, HEADS * HD), BF16),
                   jax.ShapeDtypeStruct((B, HEADS, HD, HD), F32)],
        scratch_shapes=[pltpu.VMEM((tb, HD), F32), pltpu.VMEM((tb, HD), F32), pltpu.VMEM((tb, HD), F32),
                        pltpu.VMEM((HD, HD), F32)],
        compiler_params=_cparams(("parallel", "parallel", "arbitrary")),
        name="hgrn_prompt",
    )(proj, proj, proj, proj, lb, hg_norm)


def _stack_heads_t(x, nb):
    rows = [x[:, h * HD:(h + 1) * HD] for h in range(HEADS)]
    pad = LANES - HEADS * nb
    if pad:
        rows.append(jnp.zeros((pad, HD), F32))
    return jnp.concatenate(rows, axis=0).T


def _decode_kernel(qkv_ref, z_ref, hq_ref, hf_ref, hi_ref, hg_ref, ba_ref, cs_ref, sg_ref, sh_ref,
                   cw_ref, alog_ref, dtb_ref, lb_ref, gn_ref, hn_ref,
                   oa_ref, ob_ref, sgo_ref, sho_ref, *, nb):
    cd = cw_ref.shape[1]
    cw = cw_ref[...]
    cs = cs_ref[...]
    acc = cs[:, 0:cd] * cw[0:1, :]
    acc = acc + cs[:, cd:2 * cd] * cw[1:2, :]
    acc = acc + cs[:, 2 * cd:3 * cd] * cw[2:3, :]
    acc = acc + qkv_ref[...] * cw[3:4, :]
    conv = _silu(acc)
    nqk = HEADS * HD
    ba = ba_ref[...]
    gn = gn_ref[...]
    hn = hn_ref[...]
    z = z_ref[...]
    hgate = hg_ref[...]

    qs, ks, vs = [], [], []
    for h in range(HEADS):
        qs.append(_l2n(conv[:, h * HD:(h + 1) * HD]) * (HD ** -0.5))
        ks.append(_l2n(conv[:, nqk + h * HD:nqk + (h + 1) * HD]))
        vs.append(conv[:, 2 * nqk + h * HD:2 * nqk + (h + 1) * HD])
    qT = _stack_heads_t(jnp.concatenate(qs, axis=1), nb)
    kT = _stack_heads_t(jnp.concatenate(ks, axis=1), nb)

    lb = lb_ref[...]
    hf = hf_ref[...]
    forget = lb + (1.0 - lb) * _sigmoid(hf)
    hk = (1.0 - lb) * _sigmoid(-hf)
    hq = _silu(hq_ref[...])
    hv = hi_ref[...]
    fT = _stack_heads_t(forget, nb)
    hkT = _stack_heads_t(hk, nb)
    hqT = _stack_heads_t(hq, nb)

    for h in range(HEADS):
        beta = _sigmoid(ba[:, h:h + 1])
        g = -jnp.exp(alog_ref[0:1, h:h + 1]) * _softplus(ba[:, HEADS + h:HEADS + h + 1] + dtb_ref[0:1, h:h + 1])
        eg = jnp.exp(g)
        qk = jnp.sum(qs[h] * ks[h], axis=-1, keepdims=True)
        for b in range(nb):
            idx = h * nb + b
            S = sg_ref[b, h]
            kcol = kT[:, idx:idx + 1]
            qcol = qT[:, idx:idx + 1]
            kS = jnp.sum(S * kcol, axis=0, keepdims=True)
            qS = jnp.sum(S * qcol, axis=0, keepdims=True)
            egb = eg[b:b + 1, :]
            v_new = beta[b:b + 1, :] * (vs[h][b:b + 1, :] - egb * kS)
            o = egb * qS + qk[b:b + 1, :] * v_new
            sgo_ref[b, h] = S * egb + kcol * v_new
            oa_ref[b:b + 1, h * HD:(h + 1) * HD] = _rms(o, gn) * _silu(z[b:b + 1, h * HD:(h + 1) * HD])
            Sh = sh_ref[b, h] * fT[:, idx:idx + 1] + hkT[:, idx:idx + 1] * hv[b:b + 1, h * HD:(h + 1) * HD]
            sho_ref[b, h] = Sh
            ob = jnp.sum(Sh * hqT[:, idx:idx + 1], axis=0, keepdims=True)
            ob_ref[b:b + 1, h * HD:(h + 1) * HD] = _rms(ob, hn) * _silu(hgate[b:b + 1, h * HD:(h + 1) * HD])


def _decode(proj, ba, conv_state2d, s_gdn, s_hg, conv_w, alog_row, dtb_row, lb, gdn_norm, hg_norm):
    nb = DEC_TB
    T = proj.shape[0]
    cd = conv_w.shape[1]
    w = HEADS * HD
    cblk = lambda width, idx: pl.BlockSpec((nb, width), lambda i: (i, idx))
    sblk = pl.BlockSpec((nb, HEADS, HD, HD), lambda i: (i, 0, 0, 0))
    full = lambda shape: pl.BlockSpec(shape, lambda i: (0,) * len(shape))
    return pl.pallas_call(
        functools.partial(_decode_kernel, nb=nb),
        grid=(T // nb,),
        in_specs=[
            cblk(cd, 0), cblk(w, ZB // HEADS), cblk(w, HQB // HEADS), cblk(w, HFB // HEADS),
            cblk(w, HIB // HEADS), cblk(w, HGB // HEADS),
            cblk(LANES, 0), cblk(3 * cd, 0), sblk, sblk,
            full((CONV_K, cd)), full((1, LANES)), full((1, LANES)), full((1, w)), full((1, HD)), full((1, HD)),
        ],
        out_specs=[cblk(w, 0), cblk(w, 0), sblk, sblk],
        out_shape=[jax.ShapeDtypeStruct((T, w), F32), jax.ShapeDtypeStruct((T, w), F32),
                   jax.ShapeDtypeStruct(s_gdn.shape, F32), jax.ShapeDtypeStruct(s_hg.shape, F32)],
        compiler_params=_cparams(("parallel",)),
        name="decode",
    )(proj, proj, proj, proj, proj, proj, ba, conv_state2d, s_gdn, s_hg,
      conv_w, alog_row, dtb_row, lb, gdn_norm, hg_norm)


def _out_router_kernel(x_ref, oa_ref, ob_ref, woa_ref, wob_ref, n2_ref, wr_ref, br_ref,
                       x1_ref, h2_ref, ti_ref, tw_ref):
    y = x_ref[...] + jnp.dot(oa_ref[...].astype(BF16), woa_ref[...], preferred_element_type=F32)
    y = y + jnp.dot(ob_ref[...].astype(BF16), wob_ref[...], preferred_element_type=F32)
    x1_ref[...] = y
    h2 = _rms(y, n2_ref[...])
    h2_ref[...] = h2.astype(BF16)
    logits = _mmh(h2, wr_ref[...]) + br_ref[...]
    lane = lax.broadcasted_iota(I32, logits.shape, 1)
    logits = jnp.where(lane < N_EXPERTS, logits, NEG)
    ti = jnp.zeros(logits.shape, I32)
    tw = jnp.zeros(logits.shape, F32)
    m0 = None
    for kk in range(TOP_K):
        m = jnp.max(logits, axis=-1, keepdims=True)
        idx = jnp.min(jnp.where(logits == m, lane, LANES), axis=-1, keepdims=True)
        if m0 is None:
            m0 = m
        ti = jnp.where(lane == kk, idx, ti)
        tw = jnp.where(lane == kk, jnp.exp(m - m0), tw)
        logits = jnp.where(lane == idx, NEG * 2.0, logits)
    tw_ref[...] = tw / jnp.sum(tw, axis=-1, keepdims=True)
    ti_ref[...] = ti


def _out_router(x, oa, ob, wo_a, wo_b, norm2, w_router_p, b_router_p, tm):
    T, D = x.shape
    w = oa.shape[1]
    rowblk = lambda width: pl.BlockSpec((tm, width), lambda i: (i, 0))
    full = lambda shape: pl.BlockSpec(shape, lambda i: (0,) * len(shape))
    return pl.pallas_call(
        _out_router_kernel,
        grid=(T // tm,),
        in_specs=[rowblk(D), rowblk(w), rowblk(w), full((w, D)), full((w, D)), full((1, D)),
                  full((D, LANES)), full((1, LANES))],
        out_specs=[rowblk(D), rowblk(D), rowblk(LANES), rowblk(LANES)],
        out_shape=[jax.ShapeDtypeStruct((T, D), F32), jax.ShapeDtypeStruct((T, D), BF16),
                   jax.ShapeDtypeStruct((T, LANES), I32), jax.ShapeDtypeStruct((T, LANES), F32)],
        compiler_params=_cparams(("parallel",)),
        name="out_router",
    )(x, oa, ob, wo_a, wo_b, norm2, w_router_p, b_router_p)


def _moe_gate_up_kernel(te_ref, tv_ref, x_ref, wg_ref, wu_ref, bg_ref, bu_ref, act_ref):
    @pl.when(tv_ref[pl.program_id(1)] != 0)
    def _():
        x = x_ref[...]
        gate = jnp.dot(x, wg_ref[0], preferred_element_type=F32) + bg_ref[0]
        up = jnp.dot(x, wu_ref[0], preferred_element_type=F32) + bu_ref[0]
        gate = jnp.minimum(gate, SWIGLU_LIMIT)
        up = jnp.clip(up, -SWIGLU_LIMIT, SWIGLU_LIMIT)
        act_ref[...] = ((up + 1.0) * (gate * _sigmoid(gate * SWIGLU_ALPHA))).astype(act_ref.dtype)


def _moe_gate_up(te, tv, xs, wg, wu, bg, bu):
    tm, tn = MOE_TM, MOE_TN
    P, D = xs.shape
    F = wg.shape[2]
    return pl.pallas_call(
        _moe_gate_up_kernel,
        grid_spec=pltpu.PrefetchScalarGridSpec(
            num_scalar_prefetch=2,
            grid=(F // tn, P // tm),
            in_specs=[
                pl.BlockSpec((tm, D), lambda n, m, te, tv: (m, 0)),
                pl.BlockSpec((1, D, tn), lambda n, m, te, tv: (te[m], 0, n)),
                pl.BlockSpec((1, D, tn), lambda n, m, te, tv: (te[m], 0, n)),
                pl.BlockSpec((1, 1, tn), lambda n, m, te, tv: (te[m], 0, n)),
                pl.BlockSpec((1, 1, tn), lambda n, m, te, tv: (te[m], 0, n)),
            ],
            out_specs=pl.BlockSpec((tm, tn), lambda n, m, te, tv: (m, n)),
        ),
        out_shape=jax.ShapeDtypeStruct((P, F), BF16),
        compiler_params=_cparams(("parallel", "arbitrary")),
        name="moe_gate_up",
    )(te, tv, xs, wg, wu, bg, bu)


def _moe_down_kernel(te_ref, tv_ref, a_ref, wd_ref, bd_ref, y_ref):
    @pl.when(tv_ref[pl.program_id(1)] != 0)
    def _():
        y_ref[...] = jnp.dot(a_ref[...], wd_ref[0], preferred_element_type=F32) + bd_ref[0]


def _moe_down(te, tv, act, wd, bd):
    tm, tn = MOE_TM, MOE_TN
    P, F = act.shape
    D = wd.shape[2]
    return pl.pallas_call(
        _moe_down_kernel,
        grid_spec=pltpu.PrefetchScalarGridSpec(
            num_scalar_prefetch=2,
            grid=(D // tn, P // tm),
            in_specs=[
                pl.BlockSpec((tm, F), lambda n, m, te, tv: (m, 0)),
                pl.BlockSpec((1, F, tn), lambda n, m, te, tv: (te[m], 0, n)),
                pl.BlockSpec((1, 1, tn), lambda n, m, te, tv: (te[m], 0, n)),
            ],
            out_specs=pl.BlockSpec((tm, tn), lambda n, m, te, tv: (m, n)),
        ),
        out_shape=jax.ShapeDtypeStruct((P, D), F32),
        compiler_params=_cparams(("parallel", "arbitrary")),
        name="moe_down",
    )(te, tv, act, wd, bd)


def _combine_kernel(x1_ref, yg_ref, tw_ref, fn_ref, y_ref):
    D = x1_ref.shape[1]
    tw = tw_ref[...]
    acc = x1_ref[...]
    moe = tw[:, 0:1] * yg_ref[:, 0:D]
    for kk in range(1, TOP_K):
        moe = moe + tw[:, kk:kk + 1] * yg_ref[:, kk * D:(kk + 1) * D]
    y_ref[...] = _rms(acc + moe, fn_ref[...])


def _combine(x1, yg, tw, final_norm, tm, row_off):
    T, D = x1.shape
    off = row_off // tm
    return pl.pallas_call(
        _combine_kernel,
        grid=(T // tm,),
        in_specs=[
            pl.BlockSpec((tm, D), lambda i: (i, 0)),
            pl.BlockSpec((tm, TOP_K * D), lambda i: (i + off, 0)),
            pl.BlockSpec((tm, LANES), lambda i: (i, 0)),
            pl.BlockSpec((1, D), lambda i: (0, 0)),
        ],
        out_specs=pl.BlockSpec((tm, D), lambda i: (i, 0)),
        out_shape=jax.ShapeDtypeStruct((T, D), F32),
        compiler_params=_cparams(("parallel",)),
        name="combine",
    )(x1, yg, tw, final_norm)


def _routing(top_i):
    T = top_i.shape[0]
    P = T * TOP_K
    tm = MOE_TM
    n_tiles = -(-P // tm) + N_EXPERTS
    e_flat = top_i.reshape(P)
    onehot = (e_flat[:, None] == jnp.arange(N_EXPERTS, dtype=I32)[None, :]).astype(I32)
    csum = jnp.cumsum(onehot, axis=0)
    rank = jnp.sum((csum - 1) * onehot, axis=1)
    counts = csum[-1]
    padded = ((counts + tm - 1) // tm) * tm
    ends = jnp.cumsum(padded)
    starts = ends - padded
    pos = starts[e_flat] + rank
    slot_token = jnp.zeros((n_tiles * tm,), I32).at[pos].set(jnp.arange(P, dtype=I32) // TOP_K)
    tile_start = jnp.arange(n_tiles, dtype=I32) * tm
    tile_valid = (tile_start < ends[-1]).astype(I32)
    tile_expert = jnp.minimum(jnp.sum((tile_start[:, None] >= ends[None, :]).astype(I32), axis=1), N_EXPERTS - 1)
    return pos, slot_token, tile_expert.astype(I32), tile_valid


def kernel(x_prompt, x_sample, state_gdn, state_conv, state_hgrn, lb_table, norm1, w_in, conv_w, A_log, dt_bias,
           gdn_norm, hg_norm, w_out, norm2, w_router, b_router, w_gate_up, b_gate_up, w_down, b_down, final_norm):
    depth = w_in.shape[0]
    assert depth == 1
    B, L, D = x_prompt.shape
    SB = x_sample.shape[0]
    assert x_sample.shape[1] == 1
    nqk = HEADS * HD
    cd = 3 * nqk

    lbs = jnp.cumsum(jax.nn.softmax(lb_table.astype(F32), axis=0), axis=0)
    lb = lbs[0:1]

    w0 = w_in[0]
    ba0 = cd + nqk
    w_main = jnp.concatenate([w0[:, :ba0], w0[:, ba0 + 2 * HEADS:]], axis=1).astype(BF16)
    w_ba = jnp.pad(w0[:, ba0:ba0 + 2 * HEADS], ((0, 0), (0, LANES - 2 * HEADS))).astype(BF16)
    wo_a = w_out[0, :nqk].astype(BF16)
    wo_b = w_out[0, nqk:].astype(BF16)
    wr_p = jnp.pad(w_router[0], ((0, 0), (0, LANES - N_EXPERTS)))
    br_p = jnp.pad(b_router[0], (0, LANES - N_EXPERTS))[None, :]
    wg = w_gate_up[0, :, :, 0::2].astype(BF16)
    wu = w_gate_up[0, :, :, 1::2].astype(BF16)
    bg = b_gate_up[0, :, None, 0::2]
    bu = b_gate_up[0, :, None, 1::2]
    wd = w_down[0].astype(BF16)
    bd = b_down[0][:, None, :]
    alog_b = jnp.broadcast_to(A_log[0][:, None, None], (HEADS, 1, HD))
    dtb_b = jnp.broadcast_to(dt_bias[0][:, None, None], (HEADS, 1, HD))
    alog_row = jnp.pad(A_log[0], (0, LANES - HEADS))[None, :]
    dtb_row = jnp.pad(dt_bias[0], (0, LANES - HEADS))[None, :]
    n1 = norm1[0][None, :]
    n2 = norm2[0][None, :]
    gn = gdn_norm[0][None, :]
    hn = hg_norm[0][None, :]
    fnw = final_norm[None, :]
    cw = conv_w[0]

    xp = x_prompt.reshape(B * L, D)
    xs = x_sample.reshape(SB, D)

    proj_p, ba_p = _in_proj(xp, n1, w_main, w_ba, tm=1024, tn=1024)
    proj_s, ba_s = _in_proj(xs, n1, w_main, w_ba, tm=SB, tn=1024)

    oa_p, sg_p = _gdn_prompt(proj_p, ba_p, cw, alog_b, dtb_b, gn, B, L)
    ob_p, sh_p = _hgrn_prompt(proj_p, lb, hn, B, L)
    oa_s, ob_s, sg_s, sh_s = _decode(proj_s, ba_s, state_conv[0].reshape(SB, (CONV_K - 1) * cd), state_gdn[0],
                                     state_hgrn[0], cw, alog_row, dtb_row, lb, gn, hn)

    x1_p, h2_p, ti_p, tw_p = _out_router(xp, oa_p, ob_p, wo_a, wo_b, n2, wr_p, br_p, tm=256)
    x1_s, h2_s, ti_s, tw_s = _out_router(xs, oa_s, ob_s, wo_a, wo_b, n2, wr_p, br_p, tm=SB)

    h2 = jnp.concatenate([h2_p, h2_s], axis=0)
    top_i = jnp.concatenate([ti_p, ti_s], axis=0)[:, :TOP_K]
    pos, slot_token, tile_expert, tile_valid = _routing(top_i)
    xs_sorted = jnp.take(h2, slot_token, axis=0)
    act = _moe_gate_up(tile_expert, tile_valid, xs_sorted, wg, wu, bg, bu)
    yslots = _moe_down(tile_expert, tile_valid, act, wd, bd)
    yg = jnp.take(yslots, pos, axis=0).reshape(B * L + SB, TOP_K * D)

    y_p = _combine(x1_p, yg, tw_p, fnw, tm=256, row_off=0)
    y_s = _combine(x1_s, yg, tw_s, fnw, tm=SB, row_off=B * L)

    conv_p = proj_p.reshape(B, L, MAIN_COLS)[:, L - (CONV_K - 1):, :cd]
    conv_s = jnp.concatenate([state_conv[0][:, 1:, :], proj_s[:, None, :cd]], axis=1)
    return (y_p.reshape(B, L, D), y_s.reshape(SB, 1, D),
            sg_p[None], conv_p[None].astype(state_conv.dtype), sh_p[None],
            sg_s[None], conv_s[None].astype(state_conv.dtype), sh_s[None])
```

```python
import functools

import jax
import jax.numpy as jnp
from jax import lax
from jax.experimental import pallas as pl
from jax.experimental.pallas import tpu as pltpu

F32 = jnp.float32
BF16 = jnp.bfloat16
I32 = jnp.int32

EPS = 1e-6
HEADS = 8
HD = 128
CONV_K = 4
N_EXPERTS = 32
TOP_K = 4
SWIGLU_ALPHA = 1.702
SWIGLU_LIMIT = 7.0
LANES = 128
NEG = -1e30

VMEM_LIMIT = 56 * 1024 * 1024

QB, KB, VB, ZB, HQB, HFB, HIB, HGB = (i * HEADS for i in range(8))
MAIN_COLS = 8 * HEADS * HD

GDN_CHUNK = 128
GDN_TB = 512
HG_CHUNK = 16
HG_TB = 256
HG_GROUP = 4
DEC_TB = 8
MOE_TM = 256
MOE_TN = 512
MOE_TNW = 1024


def _cparams(sem):
    return pltpu.CompilerParams(dimension_semantics=sem, vmem_limit_bytes=VMEM_LIMIT)


def _mm(a, b):
    return jnp.dot(a.astype(BF16), b.astype(BF16), preferred_element_type=F32)


def _mm_nt(a, b):
    return lax.dot_general(a.astype(BF16), b.astype(BF16), (((1,), (1,)), ((), ())),
                           preferred_element_type=F32)


def _mm_tn(a, b):
    return lax.dot_general(a.astype(BF16), b.astype(BF16), (((0,), (0,)), ((), ())),
                           preferred_element_type=F32)


def _mmh(a, b):
    return jnp.dot(a, b, precision=lax.Precision.HIGHEST, preferred_element_type=F32)


def _split(a):
    hi = a.astype(BF16)
    return hi, (a - hi.astype(F32)).astype(BF16)


def _mm3(a, b):
    d = lambda x, y: jnp.dot(x, y, preferred_element_type=F32)
    return d(a[0], b[0]) + (d(a[0], b[1]) + d(a[1], b[0]))


def _mm_exact_lhs(l_bf16, x):
    d = lambda y: jnp.dot(l_bf16, y, preferred_element_type=F32)
    x0 = x.astype(BF16)
    r1 = x - x0.astype(F32)
    x1 = r1.astype(BF16)
    x2 = (r1 - x1.astype(F32)).astype(BF16)
    return d(x0) + (d(x1) + d(x2))


def _sigmoid(x):
    return 1.0 / (1.0 + jnp.exp(-x))


def _silu(x):
    return x * _sigmoid(x)


def _softplus(x):
    return jnp.maximum(x, 0.0) + jnp.log1p(jnp.exp(-jnp.abs(x)))


def _rms(x, w):
    return x * lax.rsqrt(jnp.mean(x * x, axis=-1, keepdims=True) + EPS) * w


def _l2n(x):
    return x * lax.rsqrt(jnp.sum(x * x, axis=-1, keepdims=True) + EPS)


def _repack_kernel(wa_ref, wb_ref, wt_ref, wm_ref, wba_ref, *, nba):
    half = wa_ref.shape[1]
    wm_ref[:, 0:half] = wa_ref[...].astype(BF16)
    win = jnp.concatenate([wb_ref[...], wt_ref[...]], axis=1)
    wm_ref[:, half:2 * half] = win[:, nba:nba + half].astype(BF16)
    first = wb_ref[:, 0:LANES]
    lane = lax.broadcasted_iota(I32, first.shape, 1)
    wba_ref[...] = jnp.where(lane < nba, first, 0.0).astype(BF16)


def _repack_w_in(w, nba):
    D, n = w.shape
    half = (n - nba) // 2
    assert half % LANES == 0 and nba <= LANES
    tr = 256
    return pl.pallas_call(
        functools.partial(_repack_kernel, nba=nba),
        grid=(D // tr,),
        in_specs=[
            pl.BlockSpec((tr, half), lambda i: (i, 0)),
            pl.BlockSpec((tr, half), lambda i: (i, 1)),
            pl.BlockSpec((tr, LANES), lambda i: (i, 2 * half // LANES)),
        ],
        out_specs=[pl.BlockSpec((tr, 2 * half), lambda i: (i, 0)), pl.BlockSpec((tr, LANES), lambda i: (i, 0))],
        out_shape=[jax.ShapeDtypeStruct((D, 2 * half), BF16), jax.ShapeDtypeStruct((D, LANES), BF16)],
        compiler_params=_cparams(("parallel",)),
        name="repack_w_in",
    )(w, w, w)


def _in_proj_kernel(x_ref, nw_ref, w_ref, wba_ref, o_ref, ba_ref, h_scr):
    @pl.when(pl.program_id(1) == 0)
    def _():
        h = _rms(x_ref[...], nw_ref[...]).astype(BF16)
        h_scr[...] = h
        ba_ref[...] = jnp.dot(h, wba_ref[...], preferred_element_type=F32)

    o_ref[...] = jnp.dot(h_scr[...], w_ref[...], preferred_element_type=F32)


def _in_proj(x, norm_w, w_main, w_ba, tm, tn):
    T, D = x.shape
    N = w_main.shape[1]
    return pl.pallas_call(
        _in_proj_kernel,
        grid=(T // tm, N // tn),
        in_specs=[
            pl.BlockSpec((tm, D), lambda i, j: (i, 0)),
            pl.BlockSpec((1, D), lambda i, j: (0, 0)),
            pl.BlockSpec((D, tn), lambda i, j: (0, j)),
            pl.BlockSpec((D, LANES), lambda i, j: (0, 0)),
        ],
        out_specs=[
            pl.BlockSpec((tm, tn), lambda i, j: (i, j)),
            pl.BlockSpec((tm, LANES), lambda i, j: (i, 0)),
        ],
        out_shape=[jax.ShapeDtypeStruct((T, N), F32), jax.ShapeDtypeStruct((T, LANES), F32)],
        scratch_shapes=[pltpu.VMEM((tm, D), BF16)],
        compiler_params=_cparams(("parallel", "arbitrary")),
        name="in_proj",
    )(x, norm_w, w_main, w_ba)


def _unit_lower_inverse(As, ri, ci):
    n = As[0].shape[0]
    eye = (ri == ci).astype(F32)
    same16 = (ri // 16) == (ci // 16)
    s1 = [_split(jnp.where(same16, -A, 0.0)) for A in As]
    s2 = [_split(_mm3(s, s)) for s in s1]
    s4 = [_split(_mm3(s, s)) for s in s2]
    s8 = [_split(_mm3(s, s)) for s in s4]
    Ts = [eye + jnp.where(same16, -A, 0.0) for A in As]
    for sp in (s2, s4, s8):
        Ts = [T + _mm3(_split(T), s) for T, s in zip(Ts, sp)]
    size = 32
    while size <= n:
        off = ((ri // size) == (ci // size)) & ((ri // (size // 2)) != (ci // (size // 2)))
        sT = [_split(T) for T in Ts]
        TL = [_mm3(st, _split(jnp.where(off, A, 0.0))) for st, A in zip(sT, As)]
        Ts = [T - _mm3(_split(tl), st) for T, tl, st in zip(Ts, TL, sT)]
        size *= 2
    return Ts


def _gdn_prompt_kernel(q_ref, k_ref, v_ref, z_ref, ba_ref, cwq_ref, cwk_ref, cwv_ref, alog_ref, dtb_ref,
                       gn_ref, o_ref, s_ref, ubuf, s_scr, *, tb, chunk):
    h = pl.program_id(1)
    t = pl.program_id(2)

    @pl.when(t == 0)
    def _():
        ubuf[:, 0:8, :] = jnp.zeros((3, 8, HD), F32)
        s_scr[...] = jnp.zeros((HD, HD), F32)

    ubuf[0, 8:8 + tb, :] = q_ref[...]
    ubuf[1, 8:8 + tb, :] = k_ref[...]
    ubuf[2, 8:8 + tb, :] = v_ref[...]

    ba = ba_ref[...]
    lane = lax.broadcasted_iota(I32, ba.shape, 1)
    b_col = jnp.sum(jnp.where(lane == h, ba, 0.0), axis=-1, keepdims=True)
    a_col = jnp.sum(jnp.where(lane == h + HEADS, ba, 0.0), axis=-1, keepdims=True)
    beta_col = _sigmoid(b_col)
    g_col = -jnp.exp(alog_ref[0][:, 0:1]) * _softplus(a_col + dtb_ref[0][:, 0:1])

    ri = lax.broadcasted_iota(I32, (chunk, chunk), 0)
    ci = lax.broadcasted_iota(I32, (chunk, chunk), 1)
    causal = ci <= ri
    strict = ci < ri
    ltri = jnp.where(causal, 1.0, 0.0).astype(BF16)

    def conv(idx, w_ref, r0):
        w = w_ref[...]
        acc = ubuf[idx, r0 + 5:r0 + 5 + chunk, :] * w[0:1, :]
        for j in range(1, CONV_K):
            acc = acc + ubuf[idx, r0 + 5 + j:r0 + 5 + j + chunk, :] * w[j:j + 1, :]
        return _silu(acc)

    nc = tb // chunk
    qs, ks, vs, betas, gcols, decays, As = [], [], [], [], [], [], []
    for c in range(nc):
        r0 = c * chunk
        q = _l2n(conv(0, cwq_ref, r0)) * (HD ** -0.5)
        k = _l2n(conv(1, cwk_ref, r0))
        v = conv(2, cwv_ref, r0)
        beta = beta_col[r0:r0 + chunk, :]
        gcol = _mm_exact_lhs(ltri, jnp.broadcast_to(g_col[r0:r0 + chunk, :], (chunk, HD)))
        decay = jnp.exp(jnp.where(causal, gcol - gcol.T, NEG))
        kb = k * beta
        qs.append(q); ks.append(k); vs.append(v); betas.append(beta); gcols.append(gcol); decays.append(decay)
        As.append(jnp.where(strict, _mm_nt(kb, k) * decay, 0.0))
    Ts = _unit_lower_inverse(As, ri, ci)
    us, ws, scs = [], [], []
    for c in range(nc):
        sT = _split(Ts[c])
        kb = ks[c] * betas[c]
        us.append(_mm3(sT, _split(vs[c] * betas[c])))
        ws.append(_mm3(sT, _split(kb * jnp.exp(gcols[c]))))
        scs.append(_mm_nt(qs[c], ks[c]) * decays[c])

    S = s_scr[...]
    for c in range(nc):
        r0 = c * chunk
        gcol = gcols[c]
        v_new = us[c] - _mm(ws[c], S)
        o = _mm(qs[c] * jnp.exp(gcol), S) + _mm(scs[c], v_new)
        glast = gcol[chunk - 1:chunk, :]
        S = S * jnp.exp(glast) + _mm_tn(ks[c] * jnp.exp(glast - gcol), v_new)
        o_ref[r0:r0 + chunk, :] = (_rms(o, gn_ref[...]) * _silu(z_ref[r0:r0 + chunk, :])).astype(o_ref.dtype)
    s_scr[...] = S

    ubuf[:, 0:8, :] = ubuf[:, tb:tb + 8, :]

    @pl.when(t == pl.num_programs(2) - 1)
    def _():
        s_ref[0, 0] = S


def _gdn_prompt(proj, ba, conv_w, alog_b, dtb_b, gdn_norm, B, L):
    tb, chunk = GDN_TB, GDN_CHUNK
    nt = L // tb
    row = lambda b, h, t: b * nt + t
    colspec = lambda base: pl.BlockSpec((tb, HD), lambda b, h, t: (row(b, h, t), base + h))
    cwspec = lambda base: pl.BlockSpec((CONV_K, HD), lambda b, h, t: (0, base + h))
    hvec = pl.BlockSpec((1, 1, HD), lambda b, h, t: (h, 0, 0))
    return pl.pallas_call(
        functools.partial(_gdn_prompt_kernel, tb=tb, chunk=chunk),
        grid=(B, HEADS, nt),
        in_specs=[
            colspec(QB), colspec(KB), colspec(VB), colspec(ZB),
            pl.BlockSpec((tb, LANES), lambda b, h, t: (row(b, h, t), 0)),
            cwspec(QB), cwspec(KB), cwspec(VB),
            hvec, hvec,
            pl.BlockSpec((1, HD), lambda b, h, t: (0, 0)),
        ],
        out_specs=[
            pl.BlockSpec((tb, HD), lambda b, h, t: (row(b, h, t), h)),
            pl.BlockSpec((1, 1, HD, HD), lambda b, h, t: (b, h, 0, 0)),
        ],
        out_shape=[jax.ShapeDtypeStruct((B * L, HEADS * HD), BF16),
                   jax.ShapeDtypeStruct((B, HEADS, HD, HD), F32)],
        scratch_shapes=[pltpu.VMEM((3, tb + 8, HD), F32), pltpu.VMEM((HD, HD), F32)],
        compiler_params=_cparams(("parallel", "parallel", "arbitrary")),
        name="gdn_prompt",
    )(proj, proj, proj, proj, ba, conv_w, conv_w, conv_w, alog_b, dtb_b, gdn_norm)


def _hgrn_prompt_kernel(hq_ref, hf_ref, hi_ref, hg_ref, lb_ref, hn_ref, o_ref, s_ref,
                        g_scr, q_scr, k_scr, st_scr, *, tb, chunk, group):
    t = pl.program_id(2)

    @pl.when(t == 0)
    def _():
        st_scr[...] = jnp.zeros((HD, HD), F32)

    lb = lb_ref[...]
    hf = hf_ref[...]
    forget = lb + (1.0 - lb) * _sigmoid(hf)
    k_scr[...] = (1.0 - lb) * _sigmoid(-hf)
    q_scr[...] = _silu(hq_ref[...])
    ri = lax.broadcasted_iota(I32, (LANES, LANES), 0)
    ci = lax.broadcasted_iota(I32, (LANES, LANES), 1)
    lblk = jnp.where(((ri // chunk) == (ci // chunk)) & (ci <= ri), 1.0, 0.0).astype(BF16)
    lf = jnp.log(forget)
    for r in range(tb // LANES):
        g_scr[r * LANES:(r + 1) * LANES, :] = _mm_exact_lhs(lblk, lf[r * LANES:(r + 1) * LANES, :])

    half = chunk // 2
    row8 = lax.broadcasted_iota(I32, (half, HD), 0)
    lane8 = lax.broadcasted_iota(I32, (half, HD), 1)
    zpad = jnp.zeros((half, HD), F32)
    hn = hn_ref[...]

    def body(gi, st):
        base = gi * (group * chunk)
        cks = range(group)
        vs = [hi_ref[pl.ds(base + cc * chunk, chunk), :] for cc in cks]
        vts = [v.T for v in vs]
        blocks = []
        for n in range(2 * group):
            r0 = base + n * half
            blocks.append((g_scr[pl.ds(r0, half), :], q_scr[pl.ds(r0, half), :], k_scr[pl.ds(r0, half), :]))
        accs = [zpad] * (2 * group)
        for j in range(half):
            for n, (Gb, qb, kb) in enumerate(blocks):
                e = jnp.exp(jnp.where(row8 >= j, Gb - Gb[j:j + 1, :], NEG))
                col = jnp.sum(e * qb * kb[j:j + 1, :], axis=-1, keepdims=True)
                accs[n] = jnp.where(lane8 == (n % 2) * half + j, col, accs[n])
        Gs = [jnp.concatenate([blocks[2 * cc][0], blocks[2 * cc + 1][0]], axis=0) for cc in cks]
        qs = [jnp.concatenate([blocks[2 * cc][1], blocks[2 * cc + 1][1]], axis=0) for cc in cks]
        ks = [jnp.concatenate([blocks[2 * cc][2], blocks[2 * cc + 1][2]], axis=0) for cc in cks]
        offd = []
        for cc in cks:
            Ga, _, ka = blocks[2 * cc]
            Gb, qb, _ = blocks[2 * cc + 1]
            gmid = Ga[half - 1:half, :]
            kh = jnp.concatenate([ka * jnp.exp(gmid - Ga), zpad], axis=0)
            offd.append(_mm_nt(qb * jnp.exp(Gb - gmid), kh))
        glasts = [Gs[cc][chunk - 1:chunk, :] for cc in cks]
        incs = [_mm(vts[cc], ks[cc] * jnp.exp(glasts[cc] - Gs[cc])) for cc in cks]
        intra = []
        for cc in cks:
            s = jnp.concatenate([accs[2 * cc][:, 0:chunk], accs[2 * cc + 1][:, 0:chunk] + offd[cc]], axis=0)
            intra.append(_mm(s, vs[cc]))
        sts = []
        for cc in cks:
            sts.append(st)
            st = st * jnp.exp(glasts[cc]) + incs[cc]
        outs = [_mm_nt(qs[cc] * jnp.exp(Gs[cc]), sts[cc]) + intra[cc] for cc in cks]
        for cc in cks:
            r0 = base + cc * chunk
            o_ref[pl.ds(r0, chunk), :] = (
                _rms(outs[cc], hn) * _silu(hg_ref[pl.ds(r0, chunk), :])).astype(o_ref.dtype)
        return st

    st = st_scr[...]
    for gi in range(tb // (group * chunk)):
        st = body(gi, st)
    st_scr[...] = st

    @pl.when(t == pl.num_programs(2) - 1)
    def _():
        s_ref[0, 0] = st.T


def _hgrn_prompt(proj, lb, hg_norm, B, L):
    tb, chunk = HG_TB, HG_CHUNK
    nt = L // tb
    row = lambda b, h, t: b * nt + t
    colspec = lambda base: pl.BlockSpec((tb, HD), lambda b, h, t: (row(b, h, t), base + h))
    return pl.pallas_call(
        functools.partial(_hgrn_prompt_kernel, tb=tb, chunk=chunk, group=HG_GROUP),
        grid=(B, HEADS, nt),
        in_specs=[
            colspec(HQB), colspec(HFB), colspec(HIB), colspec(HGB),
            pl.BlockSpec((1, HD), lambda b, h, t: (0, h)),
            pl.BlockSpec((1, HD), lambda b, h, t: (0, 0)),
        ],
        out_specs=[
            pl.BlockSpec((tb, HD), lambda b, h, t: (row(b, h, t), h)),
            pl.BlockSpec((1, 1, HD, HD), lambda b, h, t: (b, h, 0, 0)),
        ],
        out_shape=[jax.ShapeDtypeStruct((B * L, HEADS * HD), BF16),
                   jax.ShapeDtypeStruct((B, HEADS, HD, HD), F32)],
        scratch_shapes=[pltpu.VMEM((tb, HD), F32), pltpu.VMEM((tb, HD), F32), pltpu.VMEM((tb, HD), F32),
                        pltpu.VMEM((HD, HD), F32)],
        compiler_params=_cparams(("parallel", "parallel", "arbitrary")),
        name="hgrn_prompt",
    )(proj, proj, proj, proj, lb, hg_norm)


def _stack_heads_t(x, nb):
    rows = [x[:, h * HD:(h + 1) * HD] for h in range(HEADS)]
    pad = LANES - HEADS * nb
    if pad:
        rows.append(jnp.zeros((pad, HD), F32))
    return jnp.concatenate(rows, axis=0).T


def _decode_kernel(qkv_ref, z_ref, hq_ref, hf_ref, hi_ref, hg_ref, ba_ref, cs_ref, sg_ref, sh_ref,
                   cw_ref, alog_ref, dtb_ref, lb_ref, gn_ref, hn_ref,
                   oa_ref, ob_ref, sgo_ref, sho_ref, *, nb):
    cd = cw_ref.shape[1]
    cw = cw_ref[...]
    cs = cs_ref[...]
    acc = cs[:, 0:cd] * cw[0:1, :]
    acc = acc + cs[:, cd:2 * cd] * cw[1:2, :]
    acc = acc + cs[:, 2 * cd:3 * cd] * cw[2:3, :]
    acc = acc + qkv_ref[...] * cw[3:4, :]
    conv = _silu(acc)
    nqk = HEADS * HD
    ba = ba_ref[...]
    gn = gn_ref[...]
    hn = hn_ref[...]
    z = z_ref[...]
    hgate = hg_ref[...]

    qs, ks, vs = [], [], []
    for h in range(HEADS):
        qs.append(_l2n(conv[:, h * HD:(h + 1) * HD]) * (HD ** -0.5))
        ks.append(_l2n(conv[:, nqk + h * HD:nqk + (h + 1) * HD]))
        vs.append(conv[:, 2 * nqk + h * HD:2 * nqk + (h + 1) * HD])
    qT = _stack_heads_t(jnp.concatenate(qs, axis=1), nb)
    kT = _stack_heads_t(jnp.concatenate(ks, axis=1), nb)

    lb = lb_ref[...]
    hf = hf_ref[...]
    forget = lb + (1.0 - lb) * _sigmoid(hf)
    hk = (1.0 - lb) * _sigmoid(-hf)
    hq = _silu(hq_ref[...])
    hv = hi_ref[...]
    fT = _stack_heads_t(forget, nb)
    hkT = _stack_heads_t(hk, nb)
    hqT = _stack_heads_t(hq, nb)

    for h in range(HEADS):
        beta = _sigmoid(ba[:, h:h + 1])
        g = -jnp.exp(alog_ref[0:1, h:h + 1]) * _softplus(ba[:, HEADS + h:HEADS + h + 1] + dtb_ref[0:1, h:h + 1])
        eg = jnp.exp(g)
        qk = jnp.sum(qs[h] * ks[h], axis=-1, keepdims=True)
        for b in range(nb):
            idx = h * nb + b
            S = sg_ref[b, h]
            kcol = kT[:, idx:idx + 1]
            qcol = qT[:, idx:idx + 1]
            kS = jnp.sum(S * kcol, axis=0, keepdims=True)
            qS = jnp.sum(S * qcol, axis=0, keepdims=True)
            egb = eg[b:b + 1, :]
            v_new = beta[b:b + 1, :] * (vs[h][b:b + 1, :] - egb * kS)
            o = egb * qS + qk[b:b + 1, :] * v_new
            sgo_ref[b, h] = S * egb + kcol * v_new
            oa_ref[b:b + 1, h * HD:(h + 1) * HD] = _rms(o, gn) * _silu(z[b:b + 1, h * HD:(h + 1) * HD])
            Sh = sh_ref[b, h] * fT[:, idx:idx + 1] + hkT[:, idx:idx + 1] * hv[b:b + 1, h * HD:(h + 1) * HD]
            sho_ref[b, h] = Sh
            ob = jnp.sum(Sh * hqT[:, idx:idx + 1], axis=0, keepdims=True)
            ob_ref[b:b + 1, h * HD:(h + 1) * HD] = _rms(ob, hn) * _silu(hgate[b:b + 1, h * HD:(h + 1) * HD])


def _decode(proj, ba, conv_state2d, s_gdn, s_hg, conv_w, alog_row, dtb_row, lb, gdn_norm, hg_norm):
    nb = DEC_TB
    T = proj.shape[0]
    cd = conv_w.shape[1]
    w = HEADS * HD
    cblk = lambda width, idx: pl.BlockSpec((nb, width), lambda i: (i, idx))
    sblk = pl.BlockSpec((nb, HEADS, HD, HD), lambda i: (i, 0, 0, 0))
    full = lambda shape: pl.BlockSpec(shape, lambda i: (0,) * len(shape))
    return pl.pallas_call(
        functools.partial(_decode_kernel, nb=nb),
        grid=(T // nb,),
        in_specs=[
            cblk(cd, 0), cblk(w, ZB // HEADS), cblk(w, HQB // HEADS), cblk(w, HFB // HEADS),
            cblk(w, HIB // HEADS), cblk(w, HGB // HEADS),
            cblk(LANES, 0), cblk(3 * cd, 0), sblk, sblk,
            full((CONV_K, cd)), full((1, LANES)), full((1, LANES)), full((1, w)), full((1, HD)), full((1, HD)),
        ],
        out_specs=[cblk(w, 0), cblk(w, 0), sblk, sblk],
        out_shape=[jax.ShapeDtypeStruct((T, w), F32), jax.ShapeDtypeStruct((T, w), F32),
                   jax.ShapeDtypeStruct(s_gdn.shape, F32), jax.ShapeDtypeStruct(s_hg.shape, F32)],
        compiler_params=_cparams(("parallel",)),
        name="decode",
    )(proj, proj, proj, proj, proj, proj, ba, conv_state2d, s_gdn, s_hg,
      conv_w, alog_row, dtb_row, lb, gdn_norm, hg_norm)


def _out_router_kernel(x_ref, oa_ref, ob_ref, woa_ref, wob_ref, n2_ref, wr_ref, br_ref,
                       x1_ref, h2_ref, ti_ref, tw_ref):
    y = x_ref[...] + jnp.dot(oa_ref[...].astype(BF16), woa_ref[...], preferred_element_type=F32)
    y = y + jnp.dot(ob_ref[...].astype(BF16), wob_ref[...], preferred_element_type=F32)
    x1_ref[...] = y
    h2 = _rms(y, n2_ref[...])
    h2_ref[...] = h2.astype(BF16)
    logits = _mmh(h2, wr_ref[...]) + br_ref[...]
    lane = lax.broadcasted_iota(I32, logits.shape, 1)
    logits = jnp.where(lane < N_EXPERTS, logits, NEG)
    ti = jnp.zeros(logits.shape, I32)
    tw = jnp.zeros(logits.shape, F32)
    m0 = None
    for kk in range(TOP_K):
        m = jnp.max(logits, axis=-1, keepdims=True)
        idx = jnp.min(jnp.where(logits == m, lane, LANES), axis=-1, keepdims=True)
        if m0 is None:
            m0 = m
        ti = jnp.where(lane == kk, idx, ti)
        tw = jnp.where(lane == kk, jnp.exp(m - m0), tw)
        logits = jnp.where(lane == idx, NEG * 2.0, logits)
    tw_ref[...] = tw / jnp.sum(tw, axis=-1, keepdims=True)
    ti_ref[...] = ti


def _out_router(x, oa, ob, wo_a, wo_b, norm2, w_router_p, b_router_p, tm):
    T, D = x.shape
    w = oa.shape[1]
    rowblk = lambda width: pl.BlockSpec((tm, width), lambda i: (i, 0))
    full = lambda shape: pl.BlockSpec(shape, lambda i: (0,) * len(shape))
    return pl.pallas_call(
        _out_router_kernel,
        grid=(T // tm,),
        in_specs=[rowblk(D), rowblk(w), rowblk(w), full((w, D)), full((w, D)), full((1, D)),
                  full((D, LANES)), full((1, LANES))],
        out_specs=[rowblk(D), rowblk(D), rowblk(LANES), rowblk(LANES)],
        out_shape=[jax.ShapeDtypeStruct((T, D), F32), jax.ShapeDtypeStruct((T, D), BF16),
                   jax.ShapeDtypeStruct((T, LANES), I32), jax.ShapeDtypeStruct((T, LANES), F32)],
        compiler_params=_cparams(("parallel",)),
        name="out_router",
    )(x, oa, ob, wo_a, wo_b, norm2, w_router_p, b_router_p)


def _moe_gate_up_kernel(te_ref, tv_ref, tf_ref, x_ref, w_ref, bg_ref, bu_ref, act_ref, wc_scr):
    m = pl.program_id(1)
    D, tnw = w_ref.shape[1], w_ref.shape[2]
    grp = 2 * LANES
    ngrp = tnw // grp

    @pl.when(tv_ref[m] != 0)
    def _():
        @pl.when(tf_ref[m] != 0)
        def _():
            src = lax.broadcasted_iota(I32, (grp, grp), 0)
            dst = lax.broadcasted_iota(I32, (grp, grp), 1)
            want = jnp.where(dst < LANES, 2 * dst, 2 * (dst - LANES) + 1)
            perm = jnp.where(src == want, 1.0, 0.0).astype(BF16)

            def body(r, carry):
                r0 = pl.multiple_of(r * grp, grp)
                for g in range(ngrp):
                    wt = w_ref[0, pl.ds(r0, grp), g * grp:(g + 1) * grp].astype(BF16)
                    wc_scr[pl.ds(r0, grp), g * grp:(g + 1) * grp] = jnp.dot(
                        wt, perm, preferred_element_type=F32).astype(BF16)
                return carry

            lax.fori_loop(0, D // grp, body, 0)

        x = x_ref[...]
        bg = bg_ref[0]
        bu = bu_ref[0]
        for g in range(ngrp):
            gu = jnp.dot(x, wc_scr[:, g * grp:(g + 1) * grp], preferred_element_type=F32)
            gate = jnp.minimum(gu[:, 0:LANES] + bg[:, g * LANES:(g + 1) * LANES], SWIGLU_LIMIT)
            up = jnp.clip(gu[:, LANES:grp] + bu[:, g * LANES:(g + 1) * LANES], -SWIGLU_LIMIT, SWIGLU_LIMIT)
            act_ref[:, g * LANES:(g + 1) * LANES] = (
                (up + 1.0) * (gate * _sigmoid(gate * SWIGLU_ALPHA))).astype(act_ref.dtype)


def _moe_gate_up(te, tv, tf, xs, w_gate_up, bg, bu):
    tm, tnw = MOE_TM, MOE_TNW
    P, D = xs.shape
    F2 = w_gate_up.shape[2]
    imap_w = lambda n, m, te, tv, tf: (te[m], 0, n)
    return pl.pallas_call(
        _moe_gate_up_kernel,
        grid_spec=pltpu.PrefetchScalarGridSpec(
            num_scalar_prefetch=3,
            grid=(F2 // tnw, P // tm),
            in_specs=[
                pl.BlockSpec((tm, D), lambda n, m, te, tv, tf: (m, 0)),
                pl.BlockSpec((1, D, tnw), imap_w),
                pl.BlockSpec((1, 1, tnw // 2), imap_w),
                pl.BlockSpec((1, 1, tnw // 2), imap_w),
            ],
            out_specs=pl.BlockSpec((tm, tnw // 2), lambda n, m, te, tv, tf: (m, n)),
            scratch_shapes=[pltpu.VMEM((D, tnw), BF16)],
        ),
        out_shape=jax.ShapeDtypeStruct((P, F2 // 2), BF16),
        compiler_params=_cparams(("arbitrary", "arbitrary")),
        name="moe_gate_up",
    )(te, tv, tf, xs, w_gate_up, bg, bu)


def _moe_down_kernel(te_ref, tv_ref, tf_ref, a_ref, wd_ref, bd_ref, y_ref, wc_scr):
    m = pl.program_id(1)

    @pl.when(tv_ref[m] != 0)
    def _():
        @pl.when(tf_ref[m] != 0)
        def _():
            wc_scr[...] = wd_ref[0].astype(BF16)

        y_ref[...] = jnp.dot(a_ref[...], wc_scr[...], preferred_element_type=F32) + bd_ref[0]


def _moe_down(te, tv, tf, act, w_down, bd):
    tm, tn = MOE_TM, MOE_TN
    P, F = act.shape
    D = w_down.shape[2]
    imap_w = lambda n, m, te, tv, tf: (te[m], 0, n)
    return pl.pallas_call(
        _moe_down_kernel,
        grid_spec=pltpu.PrefetchScalarGridSpec(
            num_scalar_prefetch=3,
            grid=(D // tn, P // tm),
            in_specs=[
                pl.BlockSpec((tm, F), lambda n, m, te, tv, tf: (m, 0)),
                pl.BlockSpec((1, F, tn), imap_w),
                pl.BlockSpec((1, 1, tn), imap_w),
            ],
            out_specs=pl.BlockSpec((tm, tn), lambda n, m, te, tv, tf: (m, n)),
            scratch_shapes=[pltpu.VMEM((F, tn), BF16)],
        ),
        out_shape=jax.ShapeDtypeStruct((P, D), F32),
        compiler_params=_cparams(("arbitrary", "arbitrary")),
        name="moe_down",
    )(te, tv, tf, act, w_down, bd)


def _combine_kernel(x1_ref, yg_ref, tw_ref, fn_ref, y_ref):
    tw = tw_ref[...]
    moe = tw[:, 0:1] * yg_ref[0]
    for kk in range(1, TOP_K):
        moe = moe + tw[:, kk:kk + 1] * yg_ref[kk]
    y_ref[...] = _rms(x1_ref[...] + moe, fn_ref[...])


def _combine(x1, yg, tw, final_norm, tm, row_off):
    T, D = x1.shape
    off = row_off // tm
    return pl.pallas_call(
        _combine_kernel,
        grid=(T // tm,),
        in_specs=[
            pl.BlockSpec((tm, D), lambda i: (i, 0)),
            pl.BlockSpec((TOP_K, tm, D), lambda i: (0, i + off, 0)),
            pl.BlockSpec((tm, LANES), lambda i: (i, 0)),
            pl.BlockSpec((1, D), lambda i: (0, 0)),
        ],
        out_specs=pl.BlockSpec((tm, D), lambda i: (i, 0)),
        out_shape=jax.ShapeDtypeStruct((T, D), F32),
        compiler_params=_cparams(("parallel",)),
        name="combine",
    )(x1, yg, tw, final_norm)


def _routing(top_i):
    T = top_i.shape[0]
    P = T * TOP_K
    tm = MOE_TM
    n_tiles = -(-P // tm) + N_EXPERTS
    e_flat = top_i.reshape(P)
    onehot = (e_flat[:, None] == jnp.arange(N_EXPERTS, dtype=I32)[None, :]).astype(I32)
    csum = jnp.cumsum(onehot, axis=0)
    rank = jnp.sum((csum - 1) * onehot, axis=1)
    counts = csum[-1]
    padded = ((counts + tm - 1) // tm) * tm
    ends = jnp.cumsum(padded)
    starts = ends - padded
    pos = starts[e_flat] + rank
    slot_token = jnp.zeros((n_tiles * tm,), I32).at[pos].set(jnp.arange(P, dtype=I32) // TOP_K)
    tile_start = jnp.arange(n_tiles, dtype=I32) * tm
    tile_valid = (tile_start < ends[-1]).astype(I32)
    tile_expert = jnp.minimum(jnp.sum((tile_start[:, None] >= ends[None, :]).astype(I32), axis=1), N_EXPERTS - 1)
    tile_expert = tile_expert.astype(I32)
    tile_first = jnp.concatenate([jnp.ones((1,), I32), (tile_expert[1:] != tile_expert[:-1]).astype(I32)])
    return pos, slot_token, tile_expert, tile_valid, tile_first


def kernel(x_prompt, x_sample, state_gdn, state_conv, state_hgrn, lb_table, norm1, w_in, conv_w, A_log, dt_bias,
           gdn_norm, hg_norm, w_out, norm2, w_router, b_router, w_gate_up, b_gate_up, w_down, b_down, final_norm):
    depth = w_in.shape[0]
    assert depth == 1
    B, L, D = x_prompt.shape
    SB = x_sample.shape[0]
    assert x_sample.shape[1] == 1
    nqk = HEADS * HD
    cd = 3 * nqk

    lbs = jnp.cumsum(jax.nn.softmax(lb_table.astype(F32), axis=0), axis=0)
    lb = lbs[0:1]

    assert w_in.shape[2] == MAIN_COLS + 2 * HEADS and cd + nqk == MAIN_COLS // 2
    w_main, w_ba = _repack_w_in(w_in[0], 2 * HEADS)
    wo_a = w_out[0, :nqk].astype(BF16)
    wo_b = w_out[0, nqk:].astype(BF16)
    wr_p = jnp.pad(w_router[0], ((0, 0), (0, LANES - N_EXPERTS)))
    br_p = jnp.pad(b_router[0], (0, LANES - N_EXPERTS))[None, :]
    bg = b_gate_up[0, :, None, 0::2]
    bu = b_gate_up[0, :, None, 1::2]
    bd = b_down[0][:, None, :]
    alog_b = jnp.broadcast_to(A_log[0][:, None, None], (HEADS, 1, HD))
    dtb_b = jnp.broadcast_to(dt_bias[0][:, None, None], (HEADS, 1, HD))
    alog_row = jnp.pad(A_log[0], (0, LANES - HEADS))[None, :]
    dtb_row = jnp.pad(dt_bias[0], (0, LANES - HEADS))[None, :]
    n1 = norm1[0][None, :]
    n2 = norm2[0][None, :]
    gn = gdn_norm[0][None, :]
    hn = hg_norm[0][None, :]
    fnw = final_norm[None, :]
    cw = conv_w[0]

    xp = x_prompt.reshape(B * L, D)
    xs = x_sample.reshape(SB, D)

    proj_p, ba_p = _in_proj(xp, n1, w_main, w_ba, tm=1024, tn=1024)
    proj_s, ba_s = _in_proj(xs, n1, w_main, w_ba, tm=SB, tn=1024)

    oa_p, sg_p = _gdn_prompt(proj_p, ba_p, cw, alog_b, dtb_b, gn, B, L)
    ob_p, sh_p = _hgrn_prompt(proj_p, lb, hn, B, L)
    oa_s, ob_s, sg_s, sh_s = _decode(proj_s, ba_s, state_conv[0].reshape(SB, (CONV_K - 1) * cd), state_gdn[0],
                                     state_hgrn[0], cw, alog_row, dtb_row, lb, gn, hn)

    x1_p, h2_p, ti_p, tw_p = _out_router(xp, oa_p, ob_p, wo_a, wo_b, n2, wr_p, br_p, tm=256)
    x1_s, h2_s, ti_s, tw_s = _out_router(xs, oa_s, ob_s, wo_a, wo_b, n2, wr_p, br_p, tm=SB)

    h2 = jnp.concatenate([h2_p, h2_s], axis=0)
    top_i = jnp.concatenate([ti_p, ti_s], axis=0)[:, :TOP_K]
    pos, slot_token, tile_expert, tile_valid, tile_first = _routing(top_i)
    xs_sorted = h2.at[slot_token].get(mode="promise_in_bounds")
    act = _moe_gate_up(tile_expert, tile_valid, tile_first, xs_sorted, w_gate_up[0], bg, bu)
    yslots = _moe_down(tile_expert, tile_valid, tile_first, act, w_down[0], bd)
    pos_kmajor = pos.reshape(B * L + SB, TOP_K).T.reshape(-1)
    yg = yslots.at[pos_kmajor].get(mode="promise_in_bounds").reshape(TOP_K, B * L + SB, D)

    y_p = _combine(x1_p, yg, tw_p, fnw, tm=256, row_off=0)
    y_s = _combine(x1_s, yg, tw_s, fnw, tm=SB, row_off=B * L)

    conv_p = proj_p.reshape(B, L, MAIN_COLS)[:, L - (CONV_K - 1):, :cd]
    conv_s = jnp.concatenate([state_conv[0][:, 1:, :], proj_s[:, None, :cd]], axis=1)
    return (y_p.reshape(B, L, D), y_s.reshape(SB, 1, D),
            sg_p[None], conv_p[None].astype(state_conv.dtype), sh_p[None],
            sg_s[None], conv_s[None].astype(state_conv.dtype), sh_s[None])
```

```python
import functools

import jax
import jax.numpy as jnp
from jax import lax
from jax.experimental import pallas as pl
from jax.experimental.pallas import tpu as pltpu

F32 = jnp.float32
BF16 = jnp.bfloat16
I32 = jnp.int32

EPS = 1e-6
HEADS = 8
HD = 128
CONV_K = 4
N_EXPERTS = 32
TOP_K = 4
SWIGLU_ALPHA = 1.702
SWIGLU_LIMIT = 7.0
LANES = 128
NEG = -1e30

VMEM_LIMIT = 56 * 1024 * 1024

QB, KB, VB, ZB, HQB, HFB, HIB, HGB = (i * HEADS for i in range(8))
MAIN_COLS = 8 * HEADS * HD

GDN_CHUNK = 128
GDN_TB = 512
HG_CHUNK = 16
HG_TB = 256
HG_GROUP = 4
DEC_TB = 8
MOE_TM = 256
MOE_TN = 512
MOE_TNW = 2048


def _cparams(sem):
    return pltpu.CompilerParams(dimension_semantics=sem, vmem_limit_bytes=VMEM_LIMIT)


def _mm(a, b):
    return jnp.dot(a.astype(BF16), b.astype(BF16), preferred_element_type=F32)


def _mm_nt(a, b):
    return lax.dot_general(a.astype(BF16), b.astype(BF16), (((1,), (1,)), ((), ())),
                           preferred_element_type=F32)


def _mm_tn(a, b):
    return lax.dot_general(a.astype(BF16), b.astype(BF16), (((0,), (0,)), ((), ())),
                           preferred_element_type=F32)


def _mmh(a, b):
    return jnp.dot(a, b, precision=lax.Precision.HIGHEST, preferred_element_type=F32)


def _split(a):
    hi = a.astype(BF16)
    return hi, (a - hi.astype(F32)).astype(BF16)


def _mm3(a, b):
    d = lambda x, y: jnp.dot(x, y, preferred_element_type=F32)
    return d(a[0], b[0]) + (d(a[0], b[1]) + d(a[1], b[0]))


def _mm_exact_lhs(l_bf16, x):
    d = lambda y: jnp.dot(l_bf16, y, preferred_element_type=F32)
    x0 = x.astype(BF16)
    r1 = x - x0.astype(F32)
    x1 = r1.astype(BF16)
    x2 = (r1 - x1.astype(F32)).astype(BF16)
    return d(x0) + (d(x1) + d(x2))


def _sigmoid(x):
    return 1.0 / (1.0 + jnp.exp(-x))


def _silu(x):
    return x * _sigmoid(x)


def _softplus(x):
    return jnp.maximum(x, 0.0) + jnp.log1p(jnp.exp(-jnp.abs(x)))


def _rms(x, w):
    return x * lax.rsqrt(jnp.mean(x * x, axis=-1, keepdims=True) + EPS) * w


def _l2n(x):
    return x * lax.rsqrt(jnp.sum(x * x, axis=-1, keepdims=True) + EPS)


def _repack_kernel(wa_ref, wb_ref, wt_ref, wm_ref, wba_ref, *, nba):
    half = wa_ref.shape[1]
    wm_ref[:, 0:half] = wa_ref[...].astype(BF16)
    win = jnp.concatenate([wb_ref[...], wt_ref[...]], axis=1)
    wm_ref[:, half:2 * half] = win[:, nba:nba + half].astype(BF16)
    first = wb_ref[:, 0:LANES]
    lane = lax.broadcasted_iota(I32, first.shape, 1)
    wba_ref[...] = jnp.where(lane < nba, first, 0.0).astype(BF16)


def _repack_w_in(w, nba):
    D, n = w.shape
    half = (n - nba) // 2
    assert half % LANES == 0 and nba <= LANES
    tr = 256
    return pl.pallas_call(
        functools.partial(_repack_kernel, nba=nba),
        grid=(D // tr,),
        in_specs=[
            pl.BlockSpec((tr, half), lambda i: (i, 0)),
            pl.BlockSpec((tr, half), lambda i: (i, 1)),
            pl.BlockSpec((tr, LANES), lambda i: (i, 2 * half // LANES)),
        ],
        out_specs=[pl.BlockSpec((tr, 2 * half), lambda i: (i, 0)), pl.BlockSpec((tr, LANES), lambda i: (i, 0))],
        out_shape=[jax.ShapeDtypeStruct((D, 2 * half), BF16), jax.ShapeDtypeStruct((D, LANES), BF16)],
        compiler_params=_cparams(("parallel",)),
        name="repack_w_in",
    )(w, w, w)


def _in_proj_kernel(x_ref, nw_ref, w_ref, wba_ref, o_ref, ba_ref, h_scr):
    @pl.when(pl.program_id(1) == 0)
    def _():
        h = _rms(x_ref[...], nw_ref[...]).astype(BF16)
        h_scr[...] = h
        ba_ref[...] = jnp.dot(h, wba_ref[...], preferred_element_type=F32)

    o_ref[...] = jnp.dot(h_scr[...], w_ref[...], preferred_element_type=F32)


def _in_proj(x, norm_w, w_main, w_ba, tm, tn):
    T, D = x.shape
    N = w_main.shape[1]
    return pl.pallas_call(
        _in_proj_kernel,
        grid=(T // tm, N // tn),
        in_specs=[
            pl.BlockSpec((tm, D), lambda i, j: (i, 0)),
            pl.BlockSpec((1, D), lambda i, j: (0, 0)),
            pl.BlockSpec((D, tn), lambda i, j: (0, j)),
            pl.BlockSpec((D, LANES), lambda i, j: (0, 0)),
        ],
        out_specs=[
            pl.BlockSpec((tm, tn), lambda i, j: (i, j)),
            pl.BlockSpec((tm, LANES), lambda i, j: (i, 0)),
        ],
        out_shape=[jax.ShapeDtypeStruct((T, N), F32), jax.ShapeDtypeStruct((T, LANES), F32)],
        scratch_shapes=[pltpu.VMEM((tm, D), BF16)],
        compiler_params=_cparams(("parallel", "arbitrary")),
        name="in_proj",
    )(x, norm_w, w_main, w_ba)


def _unit_lower_inverse(As, ri, ci):
    n = As[0].shape[0]
    eye = (ri == ci).astype(F32)
    same16 = (ri // 16) == (ci // 16)
    s1 = [_split(jnp.where(same16, -A, 0.0)) for A in As]
    s2 = [_split(_mm3(s, s)) for s in s1]
    s4 = [_split(_mm3(s, s)) for s in s2]
    s8 = [_split(_mm3(s, s)) for s in s4]
    Ts = [eye + jnp.where(same16, -A, 0.0) for A in As]
    for sp in (s2, s4, s8):
        Ts = [T + _mm3(_split(T), s) for T, s in zip(Ts, sp)]
    size = 32
    while size <= n:
        off = ((ri // size) == (ci // size)) & ((ri // (size // 2)) != (ci // (size // 2)))
        sT = [_split(T) for T in Ts]
        TL = [_mm3(st, _split(jnp.where(off, A, 0.0))) for st, A in zip(sT, As)]
        Ts = [T - _mm3(_split(tl), st) for T, tl, st in zip(Ts, TL, sT)]
        size *= 2
    return Ts


def _gdn_prompt_kernel(q_ref, k_ref, v_ref, z_ref, ba_ref, cwq_ref, cwk_ref, cwv_ref, alog_ref, dtb_ref,
                       gn_ref, o_ref, s_ref, ubuf, s_scr, *, tb, chunk):
    h = pl.program_id(1)
    t = pl.program_id(2)

    @pl.when(t == 0)
    def _():
        ubuf[:, 0:8, :] = jnp.zeros((3, 8, HD), F32)
        s_scr[...] = jnp.zeros((HD, HD), F32)

    ubuf[0, 8:8 + tb, :] = q_ref[...]
    ubuf[1, 8:8 + tb, :] = k_ref[...]
    ubuf[2, 8:8 + tb, :] = v_ref[...]

    ba = ba_ref[...]
    lane = lax.broadcasted_iota(I32, ba.shape, 1)
    b_col = jnp.sum(jnp.where(lane == h, ba, 0.0), axis=-1, keepdims=True)
    a_col = jnp.sum(jnp.where(lane == h + HEADS, ba, 0.0), axis=-1, keepdims=True)
    beta_col = _sigmoid(b_col)
    g_col = -jnp.exp(alog_ref[0][:, 0:1]) * _softplus(a_col + dtb_ref[0][:, 0:1])

    ri = lax.broadcasted_iota(I32, (chunk, chunk), 0)
    ci = lax.broadcasted_iota(I32, (chunk, chunk), 1)
    causal = ci <= ri
    strict = ci < ri
    ltri = jnp.where(causal, 1.0, 0.0).astype(BF16)

    def conv(idx, w_ref, r0):
        w = w_ref[...]
        acc = ubuf[idx, r0 + 5:r0 + 5 + chunk, :] * w[0:1, :]
        for j in range(1, CONV_K):
            acc = acc + ubuf[idx, r0 + 5 + j:r0 + 5 + j + chunk, :] * w[j:j + 1, :]
        return _silu(acc)

    nc = tb // chunk
    qs, ks, vs, betas, gcols, decays, As = [], [], [], [], [], [], []
    for c in range(nc):
        r0 = c * chunk
        q = _l2n(conv(0, cwq_ref, r0)) * (HD ** -0.5)
        k = _l2n(conv(1, cwk_ref, r0))
        v = conv(2, cwv_ref, r0)
        beta = beta_col[r0:r0 + chunk, :]
        gcol = _mm_exact_lhs(ltri, jnp.broadcast_to(g_col[r0:r0 + chunk, :], (chunk, HD)))
        decay = jnp.exp(jnp.where(causal, gcol - gcol.T, NEG))
        kb = k * beta
        qs.append(q); ks.append(k); vs.append(v); betas.append(beta); gcols.append(gcol); decays.append(decay)
        As.append(jnp.where(strict, _mm_nt(kb, k) * decay, 0.0))
    Ts = _unit_lower_inverse(As, ri, ci)
    us, ws, scs = [], [], []
    for c in range(nc):
        sT = _split(Ts[c])
        kb = ks[c] * betas[c]
        us.append(_mm3(sT, _split(vs[c] * betas[c])))
        ws.append(_mm3(sT, _split(kb * jnp.exp(gcols[c]))))
        scs.append(_mm_nt(qs[c], ks[c]) * decays[c])

    S = s_scr[...]
    for c in range(nc):
        r0 = c * chunk
        gcol = gcols[c]
        v_new = us[c] - _mm(ws[c], S)
        o = _mm(qs[c] * jnp.exp(gcol), S) + _mm(scs[c], v_new)
        glast = gcol[chunk - 1:chunk, :]
        S = S * jnp.exp(glast) + _mm_tn(ks[c] * jnp.exp(glast - gcol), v_new)
        o_ref[r0:r0 + chunk, :] = (_rms(o, gn_ref[...]) * _silu(z_ref[r0:r0 + chunk, :])).astype(o_ref.dtype)
    s_scr[...] = S

    ubuf[:, 0:8, :] = ubuf[:, tb:tb + 8, :]

    @pl.when(t == pl.num_programs(2) - 1)
    def _():
        s_ref[0, 0] = S


def _gdn_prompt(proj, ba, conv_w, alog_b, dtb_b, gdn_norm, B, L):
    tb, chunk = GDN_TB, GDN_CHUNK
    nt = L // tb
    row = lambda b, h, t: b * nt + t
    colspec = lambda base: pl.BlockSpec((tb, HD), lambda b, h, t: (row(b, h, t), base + h))
    cwspec = lambda base: pl.BlockSpec((CONV_K, HD), lambda b, h, t: (0, base + h))
    hvec = pl.BlockSpec((1, 1, HD), lambda b, h, t: (h, 0, 0))
    return pl.pallas_call(
        functools.partial(_gdn_prompt_kernel, tb=tb, chunk=chunk),
        grid=(B, HEADS, nt),
        in_specs=[
            colspec(QB), colspec(KB), colspec(VB), colspec(ZB),
            pl.BlockSpec((tb, LANES), lambda b, h, t: (row(b, h, t), 0)),
            cwspec(QB), cwspec(KB), cwspec(VB),
            hvec, hvec,
            pl.BlockSpec((1, HD), lambda b, h, t: (0, 0)),
        ],
        out_specs=[
            pl.BlockSpec((tb, HD), lambda b, h, t: (row(b, h, t), h)),
            pl.BlockSpec((1, 1, HD, HD), lambda b, h, t: (b, h, 0, 0)),
        ],
        out_shape=[jax.ShapeDtypeStruct((B * L, HEADS * HD), BF16),
                   jax.ShapeDtypeStruct((B, HEADS, HD, HD), F32)],
        scratch_shapes=[pltpu.VMEM((3, tb + 8, HD), F32), pltpu.VMEM((HD, HD), F32)],
        compiler_params=_cparams(("parallel", "parallel", "arbitrary")),
        name="gdn_prompt",
    )(proj, proj, proj, proj, ba, conv_w, conv_w, conv_w, alog_b, dtb_b, gdn_norm)


def _hgrn_prompt_kernel(hq_ref, hf_ref, hi_ref, hg_ref, lb_ref, hn_ref, o_ref, s_ref,
                        g_scr, q_scr, k_scr, st_scr, *, tb, chunk, group):
    t = pl.program_id(2)

    @pl.when(t == 0)
    def _():
        st_scr[...] = jnp.zeros((HD, HD), F32)

    lb = lb_ref[...]
    hf = hf_ref[...]
    forget = lb + (1.0 - lb) * _sigmoid(hf)
    k_scr[...] = (1.0 - lb) * _sigmoid(-hf)
    q_scr[...] = _silu(hq_ref[...])
    ri = lax.broadcasted_iota(I32, (LANES, LANES), 0)
    ci = lax.broadcasted_iota(I32, (LANES, LANES), 1)
    lblk = jnp.where(((ri // chunk) == (ci // chunk)) & (ci <= ri), 1.0, 0.0).astype(BF16)
    lf = jnp.log(forget)
    for r in range(tb // LANES):
        g_scr[r * LANES:(r + 1) * LANES, :] = _mm_exact_lhs(lblk, lf[r * LANES:(r + 1) * LANES, :])

    half = chunk // 2
    row8 = lax.broadcasted_iota(I32, (half, HD), 0)
    lane8 = lax.broadcasted_iota(I32, (half, HD), 1)
    zpad = jnp.zeros((half, HD), F32)
    hn = hn_ref[...]

    def body(gi, st):
        base = gi * (group * chunk)
        cks = range(group)
        vs = [hi_ref[pl.ds(base + cc * chunk, chunk), :] for cc in cks]
        vts = [v.T for v in vs]
        blocks = []
        for n in range(2 * group):
            r0 = base + n * half
            blocks.append((g_scr[pl.ds(r0, half), :], q_scr[pl.ds(r0, half), :], k_scr[pl.ds(r0, half), :]))
        accs = [zpad] * (2 * group)
        for j in range(half):
            for n, (Gb, qb, kb) in enumerate(blocks):
                e = jnp.exp(jnp.where(row8 >= j, Gb - Gb[j:j + 1, :], NEG))
                col = jnp.sum(e * qb * kb[j:j + 1, :], axis=-1, keepdims=True)
                accs[n] = jnp.where(lane8 == (n % 2) * half + j, col, accs[n])
        Gs = [jnp.concatenate([blocks[2 * cc][0], blocks[2 * cc + 1][0]], axis=0) for cc in cks]
        qs = [jnp.concatenate([blocks[2 * cc][1], blocks[2 * cc + 1][1]], axis=0) for cc in cks]
        ks = [jnp.concatenate([blocks[2 * cc][2], blocks[2 * cc + 1][2]], axis=0) for cc in cks]
        offd = []
        for cc in cks:
            Ga, _, ka = blocks[2 * cc]
            Gb, qb, _ = blocks[2 * cc + 1]
            gmid = Ga[half - 1:half, :]
            kh = jnp.concatenate([ka * jnp.exp(gmid - Ga), zpad], axis=0)
            offd.append(_mm_nt(qb * jnp.exp(Gb - gmid), kh))
        glasts = [Gs[cc][chunk - 1:chunk, :] for cc in cks]
        incs = [_mm(vts[cc], ks[cc] * jnp.exp(glasts[cc] - Gs[cc])) for cc in cks]
        intra = []
        for cc in cks:
            s = jnp.concatenate([accs[2 * cc][:, 0:chunk], accs[2 * cc + 1][:, 0:chunk] + offd[cc]], axis=0)
            intra.append(_mm(s, vs[cc]))
        sts = []
        for cc in cks:
            sts.append(st)
            st = st * jnp.exp(glasts[cc]) + incs[cc]
        outs = [_mm_nt(qs[cc] * jnp.exp(Gs[cc]), sts[cc]) + intra[cc] for cc in cks]
        for cc in cks:
            r0 = base + cc * chunk
            o_ref[pl.ds(r0, chunk), :] = (
                _rms(outs[cc], hn) * _silu(hg_ref[pl.ds(r0, chunk), :])).astype(o_ref.dtype)
        return st

    st = st_scr[...]
    for gi in range(tb // (group * chunk)):
        st = body(gi, st)
    st_scr[...] = st

    @pl.when(t == pl.num_programs(2) - 1)
    def _():
        s_ref[0, 0] = st.T


def _hgrn_prompt(proj, lb, hg_norm, B, L):
    tb, chunk = HG_TB, HG_CHUNK
    nt = L // tb
    row = lambda b, h, t: b * nt + t
    colspec = lambda base: pl.BlockSpec((tb, HD), lambda b, h, t: (row(b, h, t), base + h))
    return pl.pallas_call(
        functools.partial(_hgrn_prompt_kernel, tb=tb, chunk=chunk, group=HG_GROUP),
        grid=(B, HEADS, nt),
        in_specs=[
            colspec(HQB), colspec(HFB), colspec(HIB), colspec(HGB),
            pl.BlockSpec((1, HD), lambda b, h, t: (0, h)),
            pl.BlockSpec((1, HD), lambda b, h, t: (0, 0)),
        ],
        out_specs=[
            pl.BlockSpec((tb, HD), lambda b, h, t: (row(b, h, t), h)),
            pl.BlockSpec((1, 1, HD, HD), lambda b, h, t: (b, h, 0, 0)),
        ],
        out_shape=[jax.ShapeDtypeStruct((B * L, HEADS * HD), BF16),
                   jax.ShapeDtypeStruct((B, HEADS, HD, HD), F32)],
        scratch_shapes=[pltpu.VMEM((tb, HD), F32), pltpu.VMEM((tb, HD), F32), pltpu.VMEM((tb, HD), F32),
                        pltpu.VMEM((HD, HD), F32)],
        compiler_params=_cparams(("parallel", "parallel", "arbitrary")),
        name="hgrn_prompt",
    )(proj, proj, proj, proj, lb, hg_norm)


def _stack_heads_t(x, nb):
    rows = [x[:, h * HD:(h + 1) * HD] for h in range(HEADS)]
    pad = LANES - HEADS * nb
    if pad:
        rows.append(jnp.zeros((pad, HD), F32))
    return jnp.concatenate(rows, axis=0).T


def _decode_kernel(qkv_ref, z_ref, hq_ref, hf_ref, hi_ref, hg_ref, ba_ref, cs_ref, sg_ref, sh_ref,
                   cw_ref, alog_ref, dtb_ref, lb_ref, gn_ref, hn_ref,
                   oa_ref, ob_ref, sgo_ref, sho_ref, *, nb):
    cd = cw_ref.shape[1]
    cw = cw_ref[...]
    cs = cs_ref[...]
    acc = cs[:, 0:cd] * cw[0:1, :]
    acc = acc + cs[:, cd:2 * cd] * cw[1:2, :]
    acc = acc + cs[:, 2 * cd:3 * cd] * cw[2:3, :]
    acc = acc + qkv_ref[...] * cw[3:4, :]
    conv = _silu(acc)
    nqk = HEADS * HD
    ba = ba_ref[...]
    gn = gn_ref[...]
    hn = hn_ref[...]
    z = z_ref[...]
    hgate = hg_ref[...]

    qs, ks, vs = [], [], []
    for h in range(HEADS):
        qs.append(_l2n(conv[:, h * HD:(h + 1) * HD]) * (HD ** -0.5))
        ks.append(_l2n(conv[:, nqk + h * HD:nqk + (h + 1) * HD]))
        vs.append(conv[:, 2 * nqk + h * HD:2 * nqk + (h + 1) * HD])
    qT = _stack_heads_t(jnp.concatenate(qs, axis=1), nb)
    kT = _stack_heads_t(jnp.concatenate(ks, axis=1), nb)

    lb = lb_ref[...]
    hf = hf_ref[...]
    forget = lb + (1.0 - lb) * _sigmoid(hf)
    hk = (1.0 - lb) * _sigmoid(-hf)
    hq = _silu(hq_ref[...])
    hv = hi_ref[...]
    fT = _stack_heads_t(forget, nb)
    hkT = _stack_heads_t(hk, nb)
    hqT = _stack_heads_t(hq, nb)

    for h in range(HEADS):
        beta = _sigmoid(ba[:, h:h + 1])
        g = -jnp.exp(alog_ref[0:1, h:h + 1]) * _softplus(ba[:, HEADS + h:HEADS + h + 1] + dtb_ref[0:1, h:h + 1])
        eg = jnp.exp(g)
        qk = jnp.sum(qs[h] * ks[h], axis=-1, keepdims=True)
        for b in range(nb):
            idx = h * nb + b
            S = sg_ref[b, h]
            kcol = kT[:, idx:idx + 1]
            qcol = qT[:, idx:idx + 1]
            kS = jnp.sum(S * kcol, axis=0, keepdims=True)
            qS = jnp.sum(S * qcol, axis=0, keepdims=True)
            egb = eg[b:b + 1, :]
            v_new = beta[b:b + 1, :] * (vs[h][b:b + 1, :] - egb * kS)
            o = egb * qS + qk[b:b + 1, :] * v_new
            sgo_ref[b, h] = S * egb + kcol * v_new
            oa_ref[b:b + 1, h * HD:(h + 1) * HD] = _rms(o, gn) * _silu(z[b:b + 1, h * HD:(h + 1) * HD])
            Sh = sh_ref[b, h] * fT[:, idx:idx + 1] + hkT[:, idx:idx + 1] * hv[b:b + 1, h * HD:(h + 1) * HD]
            sho_ref[b, h] = Sh
            ob = jnp.sum(Sh * hqT[:, idx:idx + 1], axis=0, keepdims=True)
            ob_ref[b:b + 1, h * HD:(h + 1) * HD] = _rms(ob, hn) * _silu(hgate[b:b + 1, h * HD:(h + 1) * HD])


def _decode(proj, ba, conv_state2d, s_gdn, s_hg, conv_w, alog_row, dtb_row, lb, gdn_norm, hg_norm):
    nb = DEC_TB
    T = proj.shape[0]
    cd = conv_w.shape[1]
    w = HEADS * HD
    cblk = lambda width, idx: pl.BlockSpec((nb, width), lambda i: (i, idx))
    sblk = pl.BlockSpec((nb, HEADS, HD, HD), lambda i: (i, 0, 0, 0))
    full = lambda shape: pl.BlockSpec(shape, lambda i: (0,) * len(shape))
    return pl.pallas_call(
        functools.partial(_decode_kernel, nb=nb),
        grid=(T // nb,),
        in_specs=[
            cblk(cd, 0), cblk(w, ZB // HEADS), cblk(w, HQB // HEADS), cblk(w, HFB // HEADS),
            cblk(w, HIB // HEADS), cblk(w, HGB // HEADS),
            cblk(LANES, 0), cblk(3 * cd, 0), sblk, sblk,
            full((CONV_K, cd)), full((1, LANES)), full((1, LANES)), full((1, w)), full((1, HD)), full((1, HD)),
        ],
        out_specs=[cblk(w, 0), cblk(w, 0), sblk, sblk],
        out_shape=[jax.ShapeDtypeStruct((T, w), F32), jax.ShapeDtypeStruct((T, w), F32),
                   jax.ShapeDtypeStruct(s_gdn.shape, F32), jax.ShapeDtypeStruct(s_hg.shape, F32)],
        compiler_params=_cparams(("parallel",)),
        name="decode",
    )(proj, proj, proj, proj, proj, proj, ba, conv_state2d, s_gdn, s_hg,
      conv_w, alog_row, dtb_row, lb, gdn_norm, hg_norm)


def _out_router_kernel(x_ref, oa_ref, ob_ref, woa_ref, wob_ref, n2_ref, wrh_ref, wrl_ref, br_ref,
                       x1_ref, h2_ref, ti_ref, tw_ref):
    y = x_ref[...] + jnp.dot(oa_ref[...].astype(BF16), woa_ref[...], preferred_element_type=F32)
    y = y + jnp.dot(ob_ref[...].astype(BF16), wob_ref[...], preferred_element_type=F32)
    x1_ref[...] = y
    h2 = _rms(y, n2_ref[...])
    h2_ref[...] = h2
    logits = _mm3(_split(h2), (wrh_ref[...], wrl_ref[...])) + br_ref[...]
    lane = lax.broadcasted_iota(I32, logits.shape, 1)
    logits = jnp.where(lane < N_EXPERTS, logits, NEG)
    ti = jnp.zeros(logits.shape, I32)
    tw = jnp.zeros(logits.shape, F32)
    m0 = None
    for kk in range(TOP_K):
        m = jnp.max(logits, axis=-1, keepdims=True)
        idx = jnp.min(jnp.where(logits == m, lane, LANES), axis=-1, keepdims=True)
        if m0 is None:
            m0 = m
        ti = jnp.where(lane == kk, idx, ti)
        tw = jnp.where(lane == kk, jnp.exp(m - m0), tw)
        logits = jnp.where(lane == idx, NEG * 2.0, logits)
    tw_ref[...] = tw / jnp.sum(tw, axis=-1, keepdims=True)
    ti_ref[...] = ti


def _out_router(x, oa, ob, wo_a, wo_b, norm2, wr_hi, wr_lo, b_router_p, tm):
    T, D = x.shape
    w = oa.shape[1]
    rowblk = lambda width: pl.BlockSpec((tm, width), lambda i: (i, 0))
    full = lambda shape: pl.BlockSpec(shape, lambda i: (0,) * len(shape))
    return pl.pallas_call(
        _out_router_kernel,
        grid=(T // tm,),
        in_specs=[rowblk(D), rowblk(w), rowblk(w), full((w, D)), full((w, D)), full((1, D)),
                  full((D, LANES)), full((D, LANES)), full((1, LANES))],
        out_specs=[rowblk(D), rowblk(D), rowblk(LANES), rowblk(LANES)],
        out_shape=[jax.ShapeDtypeStruct((T, D), F32), jax.ShapeDtypeStruct((T, D), F32),
                   jax.ShapeDtypeStruct((T, LANES), I32), jax.ShapeDtypeStruct((T, LANES), F32)],
        compiler_params=_cparams(("parallel",)),
        name="out_router",
    )(x, oa, ob, wo_a, wo_b, norm2, wr_hi, wr_lo, b_router_p)


def _weight_ring_step(rt, w_hbm, wbuf, sem, tcols):
    te_ref, tv_ref, tf_ref, tnx_ref, tlast_ref, trun_ref, nr_ref = rt
    n = pl.program_id(0)
    m = pl.program_id(1)
    slot = lax.rem(n * nr_ref[0] + trun_ref[m], 2)

    def copy(e, nn, s):
        c0 = pl.multiple_of(nn * tcols, tcols)
        return pltpu.make_async_copy(w_hbm.at[e, :, pl.ds(c0, tcols)], wbuf.at[s], sem.at[s])

    @pl.when((n == 0) & (m == 0))
    def _():
        copy(te_ref[m], n, slot).start()

    copy(te_ref[m], n, slot).wait()
    last = tlast_ref[m] != 0

    @pl.when(jnp.logical_or(jnp.logical_not(last), n + 1 < pl.num_programs(0)))
    def _():
        copy(tnx_ref[m], jnp.where(last, n + 1, n), 1 - slot).start()

    return slot


def _moe_gate_up_kernel(te_ref, tv_ref, tf_ref, tnx_ref, tlast_ref, trun_ref, nr_ref,
                        x_ref, w_hbm, bg_ref, bu_ref, act_ref, wbuf, wc_scr, sem):
    rt = (te_ref, tv_ref, tf_ref, tnx_ref, tlast_ref, trun_ref, nr_ref)
    m = pl.program_id(1)
    tnw = wbuf.shape[2]
    grp = 2 * LANES
    ngrp = tnw // grp

    @pl.when(tv_ref[m] != 0)
    def _():
        @pl.when(tf_ref[m] != 0)
        def _():
            slot = _weight_ring_step(rt, w_hbm, wbuf, sem, tnw)
            src = lax.broadcasted_iota(I32, (grp, grp), 0)
            dst = lax.broadcasted_iota(I32, (grp, grp), 1)
            want = jnp.where(dst < LANES, 2 * dst, 2 * (dst - LANES) + 1)
            perm = jnp.where(src == want, 1.0, 0.0).astype(BF16)
            for g in range(ngrp):
                wt = wbuf[slot, :, g * grp:(g + 1) * grp].astype(BF16)
                wc_scr[:, g * grp:(g + 1) * grp] = jnp.dot(wt, perm, preferred_element_type=F32).astype(BF16)

        x = x_ref[...].astype(BF16)
        bg = bg_ref[0]
        bu = bu_ref[0]
        for g in range(ngrp):
            gu = jnp.dot(x, wc_scr[:, g * grp:(g + 1) * grp], preferred_element_type=F32)
            gate = jnp.minimum(gu[:, 0:LANES] + bg[:, g * LANES:(g + 1) * LANES], SWIGLU_LIMIT)
            up = jnp.clip(gu[:, LANES:grp] + bu[:, g * LANES:(g + 1) * LANES], -SWIGLU_LIMIT, SWIGLU_LIMIT)
            act_ref[:, g * LANES:(g + 1) * LANES] = (
                (up + 1.0) * (gate * _sigmoid(gate * SWIGLU_ALPHA))).astype(act_ref.dtype)

    @pl.when(tv_ref[m] == 0)
    def _():
        act_ref[...] = jnp.zeros(act_ref.shape, act_ref.dtype)


def _moe_gate_up(rt, xs, w_gate_up, bg, bu):
    tm, tnw = MOE_TM, MOE_TNW
    P, D = xs.shape
    F2 = w_gate_up.shape[2]
    imap_b = lambda n, m, te, *_: (te[m], 0, n)
    return pl.pallas_call(
        _moe_gate_up_kernel,
        grid_spec=pltpu.PrefetchScalarGridSpec(
            num_scalar_prefetch=len(rt),
            grid=(F2 // tnw, P // tm),
            in_specs=[
                pl.BlockSpec((tm, D), lambda n, m, *_: (m, 0)),
                pl.BlockSpec(memory_space=pl.ANY),
                pl.BlockSpec((1, 1, tnw // 2), imap_b),
                pl.BlockSpec((1, 1, tnw // 2), imap_b),
            ],
            out_specs=pl.BlockSpec((tm, tnw // 2), lambda n, m, *_: (m, n)),
            scratch_shapes=[pltpu.VMEM((2, D, tnw), F32), pltpu.VMEM((D, tnw), BF16),
                            pltpu.SemaphoreType.DMA((2,))],
        ),
        out_shape=jax.ShapeDtypeStruct((P, F2 // 2), BF16),
        compiler_params=_cparams(("arbitrary", "arbitrary")),
        name="moe_gate_up",
    )(*rt, xs, w_gate_up, bg, bu)


def _moe_down_kernel(te_ref, tv_ref, tf_ref, tnx_ref, tlast_ref, trun_ref, nr_ref,
                     a_ref, w_hbm, bd_ref, y_ref, wbuf, wc_scr, sem):
    rt = (te_ref, tv_ref, tf_ref, tnx_ref, tlast_ref, trun_ref, nr_ref)
    m = pl.program_id(1)

    @pl.when(tv_ref[m] != 0)
    def _():
        @pl.when(tf_ref[m] != 0)
        def _():
            slot = _weight_ring_step(rt, w_hbm, wbuf, sem, wbuf.shape[2])
            wc_scr[...] = wbuf[slot].astype(BF16)

        y_ref[...] = jnp.dot(a_ref[...], wc_scr[...], preferred_element_type=F32) + bd_ref[0]

    @pl.when(tv_ref[m] == 0)
    def _():
        y_ref[...] = jnp.zeros(y_ref.shape, y_ref.dtype)


def _moe_down(rt, act, w_down, bd):
    tm, tn = MOE_TM, MOE_TN
    P, F = act.shape
    D = w_down.shape[2]
    return pl.pallas_call(
        _moe_down_kernel,
        grid_spec=pltpu.PrefetchScalarGridSpec(
            num_scalar_prefetch=len(rt),
            grid=(D // tn, P // tm),
            in_specs=[
                pl.BlockSpec((tm, F), lambda n, m, *_: (m, 0)),
                pl.BlockSpec(memory_space=pl.ANY),
                pl.BlockSpec((1, 1, tn), lambda n, m, te, *_: (te[m], 0, n)),
            ],
            out_specs=pl.BlockSpec((tm, tn), lambda n, m, *_: (m, n)),
            scratch_shapes=[pltpu.VMEM((2, F, tn), F32), pltpu.VMEM((F, tn), BF16),
                            pltpu.SemaphoreType.DMA((2,))],
        ),
        out_shape=jax.ShapeDtypeStruct((P, D), F32),
        compiler_params=_cparams(("arbitrary", "arbitrary")),
        name="moe_down",
    )(*rt, act, w_down, bd)


def _combine_kernel(x1_ref, yg_ref, tw_ref, fn_ref, y_ref):
    tw = tw_ref[...]
    moe = tw[:, 0:1] * yg_ref[0]
    for kk in range(1, TOP_K):
        moe = moe + tw[:, kk:kk + 1] * yg_ref[kk]
    y_ref[...] = _rms(x1_ref[...] + moe, fn_ref[...])


def _combine(x1, yg, tw, final_norm, tm, row_off):
    T, D = x1.shape
    off = row_off // tm
    return pl.pallas_call(
        _combine_kernel,
        grid=(T // tm,),
        in_specs=[
            pl.BlockSpec((tm, D), lambda i: (i, 0)),
            pl.BlockSpec((TOP_K, tm, D), lambda i: (0, i + off, 0)),
            pl.BlockSpec((tm, LANES), lambda i: (i, 0)),
            pl.BlockSpec((1, D), lambda i: (0, 0)),
        ],
        out_specs=pl.BlockSpec((tm, D), lambda i: (i, 0)),
        out_shape=jax.ShapeDtypeStruct((T, D), F32),
        compiler_params=_cparams(("parallel",)),
        name="combine",
    )(x1, yg, tw, final_norm)


def _routing(top_i):
    T = top_i.shape[0]
    P = T * TOP_K
    tm = MOE_TM
    n_tiles = -(-P // tm) + N_EXPERTS
    e_flat = top_i.reshape(P)
    onehot = (e_flat[:, None] == jnp.arange(N_EXPERTS, dtype=I32)[None, :]).astype(I32)
    csum = jnp.cumsum(onehot, axis=0)
    rank = jnp.sum((csum - 1) * onehot, axis=1)
    counts = csum[-1]
    padded = ((counts + tm - 1) // tm) * tm
    ends = jnp.cumsum(padded)
    starts = ends - padded
    pos = starts[e_flat] + rank
    slot_token = jnp.zeros((n_tiles * tm,), I32).at[pos].set(jnp.arange(P, dtype=I32) // TOP_K)
    tile_start = jnp.arange(n_tiles, dtype=I32) * tm
    tile_valid = (tile_start < ends[-1]).astype(I32)
    tile_expert = jnp.minimum(jnp.sum((tile_start[:, None] >= ends[None, :]).astype(I32), axis=1), N_EXPERTS - 1)
    tile_expert = tile_expert.astype(I32)
    tile_first = jnp.concatenate([jnp.ones((1,), I32), (tile_expert[1:] != tile_expert[:-1]).astype(I32)])
    tile_first = tile_first * tile_valid
    tile_run = jnp.cumsum(tile_first) - 1
    n_runs = jnp.sum(tile_first).reshape(1)
    experts = jnp.arange(N_EXPERTS, dtype=I32)
    later = (counts > 0)[None, :] & (experts[None, :] > tile_expert[:, None])
    nxt = jnp.min(jnp.where(later, experts[None, :], N_EXPERTS), axis=1)
    tile_last = (nxt == N_EXPERTS).astype(I32)
    tile_next = jnp.where(nxt == N_EXPERTS, tile_expert[0], nxt).astype(I32)
    tables = (tile_expert, tile_valid, tile_first, tile_next, tile_last, tile_run.astype(I32), n_runs.astype(I32))
    return pos, slot_token, tables


def kernel(x_prompt, x_sample, state_gdn, state_conv, state_hgrn, lb_table, norm1, w_in, conv_w, A_log, dt_bias,
           gdn_norm, hg_norm, w_out, norm2, w_router, b_router, w_gate_up, b_gate_up, w_down, b_down, final_norm):
    depth = w_in.shape[0]
    assert depth == 1
    B, L, D = x_prompt.shape
    SB = x_sample.shape[0]
    assert x_sample.shape[1] == 1
    nqk = HEADS * HD
    cd = 3 * nqk

    lbs = jnp.cumsum(jax.nn.softmax(lb_table.astype(F32), axis=0), axis=0)
    lb = lbs[0:1]

    assert w_in.shape[2] == MAIN_COLS + 2 * HEADS and cd + nqk == MAIN_COLS // 2
    w_main, w_ba = _repack_w_in(w_in[0], 2 * HEADS)
    wo_a = w_out[0, :nqk].astype(BF16)
    wo_b = w_out[0, nqk:].astype(BF16)
    wr_hi, wr_lo = _split(jnp.pad(w_router[0], ((0, 0), (0, LANES - N_EXPERTS))))
    br_p = jnp.pad(b_router[0], (0, LANES - N_EXPERTS))[None, :]
    bg = b_gate_up[0, :, None, 0::2]
    bu = b_gate_up[0, :, None, 1::2]
    bd = b_down[0][:, None, :]
    alog_b = jnp.broadcast_to(A_log[0][:, None, None], (HEADS, 1, HD))
    dtb_b = jnp.broadcast_to(dt_bias[0][:, None, None], (HEADS, 1, HD))
    alog_row = jnp.pad(A_log[0], (0, LANES - HEADS))[None, :]
    dtb_row = jnp.pad(dt_bias[0], (0, LANES - HEADS))[None, :]
    n1 = norm1[0][None, :]
    n2 = norm2[0][None, :]
    gn = gdn_norm[0][None, :]
    hn = hg_norm[0][None, :]
    fnw = final_norm[None, :]
    cw = conv_w[0]

    xp = x_prompt.reshape(B * L, D)
    xs = x_sample.reshape(SB, D)

    proj_p, ba_p = _in_proj(xp, n1, w_main, w_ba, tm=1024, tn=1024)
    proj_s, ba_s = _in_proj(xs, n1, w_main, w_ba, tm=SB, tn=1024)

    oa_p, sg_p = _gdn_prompt(proj_p, ba_p, cw, alog_b, dtb_b, gn, B, L)
    ob_p, sh_p = _hgrn_prompt(proj_p, lb, hn, B, L)
    oa_s, ob_s, sg_s, sh_s = _decode(proj_s, ba_s, state_conv[0].reshape(SB, (CONV_K - 1) * cd), state_gdn[0],
                                     state_hgrn[0], cw, alog_row, dtb_row, lb, gn, hn)

    x1_p, h2_p, ti_p, tw_p = _out_router(xp, oa_p, ob_p, wo_a, wo_b, n2, wr_hi, wr_lo, br_p, tm=256)
    x1_s, h2_s, ti_s, tw_s = _out_router(xs, oa_s, ob_s, wo_a, wo_b, n2, wr_hi, wr_lo, br_p, tm=SB)

    h2 = jnp.concatenate([h2_p, h2_s], axis=0)
    top_i = jnp.concatenate([ti_p, ti_s], axis=0)[:, :TOP_K]
    pos, slot_token, tables = _routing(top_i)
    xs_sorted = h2.at[slot_token].get(mode="promise_in_bounds")
    act = _moe_gate_up(tables, xs_sorted, w_gate_up[0], bg, bu)
    yslots = _moe_down(tables, act, w_down[0], bd)
    pos_kmajor = pos.reshape(B * L + SB, TOP_K).T.reshape(-1)
    yg = yslots.at[pos_kmajor].get(mode="promise_in_bounds").reshape(TOP_K, B * L + SB, D)

    y_p = _combine(x1_p, yg, tw_p, fnw, tm=256, row_off=0)
    y_s = _combine(x1_s, yg, tw_s, fnw, tm=SB, row_off=B * L)

    conv_p = proj_p.reshape(B, L, MAIN_COLS)[:, L - (CONV_K - 1):, :cd]
    conv_s = jnp.concatenate([state_conv[0][:, 1:, :], proj_s[:, None, :cd]], axis=1)
    return (y_p.reshape(B, L, D), y_s.reshape(SB, 1, D),
            sg_p[None], conv_p[None].astype(state_conv.dtype), sh_p[None],
            sg_s[None], conv_s[None].astype(state_conv.dtype), sh_s[None])
```

```python
import functools

import jax
import jax.numpy as jnp
from jax import lax
from jax.experimental import pallas as pl
from jax.experimental.pallas import tpu as pltpu

F32 = jnp.float32
BF16 = jnp.bfloat16
I32 = jnp.int32

EPS = 1e-6
HEADS = 8
HD = 128
CONV_K = 4
N_EXPERTS = 32
TOP_K = 4
SWIGLU_ALPHA = 1.702
SWIGLU_LIMIT = 7.0
LANES = 128
NEG = -1e30

VMEM_LIMIT = 56 * 1024 * 1024

QB, KB, VB, ZB, HQB, HFB, HIB, HGB = (i * HEADS for i in range(8))
MAIN_COLS = 8 * HEADS * HD

HEADS_PER_STEP = 2
GDN_CHUNK = 128
GDN_TB = 512
HG_CHUNK = 16
HG_TB = 256
HG_GROUP = 4
DEC_TB = 8
MOE_TM = 256
MOE_TN = 2048
MOE_TNW = 2048


def _cparams(sem):
    return pltpu.CompilerParams(dimension_semantics=sem, vmem_limit_bytes=VMEM_LIMIT)


def _mm(a, b):
    return jnp.dot(a.astype(BF16), b.astype(BF16), preferred_element_type=F32)


def _mm_nt(a, b):
    return lax.dot_general(a.astype(BF16), b.astype(BF16), (((1,), (1,)), ((), ())),
                           preferred_element_type=F32)


def _mm_tn(a, b):
    return lax.dot_general(a.astype(BF16), b.astype(BF16), (((0,), (0,)), ((), ())),
                           preferred_element_type=F32)


def _mmh(a, b):
    return jnp.dot(a, b, precision=lax.Precision.HIGHEST, preferred_element_type=F32)


def _split(a):
    hi = a.astype(BF16)
    return hi, (a - hi.astype(F32)).astype(BF16)


def _mm3(a, b):
    d = lambda x, y: jnp.dot(x, y, preferred_element_type=F32)
    return d(a[0], b[0]) + (d(a[0], b[1]) + d(a[1], b[0]))


def _mm_exact_lhs(l_bf16, x):
    d = lambda y: jnp.dot(l_bf16, y, preferred_element_type=F32)
    x0 = x.astype(BF16)
    r1 = x - x0.astype(F32)
    x1 = r1.astype(BF16)
    x2 = (r1 - x1.astype(F32)).astype(BF16)
    return d(x0) + (d(x1) + d(x2))


def _sigmoid(x):
    return 1.0 / (1.0 + jnp.exp(-x))


def _silu(x):
    return x * _sigmoid(x)


def _softplus(x):
    return jnp.maximum(x, 0.0) + jnp.log1p(jnp.exp(-jnp.abs(x)))


def _rms(x, w):
    return x * lax.rsqrt(jnp.mean(x * x, axis=-1, keepdims=True) + EPS) * w


def _l2n(x):
    return x * lax.rsqrt(jnp.sum(x * x, axis=-1, keepdims=True) + EPS)


def _repack_kernel(w_ref, wm_ref, wba_ref, *, nba):
    half = wm_ref.shape[1] // 2
    wm_ref[:, 0:half] = w_ref[:, 0:half].astype(BF16)
    wm_ref[:, half:2 * half] = w_ref[:, half + nba:2 * half + nba].astype(BF16)
    first = w_ref[:, half:half + LANES]
    lane = lax.broadcasted_iota(I32, first.shape, 1)
    wba_ref[...] = jnp.where(lane < nba, first, 0.0).astype(BF16)


def _repack_w_in(w, nba):
    D, n = w.shape
    half = (n - nba) // 2
    assert half % LANES == 0 and nba <= LANES
    tr = 128
    return pl.pallas_call(
        functools.partial(_repack_kernel, nba=nba),
        grid=(D // tr,),
        in_specs=[pl.BlockSpec((tr, n), lambda i: (i, 0))],
        out_specs=[pl.BlockSpec((tr, 2 * half), lambda i: (i, 0)), pl.BlockSpec((tr, LANES), lambda i: (i, 0))],
        out_shape=[jax.ShapeDtypeStruct((D, 2 * half), BF16), jax.ShapeDtypeStruct((D, LANES), BF16)],
        compiler_params=_cparams(("parallel",)),
        name="repack_w_in",
    )(w)


def _in_proj_kernel(x_ref, nw_ref, w_ref, wba_ref, o_ref, ba_ref, h_scr):
    @pl.when(pl.program_id(1) == 0)
    def _():
        h = _rms(x_ref[...], nw_ref[...]).astype(BF16)
        h_scr[...] = h
        ba_ref[...] = jnp.dot(h, wba_ref[...], preferred_element_type=F32)

    o_ref[...] = jnp.dot(h_scr[...], w_ref[...], preferred_element_type=F32)


def _in_proj(x, norm_w, w_main, w_ba, tm, tn):
    T, D = x.shape
    N = w_main.shape[1]
    return pl.pallas_call(
        _in_proj_kernel,
        grid=(T // tm, N // tn),
        in_specs=[
            pl.BlockSpec((tm, D), lambda i, j: (i, 0)),
            pl.BlockSpec((1, D), lambda i, j: (0, 0)),
            pl.BlockSpec((D, tn), lambda i, j: (0, j)),
            pl.BlockSpec((D, LANES), lambda i, j: (0, 0)),
        ],
        out_specs=[
            pl.BlockSpec((tm, tn), lambda i, j: (i, j)),
            pl.BlockSpec((tm, LANES), lambda i, j: (i, 0)),
        ],
        out_shape=[jax.ShapeDtypeStruct((T, N), F32), jax.ShapeDtypeStruct((T, LANES), F32)],
        scratch_shapes=[pltpu.VMEM((tm, D), BF16)],
        compiler_params=_cparams(("parallel", "arbitrary")),
        name="in_proj",
    )(x, norm_w, w_main, w_ba)


def _unit_lower_inverse(As, ri, ci):
    n = As[0].shape[0]
    eye = (ri == ci).astype(F32)
    same16 = (ri // 16) == (ci // 16)
    s1 = [_split(jnp.where(same16, -A, 0.0)) for A in As]
    s2 = [_split(_mm3(s, s)) for s in s1]
    s4 = [_split(_mm3(s, s)) for s in s2]
    s8 = [_split(_mm3(s, s)) for s in s4]
    Ts = [eye + jnp.where(same16, -A, 0.0) for A in As]
    for sp in (s2, s4, s8):
        Ts = [T + _mm3(_split(T), s) for T, s in zip(Ts, sp)]
    size = 32
    while size <= n:
        off = ((ri // size) == (ci // size)) & ((ri // (size // 2)) != (ci // (size // 2)))
        sT = [_split(T) for T in Ts]
        TL = [_mm3(st, _split(jnp.where(off, A, 0.0))) for st, A in zip(sT, As)]
        Ts = [T - _mm3(_split(tl), st) for T, tl, st in zip(Ts, TL, sT)]
        size *= 2
    return Ts


def _gdn_prompt_kernel(q_ref, k_ref, v_ref, z_ref, ba_ref, cwq_ref, cwk_ref, cwv_ref, alog_ref, dtb_ref,
                       gn_ref, o_ref, s_ref, ubuf, s_scr, *, tb, chunk, hps):
    hg = pl.program_id(1)
    t = pl.program_id(2)

    @pl.when(t == 0)
    def _():
        ubuf[:, 0:8, :] = jnp.zeros((3, 8, hps * HD), F32)
        s_scr[...] = jnp.zeros((hps, HD, HD), F32)

    ubuf[0, 8:8 + tb, :] = q_ref[...]
    ubuf[1, 8:8 + tb, :] = k_ref[...]
    ubuf[2, 8:8 + tb, :] = v_ref[...]

    ba = ba_ref[...]
    lane = lax.broadcasted_iota(I32, ba.shape, 1)
    beta_cols, g_cols = [], []
    for hh in range(hps):
        h = hg * hps + hh
        b_col = jnp.sum(jnp.where(lane == h, ba, 0.0), axis=-1, keepdims=True)
        a_col = jnp.sum(jnp.where(lane == h + HEADS, ba, 0.0), axis=-1, keepdims=True)
        beta_cols.append(_sigmoid(b_col))
        g_cols.append(-jnp.exp(alog_ref[hh][:, 0:1]) * _softplus(a_col + dtb_ref[hh][:, 0:1]))

    ri = lax.broadcasted_iota(I32, (chunk, chunk), 0)
    ci = lax.broadcasted_iota(I32, (chunk, chunk), 1)
    causal = ci <= ri
    strict = ci < ri
    ltri = jnp.where(causal, 1.0, 0.0).astype(BF16)

    def conv(idx, w_ref, r0, c0):
        w = w_ref[:, c0:c0 + HD]
        acc = ubuf[idx, r0 + 5:r0 + 5 + chunk, c0:c0 + HD] * w[0:1, :]
        for j in range(1, CONV_K):
            acc = acc + ubuf[idx, r0 + 5 + j:r0 + 5 + j + chunk, c0:c0 + HD] * w[j:j + 1, :]
        return _silu(acc)

    nc = tb // chunk
    items = [(hh, c) for c in range(nc) for hh in range(hps)]
    qs, ks, vs, betas, gcols, decays, As = [], [], [], [], [], [], []
    for hh, c in items:
        r0, c0 = c * chunk, hh * HD
        q = _l2n(conv(0, cwq_ref, r0, c0)) * (HD ** -0.5)
        k = _l2n(conv(1, cwk_ref, r0, c0))
        v = conv(2, cwv_ref, r0, c0)
        beta = beta_cols[hh][r0:r0 + chunk, :]
        gcol = _mm_exact_lhs(ltri, jnp.broadcast_to(g_cols[hh][r0:r0 + chunk, :], (chunk, HD)))
        decay = jnp.exp(jnp.where(causal, gcol - gcol.T, NEG))
        kb = k * beta
        qs.append(q); ks.append(k); vs.append(v); betas.append(beta); gcols.append(gcol); decays.append(decay)
        As.append(jnp.where(strict, _mm_nt(kb, k) * decay, 0.0))
    Ts = _unit_lower_inverse(As, ri, ci)
    us, ws, scs = [], [], []
    for n in range(len(items)):
        sT = _split(Ts[n])
        kb = ks[n] * betas[n]
        us.append(_mm3(sT, _split(vs[n] * betas[n])))
        ws.append(_mm3(sT, _split(kb * jnp.exp(gcols[n]))))
        scs.append(_mm_nt(qs[n], ks[n]) * decays[n])

    S = [s_scr[hh] for hh in range(hps)]
    for n, (hh, c) in enumerate(items):
        r0, c0 = c * chunk, hh * HD
        gcol = gcols[n]
        v_new = us[n] - _mm(ws[n], S[hh])
        o = _mm(qs[n] * jnp.exp(gcol), S[hh]) + _mm(scs[n], v_new)
        glast = gcol[chunk - 1:chunk, :]
        S[hh] = S[hh] * jnp.exp(glast) + _mm_tn(ks[n] * jnp.exp(glast - gcol), v_new)
        o_ref[r0:r0 + chunk, c0:c0 + HD] = (
            _rms(o, gn_ref[...]) * _silu(z_ref[r0:r0 + chunk, c0:c0 + HD])).astype(o_ref.dtype)
    for hh in range(hps):
        s_scr[hh] = S[hh]

    ubuf[:, 0:8, :] = ubuf[:, tb:tb + 8, :]

    @pl.when(t == pl.num_programs(2) - 1)
    def _():
        for hh in range(hps):
            s_ref[0, hh] = S[hh]


def _gdn_prompt(proj, ba, conv_w, alog_b, dtb_b, gdn_norm, B, L):
    tb, chunk, hps = GDN_TB, GDN_CHUNK, HEADS_PER_STEP
    nt = L // tb
    w = hps * HD
    row = lambda b, h, t: b * nt + t
    colspec = lambda base: pl.BlockSpec((tb, w), lambda b, h, t: (row(b, h, t), base // hps + h))
    cwspec = lambda base: pl.BlockSpec((CONV_K, w), lambda b, h, t: (0, base // hps + h))
    hvec = pl.BlockSpec((hps, 1, HD), lambda b, h, t: (h, 0, 0))
    return pl.pallas_call(
        functools.partial(_gdn_prompt_kernel, tb=tb, chunk=chunk, hps=hps),
        grid=(B, HEADS // hps, nt),
        in_specs=[
            colspec(QB), colspec(KB), colspec(VB), colspec(ZB),
            pl.BlockSpec((tb, LANES), lambda b, h, t: (row(b, h, t), 0)),
            cwspec(QB), cwspec(KB), cwspec(VB),
            hvec, hvec,
            pl.BlockSpec((1, HD), lambda b, h, t: (0, 0)),
        ],
        out_specs=[
            pl.BlockSpec((tb, w), lambda b, h, t: (row(b, h, t), h)),
            pl.BlockSpec((1, hps, HD, HD), lambda b, h, t: (b, h, 0, 0)),
        ],
        out_shape=[jax.ShapeDtypeStruct((B * L, HEADS * HD), BF16),
                   jax.ShapeDtypeStruct((B, HEADS, HD, HD), F32)],
        scratch_shapes=[pltpu.VMEM((3, tb + 8, w), F32), pltpu.VMEM((hps, HD, HD), F32)],
        compiler_params=_cparams(("parallel", "parallel", "arbitrary")),
        name="gdn_prompt",
    )(proj, proj, proj, proj, ba, conv_w, conv_w, conv_w, alog_b, dtb_b, gdn_norm)


def _hgrn_prompt_kernel(hq_ref, hf_ref, hi_ref, hg_ref, lb_ref, hn_ref, o_ref, s_ref,
                        g_scr, q_scr, k_scr, st_scr, *, tb, chunk, group, hps):
    t = pl.program_id(2)

    @pl.when(t == 0)
    def _():
        st_scr[...] = jnp.zeros((hps, HD, HD), F32)

    lb = lb_ref[...]
    hf = hf_ref[...]
    forget = lb + (1.0 - lb) * _sigmoid(hf)
    k_scr[...] = (1.0 - lb) * _sigmoid(-hf)
    q_scr[...] = _silu(hq_ref[...])
    ri = lax.broadcasted_iota(I32, (LANES, LANES), 0)
    ci = lax.broadcasted_iota(I32, (LANES, LANES), 1)
    lblk = jnp.where(((ri // chunk) == (ci // chunk)) & (ci <= ri), 1.0, 0.0).astype(BF16)
    lf = jnp.log(forget)
    for r in range(tb // LANES):
        g_scr[r * LANES:(r + 1) * LANES, :] = _mm_exact_lhs(lblk, lf[r * LANES:(r + 1) * LANES, :])

    half = chunk // 2
    row8 = lax.broadcasted_iota(I32, (half, HD), 0)
    lane8 = lax.broadcasted_iota(I32, (half, HD), 1)
    zpad = jnp.zeros((half, HD), F32)
    hn = hn_ref[...]

    def body(gi, sts_in):
        base = gi * (group * chunk)
        items = [(hh, cc) for cc in range(group) for hh in range(hps)]
        ni = range(len(items))
        rc = [(base + cc * chunk, hh * HD) for hh, cc in items]
        vs = [hi_ref[pl.ds(r0, chunk), c0:c0 + HD] for r0, c0 in rc]
        vts = [v.T for v in vs]
        blocks = []
        for r0, c0 in rc:
            for r in (r0, r0 + half):
                blocks.append((g_scr[pl.ds(r, half), c0:c0 + HD], q_scr[pl.ds(r, half), c0:c0 + HD],
                               k_scr[pl.ds(r, half), c0:c0 + HD]))
        accs = [zpad] * len(blocks)
        for j in range(half):
            for n, (Gb, qb, kb) in enumerate(blocks):
                e = jnp.exp(jnp.where(row8 >= j, Gb - Gb[j:j + 1, :], NEG))
                col = jnp.sum(e * qb * kb[j:j + 1, :], axis=-1, keepdims=True)
                accs[n] = jnp.where(lane8 == (n % 2) * half + j, col, accs[n])
        Gs = [jnp.concatenate([blocks[2 * n][0], blocks[2 * n + 1][0]], axis=0) for n in ni]
        qs = [jnp.concatenate([blocks[2 * n][1], blocks[2 * n + 1][1]], axis=0) for n in ni]
        ks = [jnp.concatenate([blocks[2 * n][2], blocks[2 * n + 1][2]], axis=0) for n in ni]
        offd = []
        for n in ni:
            Ga, _, ka = blocks[2 * n]
            Gb, qb, _ = blocks[2 * n + 1]
            gmid = Ga[half - 1:half, :]
            kh = jnp.concatenate([ka * jnp.exp(gmid - Ga), zpad], axis=0)
            offd.append(_mm_nt(qb * jnp.exp(Gb - gmid), kh))
        glasts = [Gs[n][chunk - 1:chunk, :] for n in ni]
        incs = [_mm(vts[n], ks[n] * jnp.exp(glasts[n] - Gs[n])) for n in ni]
        intra = []
        for n in ni:
            s = jnp.concatenate([accs[2 * n][:, 0:chunk], accs[2 * n + 1][:, 0:chunk] + offd[n]], axis=0)
            intra.append(_mm(s, vs[n]))
        cur = list(sts_in)
        before = []
        for n, (hh, _) in enumerate(items):
            before.append(cur[hh])
            cur[hh] = cur[hh] * jnp.exp(glasts[n]) + incs[n]
        outs = [_mm_nt(qs[n] * jnp.exp(Gs[n]), before[n]) + intra[n] for n in ni]
        for n, (r0, c0) in enumerate(rc):
            o_ref[pl.ds(r0, chunk), c0:c0 + HD] = (
                _rms(outs[n], hn) * _silu(hg_ref[pl.ds(r0, chunk), c0:c0 + HD])).astype(o_ref.dtype)
        return cur

    sts = [st_scr[hh] for hh in range(hps)]
    for gi in range(tb // (group * chunk)):
        sts = body(gi, sts)
    for hh in range(hps):
        st_scr[hh] = sts[hh]

    @pl.when(t == pl.num_programs(2) - 1)
    def _():
        for hh in range(hps):
            s_ref[0, hh] = sts[hh].T


def _hgrn_prompt(proj, lb, hg_norm, B, L):
    tb, chunk, hps = HG_TB, HG_CHUNK, HEADS_PER_STEP
    nt = L // tb
    w = hps * HD
    row = lambda b, h, t: b * nt + t
    colspec = lambda base: pl.BlockSpec((tb, w), lambda b, h, t: (row(b, h, t), base // hps + h))
    return pl.pallas_call(
        functools.partial(_hgrn_prompt_kernel, tb=tb, chunk=chunk, group=HG_GROUP, hps=hps),
        grid=(B, HEADS // hps, nt),
        in_specs=[
            colspec(HQB), colspec(HFB), colspec(HIB), colspec(HGB),
            pl.BlockSpec((1, w), lambda b, h, t: (0, h)),
            pl.BlockSpec((1, HD), lambda b, h, t: (0, 0)),
        ],
        out_specs=[
            pl.BlockSpec((tb, w), lambda b, h, t: (row(b, h, t), h)),
            pl.BlockSpec((1, hps, HD, HD), lambda b, h, t: (b, h, 0, 0)),
        ],
        out_shape=[jax.ShapeDtypeStruct((B * L, HEADS * HD), BF16),
                   jax.ShapeDtypeStruct((B, HEADS, HD, HD), F32)],
        scratch_shapes=[pltpu.VMEM((tb, w), F32), pltpu.VMEM((tb, w), F32), pltpu.VMEM((tb, w), F32),
                        pltpu.VMEM((hps, HD, HD), F32)],
        compiler_params=_cparams(("parallel", "parallel", "arbitrary")),
        name="hgrn_prompt",
    )(proj, proj, proj, proj, lb, hg_norm)


def _stack_heads_t(x, nb):
    rows = [x[:, h * HD:(h + 1) * HD] for h in range(HEADS)]
    pad = LANES - HEADS * nb
    if pad:
        rows.append(jnp.zeros((pad, HD), F32))
    return jnp.concatenate(rows, axis=0).T


def _decode_kernel(qkv_ref, z_ref, hq_ref, hf_ref, hi_ref, hg_ref, ba_ref, cs_ref, sg_ref, sh_ref,
                   cw_ref, alog_ref, dtb_ref, lb_ref, gn_ref, hn_ref,
                   oa_ref, ob_ref, sgo_ref, sho_ref, *, nb):
    cd = cw_ref.shape[1]
    cw = cw_ref[...]
    cs = cs_ref[...]
    acc = cs[:, 0:cd] * cw[0:1, :]
    acc = acc + cs[:, cd:2 * cd] * cw[1:2, :]
    acc = acc + cs[:, 2 * cd:3 * cd] * cw[2:3, :]
    acc = acc + qkv_ref[...] * cw[3:4, :]
    conv = _silu(acc)
    nqk = HEADS * HD
    ba = ba_ref[...]
    gn = gn_ref[...]
    hn = hn_ref[...]
    z = z_ref[...]
    hgate = hg_ref[...]

    qs, ks, vs = [], [], []
    for h in range(HEADS):
        qs.append(_l2n(conv[:, h * HD:(h + 1) * HD]) * (HD ** -0.5))
        ks.append(_l2n(conv[:, nqk + h * HD:nqk + (h + 1) * HD]))
        vs.append(conv[:, 2 * nqk + h * HD:2 * nqk + (h + 1) * HD])
    qT = _stack_heads_t(jnp.concatenate(qs, axis=1), nb)
    kT = _stack_heads_t(jnp.concatenate(ks, axis=1), nb)

    lb = lb_ref[...]
    hf = hf_ref[...]
    forget = lb + (1.0 - lb) * _sigmoid(hf)
    hk = (1.0 - lb) * _sigmoid(-hf)
    hq = _silu(hq_ref[...])
    hv = hi_ref[...]
    fT = _stack_heads_t(forget, nb)
    hkT = _stack_heads_t(hk, nb)
    hqT = _stack_heads_t(hq, nb)

    for h in range(HEADS):
        beta = _sigmoid(ba[:, h:h + 1])
        g = -jnp.exp(alog_ref[0:1, h:h + 1]) * _softplus(ba[:, HEADS + h:HEADS + h + 1] + dtb_ref[0:1, h:h + 1])
        eg = jnp.exp(g)
        qk = jnp.sum(qs[h] * ks[h], axis=-1, keepdims=True)
        for b in range(nb):
            idx = h * nb + b
            S = sg_ref[b, h]
            kcol = kT[:, idx:idx + 1]
            qcol = qT[:, idx:idx + 1]
            kS = jnp.sum(S * kcol, axis=0, keepdims=True)
            qS = jnp.sum(S * qcol, axis=0, keepdims=True)
            egb = eg[b:b + 1, :]
            v_new = beta[b:b + 1, :] * (vs[h][b:b + 1, :] - egb * kS)
            o = egb * qS + qk[b:b + 1, :] * v_new
            sgo_ref[b, h] = S * egb + kcol * v_new
            oa_ref[b:b + 1, h * HD:(h + 1) * HD] = _rms(o, gn) * _silu(z[b:b + 1, h * HD:(h + 1) * HD])
            Sh = sh_ref[b, h] * fT[:, idx:idx + 1] + hkT[:, idx:idx + 1] * hv[b:b + 1, h * HD:(h + 1) * HD]
            sho_ref[b, h] = Sh
            ob = jnp.sum(Sh * hqT[:, idx:idx + 1], axis=0, keepdims=True)
            ob_ref[b:b + 1, h * HD:(h + 1) * HD] = _rms(ob, hn) * _silu(hgate[b:b + 1, h * HD:(h + 1) * HD])


def _decode(proj, ba, conv_state2d, s_gdn, s_hg, conv_w, alog_row, dtb_row, lb, gdn_norm, hg_norm):
    nb = DEC_TB
    T = proj.shape[0]
    cd = conv_w.shape[1]
    w = HEADS * HD
    cblk = lambda width, idx: pl.BlockSpec((nb, width), lambda i: (i, idx))
    sblk = pl.BlockSpec((nb, HEADS, HD, HD), lambda i: (i, 0, 0, 0))
    full = lambda shape: pl.BlockSpec(shape, lambda i: (0,) * len(shape))
    return pl.pallas_call(
        functools.partial(_decode_kernel, nb=nb),
        grid=(T // nb,),
        in_specs=[
            cblk(cd, 0), cblk(w, ZB // HEADS), cblk(w, HQB // HEADS), cblk(w, HFB // HEADS),
            cblk(w, HIB // HEADS), cblk(w, HGB // HEADS),
            cblk(LANES, 0), cblk(3 * cd, 0), sblk, sblk,
            full((CONV_K, cd)), full((1, LANES)), full((1, LANES)), full((1, w)), full((1, HD)), full((1, HD)),
        ],
        out_specs=[cblk(w, 0), cblk(w, 0), sblk, sblk],
        out_shape=[jax.ShapeDtypeStruct((T, w), F32), jax.ShapeDtypeStruct((T, w), F32),
                   jax.ShapeDtypeStruct(s_gdn.shape, F32), jax.ShapeDtypeStruct(s_hg.shape, F32)],
        compiler_params=_cparams(("parallel",)),
        name="decode",
    )(proj, proj, proj, proj, proj, proj, ba, conv_state2d, s_gdn, s_hg,
      conv_w, alog_row, dtb_row, lb, gdn_norm, hg_norm)


def _out_router_kernel(x_ref, oa_ref, ob_ref, woa_ref, wob_ref, n2_ref, wrh_ref, wrl_ref, br_ref,
                       x1_ref, h2_ref, ti_ref, tw_ref, *, nvalid):
    i = pl.program_id(0)

    @pl.when(i >= nvalid)
    def _():
        h2_ref[...] = jnp.zeros(h2_ref.shape, h2_ref.dtype)
        ti_ref[...] = jnp.zeros(ti_ref.shape, ti_ref.dtype)

    @pl.when(i < nvalid)
    def _():
        _out_router_tile(x_ref, oa_ref, ob_ref, woa_ref, wob_ref, n2_ref, wrh_ref, wrl_ref, br_ref,
                         x1_ref, h2_ref, ti_ref, tw_ref)


def _out_router_tile(x_ref, oa_ref, ob_ref, woa_ref, wob_ref, n2_ref, wrh_ref, wrl_ref, br_ref,
                     x1_ref, h2_ref, ti_ref, tw_ref):
    y = x_ref[...] + jnp.dot(oa_ref[...].astype(BF16), woa_ref[...], preferred_element_type=F32)
    y = y + jnp.dot(ob_ref[...].astype(BF16), wob_ref[...], preferred_element_type=F32)
    x1_ref[...] = y
    h2 = _rms(y, n2_ref[...])
    h2_ref[...] = h2
    logits = _mm3(_split(h2), (wrh_ref[...], wrl_ref[...])) + br_ref[...]
    lane = lax.broadcasted_iota(I32, logits.shape, 1)
    logits = jnp.where(lane < N_EXPERTS, logits, NEG)
    ti = jnp.zeros(logits.shape, I32)
    tw = jnp.zeros(logits.shape, F32)
    m0 = None
    for kk in range(TOP_K):
        m = jnp.max(logits, axis=-1, keepdims=True)
        idx = jnp.min(jnp.where(logits == m, lane, LANES), axis=-1, keepdims=True)
        if m0 is None:
            m0 = m
        ti = jnp.where(lane == kk, idx, ti)
        tw = jnp.where(lane == kk, jnp.exp(m - m0), tw)
        logits = jnp.where(lane == idx, NEG * 2.0, logits)
    tw_ref[...] = tw / jnp.sum(tw, axis=-1, keepdims=True)
    ti_ref[...] = ti


def _out_router_into_kernel(h2_all_ref, ti_all_ref, *refs, nvalid):
    del h2_all_ref, ti_all_ref
    _out_router_kernel(*refs, nvalid=nvalid)


def _out_router(x, oa, ob, wo_a, wo_b, norm2, wr_hi, wr_lo, b_router_p, tm, total_rows, row_off, into=None):
    T, D = x.shape
    w = oa.shape[1]
    off = row_off // tm
    nvalid = T // tm
    steps = nvalid if into is not None else -(-total_rows // tm)
    rowblk = lambda width: pl.BlockSpec((tm, width), lambda i: (jnp.minimum(i, nvalid - 1), 0))
    allblk = lambda width: pl.BlockSpec((tm, width), lambda i: (i + off, 0))
    full = lambda shape: pl.BlockSpec(shape, lambda i: (0,) * len(shape))
    in_specs = [rowblk(D), rowblk(w), rowblk(w), full((w, D)), full((w, D)), full((1, D)),
                full((D, LANES)), full((D, LANES)), full((1, LANES))]
    args = (x, oa, ob, wo_a, wo_b, norm2, wr_hi, wr_lo, b_router_p)
    body, aliases = _out_router_kernel, {}
    if into is not None:
        in_specs = [pl.BlockSpec(memory_space=pl.ANY)] * 2 + in_specs
        args = tuple(into) + args
        body, aliases = _out_router_into_kernel, {0: 1, 1: 2}
    return pl.pallas_call(
        functools.partial(body, nvalid=nvalid),
        grid=(steps,),
        in_specs=in_specs,
        out_specs=[rowblk(D), allblk(D), allblk(LANES), rowblk(LANES)],
        out_shape=[jax.ShapeDtypeStruct((T, D), F32), jax.ShapeDtypeStruct((total_rows, D), F32),
                   jax.ShapeDtypeStruct((total_rows, LANES), I32), jax.ShapeDtypeStruct((T, LANES), F32)],
        input_output_aliases=aliases,
        compiler_params=_cparams(("arbitrary",)),
        name="out_router",
    )(*args)


def _weight_ring_step(rt, w_hbm, wbuf, sem, tcols):
    te_ref, tv_ref, tf_ref, tnx_ref, tlast_ref, trun_ref, nr_ref = rt
    n = pl.program_id(0)
    m = pl.program_id(1)
    slot = lax.rem(n * nr_ref[0] + trun_ref[m], 2)

    def copy(e, nn, s):
        c0 = pl.multiple_of(nn * tcols, tcols)
        return pltpu.make_async_copy(w_hbm.at[e, :, pl.ds(c0, tcols)], wbuf.at[s], sem.at[s])

    @pl.when((n == 0) & (m == 0))
    def _():
        copy(te_ref[m], n, slot).start()

    copy(te_ref[m], n, slot).wait()
    last = tlast_ref[m] != 0

    @pl.when(jnp.logical_or(jnp.logical_not(last), n + 1 < pl.num_programs(0)))
    def _():
        copy(tnx_ref[m], jnp.where(last, n + 1, n), 1 - slot).start()

    return slot


def _moe_gate_up_kernel(te_ref, tv_ref, tf_ref, tnx_ref, tlast_ref, trun_ref, nr_ref,
                        x_ref, w_hbm, bg_ref, bu_ref, act_ref, wbuf, wc_scr, sem):
    rt = (te_ref, tv_ref, tf_ref, tnx_ref, tlast_ref, trun_ref, nr_ref)
    m = pl.program_id(1)
    tnw = wbuf.shape[2]
    grp = 2 * LANES
    ngrp = tnw // grp

    @pl.when(tv_ref[m] != 0)
    def _():
        @pl.when(tf_ref[m] != 0)
        def _():
            slot = _weight_ring_step(rt, w_hbm, wbuf, sem, tnw)
            src = lax.broadcasted_iota(I32, (grp, grp), 0)
            dst = lax.broadcasted_iota(I32, (grp, grp), 1)
            want = jnp.where(dst < LANES, 2 * dst, 2 * (dst - LANES) + 1)
            perm = jnp.where(src == want, 1.0, 0.0).astype(BF16)
            for g in range(ngrp):
                wt = wbuf[slot, :, g * grp:(g + 1) * grp].astype(BF16)
                wc_scr[:, g * grp:(g + 1) * grp] = jnp.dot(wt, perm, preferred_element_type=F32).astype(BF16)

        x = x_ref[...].astype(BF16)
        bg = bg_ref[0]
        bu = bu_ref[0]
        for g in range(ngrp):
            gu = jnp.dot(x, wc_scr[:, g * grp:(g + 1) * grp], preferred_element_type=F32)
            gate = jnp.minimum(gu[:, 0:LANES] + bg[:, g * LANES:(g + 1) * LANES], SWIGLU_LIMIT)
            up = jnp.clip(gu[:, LANES:grp] + bu[:, g * LANES:(g + 1) * LANES], -SWIGLU_LIMIT, SWIGLU_LIMIT)
            act_ref[:, g * LANES:(g + 1) * LANES] = (
                (up + 1.0) * (gate * _sigmoid(gate * SWIGLU_ALPHA))).astype(act_ref.dtype)

    @pl.when(tv_ref[m] == 0)
    def _():
        act_ref[...] = jnp.zeros(act_ref.shape, act_ref.dtype)


def _moe_gate_up(rt, xs, w_gate_up, bg, bu):
    tm, tnw = MOE_TM, MOE_TNW
    P, D = xs.shape
    F2 = w_gate_up.shape[2]
    imap_b = lambda n, m, te, *_: (te[m], 0, n)
    return pl.pallas_call(
        _moe_gate_up_kernel,
        grid_spec=pltpu.PrefetchScalarGridSpec(
            num_scalar_prefetch=len(rt),
            grid=(F2 // tnw, P // tm),
            in_specs=[
                pl.BlockSpec((tm, D), lambda n, m, *_: (m, 0)),
                pl.BlockSpec(memory_space=pl.ANY),
                pl.BlockSpec((1, 1, tnw // 2), imap_b),
                pl.BlockSpec((1, 1, tnw // 2), imap_b),
            ],
            out_specs=pl.BlockSpec((tm, tnw // 2), lambda n, m, *_: (m, n)),
            scratch_shapes=[pltpu.VMEM((2, D, tnw), F32), pltpu.VMEM((D, tnw), BF16),
                            pltpu.SemaphoreType.DMA((2,))],
        ),
        out_shape=jax.ShapeDtypeStruct((P, F2 // 2), BF16),
        compiler_params=_cparams(("arbitrary", "arbitrary")),
        name="moe_gate_up",
    )(*rt, xs, w_gate_up, bg, bu)


def _moe_down_kernel(te_ref, tv_ref, tf_ref, tnx_ref, tlast_ref, trun_ref, nr_ref,
                     a_ref, w_hbm, bd_ref, y_ref, wbuf, wc_scr, sem):
    rt = (te_ref, tv_ref, tf_ref, tnx_ref, tlast_ref, trun_ref, nr_ref)
    m = pl.program_id(1)

    @pl.when(tv_ref[m] != 0)
    def _():
        @pl.when(tf_ref[m] != 0)
        def _():
            slot = _weight_ring_step(rt, w_hbm, wbuf, sem, wbuf.shape[2])
            wc_scr[...] = wbuf[slot].astype(BF16)

        y_ref[...] = jnp.dot(a_ref[...], wc_scr[...], preferred_element_type=F32) + bd_ref[0]

    @pl.when(tv_ref[m] == 0)
    def _():
        y_ref[...] = jnp.zeros(y_ref.shape, y_ref.dtype)


def _moe_down(rt, act, w_down, bd):
    tm, tn = MOE_TM, MOE_TN
    P, F = act.shape
    D = w_down.shape[2]
    return pl.pallas_call(
        _moe_down_kernel,
        grid_spec=pltpu.PrefetchScalarGridSpec(
            num_scalar_prefetch=len(rt),
            grid=(D // tn, P // tm),
            in_specs=[
                pl.BlockSpec((tm, F), lambda n, m, *_: (m, 0)),
                pl.BlockSpec(memory_space=pl.ANY),
                pl.BlockSpec((1, 1, tn), lambda n, m, te, *_: (te[m], 0, n)),
            ],
            out_specs=pl.BlockSpec((tm, tn), lambda n, m, *_: (m, n)),
            scratch_shapes=[pltpu.VMEM((2, F, tn), F32), pltpu.VMEM((F, tn), BF16),
                            pltpu.SemaphoreType.DMA((2,))],
        ),
        out_shape=jax.ShapeDtypeStruct((P, D), F32),
        compiler_params=_cparams(("arbitrary", "arbitrary")),
        name="moe_down",
    )(*rt, act, w_down, bd)


def _combine_kernel(x1_ref, yg_ref, tw_ref, fn_ref, y_ref):
    tw = tw_ref[...]
    moe = tw[:, 0:1] * yg_ref[0]
    for kk in range(1, TOP_K):
        moe = moe + tw[:, kk:kk + 1] * yg_ref[kk]
    y_ref[...] = _rms(x1_ref[...] + moe, fn_ref[...])


def _combine(x1, yg, tw, final_norm, tm, row_off):
    T, D = x1.shape
    off = row_off // tm
    return pl.pallas_call(
        _combine_kernel,
        grid=(T // tm,),
        in_specs=[
            pl.BlockSpec((tm, D), lambda i: (i, 0)),
            pl.BlockSpec((TOP_K, tm, D), lambda i: (0, i + off, 0)),
            pl.BlockSpec((tm, LANES), lambda i: (i, 0)),
            pl.BlockSpec((1, D), lambda i: (0, 0)),
        ],
        out_specs=pl.BlockSpec((tm, D), lambda i: (i, 0)),
        out_shape=jax.ShapeDtypeStruct((T, D), F32),
        compiler_params=_cparams(("parallel",)),
        name="combine",
    )(x1, yg, tw, final_norm)


def _routing(top_i):
    T = top_i.shape[0]
    P = T * TOP_K
    tm = MOE_TM
    n_tiles = -(-P // tm) + N_EXPERTS
    e_flat = top_i.reshape(P)
    onehot = (e_flat[:, None] == jnp.arange(N_EXPERTS, dtype=I32)[None, :]).astype(I32)
    csum = jnp.cumsum(onehot, axis=0)
    rank = jnp.sum((csum - 1) * onehot, axis=1)
    counts = csum[-1]
    padded = ((counts + tm - 1) // tm) * tm
    ends = jnp.cumsum(padded)
    starts = ends - padded
    pos = starts[e_flat] + rank
    slot_token = jnp.zeros((n_tiles * tm,), I32).at[pos].set(jnp.arange(P, dtype=I32) // TOP_K)
    tile_start = jnp.arange(n_tiles, dtype=I32) * tm
    tile_valid = (tile_start < ends[-1]).astype(I32)
    tile_expert = jnp.minimum(jnp.sum((tile_start[:, None] >= ends[None, :]).astype(I32), axis=1), N_EXPERTS - 1)
    tile_expert = tile_expert.astype(I32)
    tile_first = jnp.concatenate([jnp.ones((1,), I32), (tile_expert[1:] != tile_expert[:-1]).astype(I32)])
    tile_first = tile_first * tile_valid
    tile_run = jnp.cumsum(tile_first) - 1
    n_runs = jnp.sum(tile_first).reshape(1)
    experts = jnp.arange(N_EXPERTS, dtype=I32)
    later = (counts > 0)[None, :] & (experts[None, :] > tile_expert[:, None])
    nxt = jnp.min(jnp.where(later, experts[None, :], N_EXPERTS), axis=1)
    tile_last = (nxt == N_EXPERTS).astype(I32)
    tile_next = jnp.where(nxt == N_EXPERTS, tile_expert[0], nxt).astype(I32)
    tables = (tile_expert, tile_valid, tile_first, tile_next, tile_last, tile_run.astype(I32), n_runs.astype(I32))
    return pos, slot_token, tables


def kernel(x_prompt, x_sample, state_gdn, state_conv, state_hgrn, lb_table, norm1, w_in, conv_w, A_log, dt_bias,
           gdn_norm, hg_norm, w_out, norm2, w_router, b_router, w_gate_up, b_gate_up, w_down, b_down, final_norm):
    depth = w_in.shape[0]
    assert depth == 1
    B, L, D = x_prompt.shape
    SB = x_sample.shape[0]
    assert x_sample.shape[1] == 1
    nqk = HEADS * HD
    cd = 3 * nqk

    lbs = jnp.cumsum(jax.nn.softmax(lb_table.astype(F32), axis=0), axis=0)
    lb = lbs[0:1]

    assert w_in.shape[2] == MAIN_COLS + 2 * HEADS and cd + nqk == MAIN_COLS // 2
    w_main, w_ba = _repack_w_in(w_in[0], 2 * HEADS)
    wo_a = w_out[0, :nqk].astype(BF16)
    wo_b = w_out[0, nqk:].astype(BF16)
    wr_hi, wr_lo = _split(jnp.pad(w_router[0], ((0, 0), (0, LANES - N_EXPERTS))))
    br_p = jnp.pad(b_router[0], (0, LANES - N_EXPERTS))[None, :]
    bg = b_gate_up[0, :, None, 0::2]
    bu = b_gate_up[0, :, None, 1::2]
    bd = b_down[0][:, None, :]
    alog_b = jnp.broadcast_to(A_log[0][:, None, None], (HEADS, 1, HD))
    dtb_b = jnp.broadcast_to(dt_bias[0][:, None, None], (HEADS, 1, HD))
    alog_row = jnp.pad(A_log[0], (0, LANES - HEADS))[None, :]
    dtb_row = jnp.pad(dt_bias[0], (0, LANES - HEADS))[None, :]
    n1 = norm1[0][None, :]
    n2 = norm2[0][None, :]
    gn = gdn_norm[0][None, :]
    hn = hg_norm[0][None, :]
    fnw = final_norm[None, :]
    cw = conv_w[0]

    xp = x_prompt.reshape(B * L, D)
    xs = x_sample.reshape(SB, D)

    proj_p, ba_p = _in_proj(xp, n1, w_main, w_ba, tm=1024, tn=1024)
    proj_s, ba_s = _in_proj(xs, n1, w_main, w_ba, tm=SB, tn=1024)

    oa_p, sg_p = _gdn_prompt(proj_p, ba_p, cw, alog_b, dtb_b, gn, B, L)
    ob_p, sh_p = _hgrn_prompt(proj_p, lb, hn, B, L)
    oa_s, ob_s, sg_s, sh_s = _decode(proj_s, ba_s, state_conv[0].reshape(SB, (CONV_K - 1) * cd), state_gdn[0],
                                     state_hgrn[0], cw, alog_row, dtb_row, lb, gn, hn)

    ntok = B * L + SB
    x1_p, h2, ti, tw_p = _out_router(xp, oa_p, ob_p, wo_a, wo_b, n2, wr_hi, wr_lo, br_p,
                                     tm=256, total_rows=ntok, row_off=0)
    x1_s, h2, ti, tw_s = _out_router(xs, oa_s, ob_s, wo_a, wo_b, n2, wr_hi, wr_lo, br_p,
                                     tm=SB, total_rows=ntok, row_off=B * L, into=(h2, ti))

    top_i = ti[:, :TOP_K]
    pos, slot_token, tables = _routing(top_i)
    xs_sorted = h2.at[slot_token].get(mode="promise_in_bounds")
    act = _moe_gate_up(tables, xs_sorted, w_gate_up[0], bg, bu)
    yslots = _moe_down(tables, act, w_down[0], bd)
    pos_kmajor = pos.reshape(B * L + SB, TOP_K).T.reshape(-1)
    yg = yslots.at[pos_kmajor].get(mode="promise_in_bounds").reshape(TOP_K, B * L + SB, D)

    y_p = _combine(x1_p, yg, tw_p, fnw, tm=256, row_off=0)
    y_s = _combine(x1_s, yg, tw_s, fnw, tm=SB, row_off=B * L)

    conv_p = proj_p.reshape(B, L, MAIN_COLS)[:, L - (CONV_K - 1):, :cd]
    conv_s = jnp.concatenate([state_conv[0][:, 1:, :], proj_s[:, None, :cd]], axis=1)
    return (y_p.reshape(B, L, D), y_s.reshape(SB, 1, D),
            sg_p[None], conv_p[None].astype(state_conv.dtype), sh_p[None],
            sg_s[None], conv_s[None].astype(state_conv.dtype), sh_s[None])
```

```python
import functools

import jax
import jax.numpy as jnp
from jax import lax
from jax.experimental import pallas as pl
from jax.experimental.pallas import tpu as pltpu

F32 = jnp.float32
BF16 = jnp.bfloat16
I32 = jnp.int32

EPS = 1e-6
HEADS = 8
HD = 128
CONV_K = 4
N_EXPERTS = 32
TOP_K = 4
SWIGLU_ALPHA = 1.702
SWIGLU_LIMIT = 7.0
LANES = 128
NEG = -1e30

VMEM_LIMIT = 56 * 1024 * 1024

QB, KB, VB, ZB, HQB, HFB, HIB, HGB = (i * HEADS for i in range(8))
MAIN_COLS = 8 * HEADS * HD

HEADS_PER_STEP = 2
GDN_CHUNK = 128
GDN_TB = 512
HG_CHUNK = 16
HG_TB = 512
HG_GROUP = 4
DEC_TB = 8
MOE_TM = 256
MOE_TN = 2048
MOE_TNW = 2048


def _cparams(sem):
    return pltpu.CompilerParams(dimension_semantics=sem, vmem_limit_bytes=VMEM_LIMIT)


def _mm(a, b):
    return jnp.dot(a.astype(BF16), b.astype(BF16), preferred_element_type=F32)


def _mm_nt(a, b):
    return lax.dot_general(a.astype(BF16), b.astype(BF16), (((1,), (1,)), ((), ())),
                           preferred_element_type=F32)


def _mm_tn(a, b):
    return lax.dot_general(a.astype(BF16), b.astype(BF16), (((0,), (0,)), ((), ())),
                           preferred_element_type=F32)


def _mmh(a, b):
    return jnp.dot(a, b, precision=lax.Precision.HIGHEST, preferred_element_type=F32)


def _split(a):
    hi = a.astype(BF16)
    return hi, (a - hi.astype(F32)).astype(BF16)


def _mm3(a, b):
    d = lambda x, y: jnp.dot(x, y, preferred_element_type=F32)
    return d(a[0], b[0]) + (d(a[0], b[1]) + d(a[1], b[0]))


def _mm_exact_lhs(l_bf16, x):
    d = lambda y: jnp.dot(l_bf16, y, preferred_element_type=F32)
    x0 = x.astype(BF16)
    r1 = x - x0.astype(F32)
    x1 = r1.astype(BF16)
    x2 = (r1 - x1.astype(F32)).astype(BF16)
    return d(x0) + (d(x1) + d(x2))


def _sigmoid(x):
    return 1.0 / (1.0 + jnp.exp(-x))


def _sigmoid_t(x):
    return 0.5 * jnp.tanh(0.5 * x) + 0.5


def _silu(x):
    return x * _sigmoid_t(x)


def _softplus(x):
    return jnp.maximum(x, 0.0) + jnp.log1p(jnp.exp(-jnp.abs(x)))


def _rms(x, w):
    return x * lax.rsqrt(jnp.mean(x * x, axis=-1, keepdims=True) + EPS) * w


def _l2n(x):
    return x * lax.rsqrt(jnp.sum(x * x, axis=-1, keepdims=True) + EPS)


def _repack_kernel(w_ref, wm_ref, wba_ref, *, nba):
    half = wm_ref.shape[1] // 2
    wm_ref[:, 0:half] = w_ref[:, 0:half].astype(BF16)
    wm_ref[:, half:2 * half] = w_ref[:, half + nba:2 * half + nba].astype(BF16)
    first = w_ref[:, half:half + LANES]
    lane = lax.broadcasted_iota(I32, first.shape, 1)
    wba_ref[...] = jnp.where(lane < nba, first, 0.0).astype(BF16)


def _repack_w_in(w, nba):
    D, n = w.shape
    half = (n - nba) // 2
    assert half % LANES == 0 and nba <= LANES
    tr = 128
    return pl.pallas_call(
        functools.partial(_repack_kernel, nba=nba),
        grid=(D // tr,),
        in_specs=[pl.BlockSpec((tr, n), lambda i: (i, 0))],
        out_specs=[pl.BlockSpec((tr, 2 * half), lambda i: (i, 0)), pl.BlockSpec((tr, LANES), lambda i: (i, 0))],
        out_shape=[jax.ShapeDtypeStruct((D, 2 * half), BF16), jax.ShapeDtypeStruct((D, LANES), BF16)],
        compiler_params=_cparams(("parallel",)),
        name="repack_w_in",
    )(w)


def _in_proj_kernel(x_ref, nw_ref, w_ref, wba_ref, o_ref, ba_ref, h_scr):
    @pl.when(pl.program_id(1) == 0)
    def _():
        h = _rms(x_ref[...], nw_ref[...]).astype(BF16)
        h_scr[...] = h
        ba_ref[...] = jnp.dot(h, wba_ref[...], preferred_element_type=F32)

    o_ref[...] = jnp.dot(h_scr[...], w_ref[...], preferred_element_type=F32)


def _in_proj(x, norm_w, w_main, w_ba, tm, tn):
    T, D = x.shape
    N = w_main.shape[1]
    return pl.pallas_call(
        _in_proj_kernel,
        grid=(T // tm, N // tn),
        in_specs=[
            pl.BlockSpec((tm, D), lambda i, j: (i, 0)),
            pl.BlockSpec((1, D), lambda i, j: (0, 0)),
            pl.BlockSpec((D, tn), lambda i, j: (0, j)),
            pl.BlockSpec((D, LANES), lambda i, j: (0, 0)),
        ],
        out_specs=[
            pl.BlockSpec((tm, tn), lambda i, j: (i, j)),
            pl.BlockSpec((tm, LANES), lambda i, j: (i, 0)),
        ],
        out_shape=[jax.ShapeDtypeStruct((T, N), F32), jax.ShapeDtypeStruct((T, LANES), F32)],
        scratch_shapes=[pltpu.VMEM((tm, D), BF16)],
        compiler_params=_cparams(("parallel", "arbitrary")),
        name="in_proj",
    )(x, norm_w, w_main, w_ba)


def _unit_lower_inverse(As, ri, ci):
    n = As[0].shape[0]
    eye = (ri == ci).astype(F32)
    same16 = (ri // 16) == (ci // 16)
    s1 = [_split(jnp.where(same16, -A, 0.0)) for A in As]
    s2 = [_split(_mm3(s, s)) for s in s1]
    s4 = [_split(_mm3(s, s)) for s in s2]
    s8 = [_split(_mm3(s, s)) for s in s4]
    Ts = [eye + jnp.where(same16, -A, 0.0) for A in As]
    for sp in (s2, s4, s8):
        Ts = [T + _mm3(_split(T), s) for T, s in zip(Ts, sp)]
    size = 32
    while size <= n:
        off = ((ri // size) == (ci // size)) & ((ri // (size // 2)) != (ci // (size // 2)))
        sT = [_split(T) for T in Ts]
        TL = [_mm3(st, _split(jnp.where(off, A, 0.0))) for st, A in zip(sT, As)]
        Ts = [T - _mm3(_split(tl), st) for T, tl, st in zip(Ts, TL, sT)]
        size *= 2
    return Ts


def _gdn_prompt_kernel(q_ref, k_ref, v_ref, z_ref, ba_ref, cwq_ref, cwk_ref, cwv_ref, alog_ref, dtb_ref,
                       gn_ref, o_ref, s_ref, ubuf, s_scr, *, tb, chunk, hps):
    hg = pl.program_id(1)
    t = pl.program_id(2)

    @pl.when(t == 0)
    def _():
        ubuf[:, 0:8, :] = jnp.zeros((3, 8, hps * HD), F32)
        s_scr[...] = jnp.zeros((hps, HD, HD), F32)

    ubuf[0, 8:8 + tb, :] = q_ref[...]
    ubuf[1, 8:8 + tb, :] = k_ref[...]
    ubuf[2, 8:8 + tb, :] = v_ref[...]

    ba = ba_ref[...]
    lane = lax.broadcasted_iota(I32, (chunk, LANES), 1)
    beta_all = _sigmoid_t(ba)
    g_all = -jnp.exp(alog_ref[...]) * _softplus(ba + dtb_ref[...])

    ri = lax.broadcasted_iota(I32, (chunk, chunk), 0)
    ci = lax.broadcasted_iota(I32, (chunk, chunk), 1)
    causal = ci <= ri
    strict = ci < ri
    ltri = jnp.where(causal, 1.0, 0.0).astype(BF16)

    def conv(idx, w_ref, r0, c0):
        w = w_ref[:, c0:c0 + HD]
        acc = ubuf[idx, r0 + 5:r0 + 5 + chunk, c0:c0 + HD] * w[0:1, :]
        for j in range(1, CONV_K):
            acc = acc + ubuf[idx, r0 + 5 + j:r0 + 5 + j + chunk, c0:c0 + HD] * w[j:j + 1, :]
        return _silu(acc)

    nc = tb // chunk
    items = [(hh, c) for c in range(nc) for hh in range(hps)]
    qs, ks, vs, betas, gcols, decays, As = [], [], [], [], [], [], []
    gc_all = [_mm_exact_lhs(ltri, g_all[c * chunk:(c + 1) * chunk, :]) for c in range(nc)]
    for hh, c in items:
        r0, c0 = c * chunk, hh * HD
        h = hg * hps + hh
        q = _l2n(conv(0, cwq_ref, r0, c0)) * (HD ** -0.5)
        k = _l2n(conv(1, cwk_ref, r0, c0))
        v = conv(2, cwv_ref, r0, c0)
        beta = jnp.sum(jnp.where(lane == h, beta_all[r0:r0 + chunk, :], 0.0), axis=-1, keepdims=True)
        gcol = jnp.broadcast_to(
            jnp.sum(jnp.where(lane == h + HEADS, gc_all[c], 0.0), axis=-1, keepdims=True), (chunk, HD))
        decay = jnp.exp(jnp.where(causal, gcol - gcol.T, NEG))
        kb = k * beta
        qs.append(q); ks.append(k); vs.append(v); betas.append(beta); gcols.append(gcol); decays.append(decay)
        As.append(jnp.where(strict, _mm_nt(kb, k) * decay, 0.0))
    Ts = _unit_lower_inverse(As, ri, ci)
    us, ws, scs = [], [], []
    for n in range(len(items)):
        sT = _split(Ts[n])
        kb = ks[n] * betas[n]
        us.append(_mm3(sT, _split(vs[n] * betas[n])))
        ws.append(_mm3(sT, _split(kb * jnp.exp(gcols[n]))))
        scs.append(_mm_nt(qs[n], ks[n]) * decays[n])

    S = [s_scr[hh] for hh in range(hps)]
    for n, (hh, c) in enumerate(items):
        r0, c0 = c * chunk, hh * HD
        gcol = gcols[n]
        v_new = us[n] - _mm(ws[n], S[hh])
        o = _mm(qs[n] * jnp.exp(gcol), S[hh]) + _mm(scs[n], v_new)
        glast = gcol[chunk - 1:chunk, :]
        S[hh] = S[hh] * jnp.exp(glast) + _mm_tn(ks[n] * jnp.exp(glast - gcol), v_new)
        o_ref[r0:r0 + chunk, c0:c0 + HD] = (
            _rms(o, gn_ref[...]) * _silu(z_ref[r0:r0 + chunk, c0:c0 + HD])).astype(o_ref.dtype)
    for hh in range(hps):
        s_scr[hh] = S[hh]

    ubuf[:, 0:8, :] = ubuf[:, tb:tb + 8, :]

    @pl.when(t == pl.num_programs(2) - 1)
    def _():
        for hh in range(hps):
            s_ref[0, hh] = S[hh]


def _gdn_prompt(proj, ba, conv_w, alog_b, dtb_b, gdn_norm, B, L):
    tb, chunk, hps = GDN_TB, GDN_CHUNK, HEADS_PER_STEP
    nt = L // tb
    w = hps * HD
    row = lambda b, h, t: b * nt + t
    colspec = lambda base: pl.BlockSpec((tb, w), lambda b, h, t: (row(b, h, t), base // hps + h))
    cwspec = lambda base: pl.BlockSpec((CONV_K, w), lambda b, h, t: (0, base // hps + h))
    hvec = pl.BlockSpec((1, LANES), lambda b, h, t: (0, 0))
    return pl.pallas_call(
        functools.partial(_gdn_prompt_kernel, tb=tb, chunk=chunk, hps=hps),
        grid=(B, HEADS // hps, nt),
        in_specs=[
            colspec(QB), colspec(KB), colspec(VB), colspec(ZB),
            pl.BlockSpec((tb, LANES), lambda b, h, t: (row(b, h, t), 0)),
            cwspec(QB), cwspec(KB), cwspec(VB),
            hvec, hvec,
            pl.BlockSpec((1, HD), lambda b, h, t: (0, 0)),
        ],
        out_specs=[
            pl.BlockSpec((tb, w), lambda b, h, t: (row(b, h, t), h)),
            pl.BlockSpec((1, hps, HD, HD), lambda b, h, t: (b, h, 0, 0)),
        ],
        out_shape=[jax.ShapeDtypeStruct((B * L, HEADS * HD), BF16),
                   jax.ShapeDtypeStruct((B, HEADS, HD, HD), F32)],
        scratch_shapes=[pltpu.VMEM((3, tb + 8, w), F32), pltpu.VMEM((hps, HD, HD), F32)],
        compiler_params=_cparams(("parallel", "parallel", "arbitrary")),
        name="gdn_prompt",
    )(proj, proj, proj, proj, ba, conv_w, conv_w, conv_w, alog_b, dtb_b, gdn_norm)


def _hgrn_prompt_kernel(hq_ref, hf_ref, hi_ref, hg_ref, lb_ref, hn_ref, o_ref, s_ref,
                        g_scr, q_scr, k_scr, st_scr, *, tb, chunk, group, hps):
    t = pl.program_id(2)

    @pl.when(t == 0)
    def _():
        st_scr[...] = jnp.zeros((hps, HD, HD), F32)

    lb = lb_ref[...]
    hf = hf_ref[...]
    forget = lb + (1.0 - lb) * _sigmoid(hf)
    k_scr[...] = (1.0 - lb) * _sigmoid(-hf)
    q_scr[...] = _silu(hq_ref[...])
    ri = lax.broadcasted_iota(I32, (LANES, LANES), 0)
    ci = lax.broadcasted_iota(I32, (LANES, LANES), 1)
    lblk = jnp.where(((ri // chunk) == (ci // chunk)) & (ci <= ri), 1.0, 0.0).astype(BF16)
    lf = jnp.log(forget)
    for r in range(tb // LANES):
        g_scr[r * LANES:(r + 1) * LANES, :] = _mm_exact_lhs(lblk, lf[r * LANES:(r + 1) * LANES, :])

    half = chunk // 2
    row8 = lax.broadcasted_iota(I32, (half, HD), 0)
    lane8 = lax.broadcasted_iota(I32, (half, HD), 1)
    zpad = jnp.zeros((half, HD), F32)
    hn = hn_ref[...]

    def body(gi, sts_in):
        base = gi * (group * chunk)
        items = [(hh, cc) for cc in range(group) for hh in range(hps)]
        ni = range(len(items))
        rc = [(base + cc * chunk, hh * HD) for hh, cc in items]
        vs = [hi_ref[pl.ds(r0, chunk), c0:c0 + HD] for r0, c0 in rc]
        vts = [v.T for v in vs]
        blocks = []
        for r0, c0 in rc:
            for r in (r0, r0 + half):
                blocks.append((g_scr[pl.ds(r, half), c0:c0 + HD], q_scr[pl.ds(r, half), c0:c0 + HD],
                               k_scr[pl.ds(r, half), c0:c0 + HD]))
        accs = [zpad] * len(blocks)
        for j in range(half):
            for n, (Gb, qb, kb) in enumerate(blocks):
                e = jnp.exp(Gb - Gb[j:j + 1, :])
                col = jnp.sum(e * qb * kb[j:j + 1, :], axis=-1, keepdims=True)
                accs[n] = jnp.where(lane8 == (n % 2) * half + j, col, accs[n])
        accs = [jnp.where(row8 >= lane8 - (n % 2) * half, a, 0.0) for n, a in enumerate(accs)]
        Gs = [jnp.concatenate([blocks[2 * n][0], blocks[2 * n + 1][0]], axis=0) for n in ni]
        qs = [jnp.concatenate([blocks[2 * n][1], blocks[2 * n + 1][1]], axis=0) for n in ni]
        ks = [jnp.concatenate([blocks[2 * n][2], blocks[2 * n + 1][2]], axis=0) for n in ni]
        offd = []
        for n in ni:
            Ga, _, ka = blocks[2 * n]
            Gb, qb, _ = blocks[2 * n + 1]
            gmid = Ga[half - 1:half, :]
            kh = jnp.concatenate([ka * jnp.exp(gmid - Ga), zpad], axis=0)
            offd.append(_mm_nt(qb * jnp.exp(Gb - gmid), kh))
        glasts = [Gs[n][chunk - 1:chunk, :] for n in ni]
        incs = [_mm(vts[n], ks[n] * jnp.exp(glasts[n] - Gs[n])) for n in ni]
        intra = []
        for n in ni:
            s = jnp.concatenate([accs[2 * n][:, 0:chunk], accs[2 * n + 1][:, 0:chunk] + offd[n]], axis=0)
            intra.append(_mm(s, vs[n]))
        cur = list(sts_in)
        before = []
        for n, (hh, _) in enumerate(items):
            before.append(cur[hh])
            cur[hh] = cur[hh] * jnp.exp(glasts[n]) + incs[n]
        outs = [_mm_nt(qs[n] * jnp.exp(Gs[n]), before[n]) + intra[n] for n in ni]
        for n, (r0, c0) in enumerate(rc):
            o_ref[pl.ds(r0, chunk), c0:c0 + HD] = (
                _rms(outs[n], hn) * _silu(hg_ref[pl.ds(r0, chunk), c0:c0 + HD])).astype(o_ref.dtype)
        return cur

    sts = [st_scr[hh] for hh in range(hps)]
    for gi in range(tb // (group * chunk)):
        sts = body(gi, sts)
    for hh in range(hps):
        st_scr[hh] = sts[hh]

    @pl.when(t == pl.num_programs(2) - 1)
    def _():
        for hh in range(hps):
            s_ref[0, hh] = sts[hh].T


def _hgrn_prompt(proj, lb, hg_norm, B, L):
    tb, chunk, hps = HG_TB, HG_CHUNK, HEADS_PER_STEP
    nt = L // tb
    w = hps * HD
    row = lambda b, h, t: b * nt + t
    colspec = lambda base: pl.BlockSpec((tb, w), lambda b, h, t: (row(b, h, t), base // hps + h))
    return pl.pallas_call(
        functools.partial(_hgrn_prompt_kernel, tb=tb, chunk=chunk, group=HG_GROUP, hps=hps),
        grid=(B, HEADS // hps, nt),
        in_specs=[
            colspec(HQB), colspec(HFB), colspec(HIB), colspec(HGB),
            pl.BlockSpec((1, w), lambda b, h, t: (0, h)),
            pl.BlockSpec((1, HD), lambda b, h, t: (0, 0)),
        ],
        out_specs=[
            pl.BlockSpec((tb, w), lambda b, h, t: (row(b, h, t), h)),
            pl.BlockSpec((1, hps, HD, HD), lambda b, h, t: (b, h, 0, 0)),
        ],
        out_shape=[jax.ShapeDtypeStruct((B * L, HEADS * HD), BF16),
                   jax.ShapeDtypeStruct((B, HEADS, HD, HD), F32)],
        scratch_shapes=[pltpu.VMEM((tb, w), F32), pltpu.VMEM((tb, w), F32), pltpu.VMEM((tb, w), F32),
                        pltpu.VMEM((hps, HD, HD), F32)],
        compiler_params=_cparams(("parallel", "parallel", "arbitrary")),
        name="hgrn_prompt",
    )(proj, proj, proj, proj, lb, hg_norm)


def _column_broadcasts(x, sel):
    d = lambda y: lax.dot_general(y, sel, (((0,), (0,)), ((), ())), preferred_element_type=F32)
    hi, lo = _split(x)
    return d(hi) + d(lo)


def _decode_kernel(qkv_ref, z_ref, hq_ref, hf_ref, hi_ref, hg_ref, ba_ref, cs_ref, sg_ref, sh_ref,
                   cw_ref, alog_ref, dtb_ref, lb_ref, gn_ref, hn_ref,
                   oa_ref, ob_ref, sgo_ref, sho_ref, *, nb):
    cd = cw_ref.shape[1]
    cw = cw_ref[...]
    cs = cs_ref[...]
    acc = cs[:, 0:cd] * cw[0:1, :]
    acc = acc + cs[:, cd:2 * cd] * cw[1:2, :]
    acc = acc + cs[:, 2 * cd:3 * cd] * cw[2:3, :]
    acc = acc + qkv_ref[...] * cw[3:4, :]
    conv = _silu(acc)
    nqk = HEADS * HD
    ba = ba_ref[...]
    gn = gn_ref[...]
    hn = hn_ref[...]
    z = z_ref[...]
    hgate = hg_ref[...]

    qs, ks, vs = [], [], []
    for h in range(HEADS):
        qs.append(_l2n(conv[:, h * HD:(h + 1) * HD]) * (HD ** -0.5))
        ks.append(_l2n(conv[:, nqk + h * HD:nqk + (h + 1) * HD]))
        vs.append(conv[:, 2 * nqk + h * HD:2 * nqk + (h + 1) * HD])
    lb = lb_ref[...]
    hf = hf_ref[...]
    forget = lb + (1.0 - lb) * _sigmoid(hf)
    hk = (1.0 - lb) * _sigmoid(-hf)
    hq = _silu(hq_ref[...])
    hv = hi_ref[...]

    sel_r = lax.broadcasted_iota(I32, (nb, nb * HD), 0)
    sel_c = lax.broadcasted_iota(I32, (nb, nb * HD), 1)
    sel = jnp.where(sel_c // HD == sel_r, 1.0, 0.0).astype(BF16)

    for h in range(HEADS):
        hs = slice(h * HD, (h + 1) * HD)
        beta = _sigmoid(ba[:, h:h + 1])
        g = -jnp.exp(alog_ref[0:1, h:h + 1]) * _softplus(ba[:, HEADS + h:HEADS + h + 1] + dtb_ref[0:1, h:h + 1])
        eg = jnp.exp(g)
        qk = jnp.sum(qs[h] * ks[h], axis=-1, keepdims=True)
        kB = _column_broadcasts(ks[h], sel)
        qB = _column_broadcasts(qs[h], sel)
        fB = _column_broadcasts(forget[:, hs], sel)
        hkB = _column_broadcasts(hk[:, hs], sel)
        hqB = _column_broadcasts(hq[:, hs], sel)
        for b in range(nb):
            bs = slice(b * HD, (b + 1) * HD)
            S = sg_ref[b, h]
            kS = jnp.sum(S * kB[:, bs], axis=0, keepdims=True)
            qS = jnp.sum(S * qB[:, bs], axis=0, keepdims=True)
            egb = eg[b:b + 1, :]
            v_new = beta[b:b + 1, :] * (vs[h][b:b + 1, :] - egb * kS)
            o = egb * qS + qk[b:b + 1, :] * v_new
            sgo_ref[b, h] = S * egb + kB[:, bs] * v_new
            oa_ref[b:b + 1, hs] = _rms(o, gn) * _silu(z[b:b + 1, hs])
            Sh = sh_ref[b, h] * fB[:, bs] + hkB[:, bs] * hv[b:b + 1, hs]
            sho_ref[b, h] = Sh
            ob = jnp.sum(Sh * hqB[:, bs], axis=0, keepdims=True)
            ob_ref[b:b + 1, hs] = _rms(ob, hn) * _silu(hgate[b:b + 1, hs])


def _decode(proj, ba, conv_state2d, s_gdn, s_hg, conv_w, alog_row, dtb_row, lb, gdn_norm, hg_norm):
    nb = DEC_TB
    T = proj.shape[0]
    cd = conv_w.shape[1]
    w = HEADS * HD
    cblk = lambda width, idx: pl.BlockSpec((nb, width), lambda i: (i, idx))
    sblk = pl.BlockSpec((nb, HEADS, HD, HD), lambda i: (i, 0, 0, 0))
    full = lambda shape: pl.BlockSpec(shape, lambda i: (0,) * len(shape))
    return pl.pallas_call(
        functools.partial(_decode_kernel, nb=nb),
        grid=(T // nb,),
        in_specs=[
            cblk(cd, 0), cblk(w, ZB // HEADS), cblk(w, HQB // HEADS), cblk(w, HFB // HEADS),
            cblk(w, HIB // HEADS), cblk(w, HGB // HEADS),
            cblk(LANES, 0), cblk(3 * cd, 0), sblk, sblk,
            full((CONV_K, cd)), full((1, LANES)), full((1, LANES)), full((1, w)), full((1, HD)), full((1, HD)),
        ],
        out_specs=[cblk(w, 0), cblk(w, 0), sblk, sblk],
        out_shape=[jax.ShapeDtypeStruct((T, w), F32), jax.ShapeDtypeStruct((T, w), F32),
                   jax.ShapeDtypeStruct(s_gdn.shape, F32), jax.ShapeDtypeStruct(s_hg.shape, F32)],
        compiler_params=_cparams(("parallel",)),
        name="decode",
    )(proj, proj, proj, proj, proj, proj, ba, conv_state2d, s_gdn, s_hg,
      conv_w, alog_row, dtb_row, lb, gdn_norm, hg_norm)


def _out_router_kernel(x_ref, oa_ref, ob_ref, woa_ref, wob_ref, n2_ref, wrh_ref, wrl_ref, br_ref,
                       x1_ref, h2_ref, ti_ref, tw_ref, *, nvalid):
    i = pl.program_id(0)

    @pl.when(i >= nvalid)
    def _():
        h2_ref[...] = jnp.zeros(h2_ref.shape, h2_ref.dtype)
        ti_ref[...] = jnp.zeros(ti_ref.shape, ti_ref.dtype)

    @pl.when(i < nvalid)
    def _():
        _out_router_tile(x_ref, oa_ref, ob_ref, woa_ref, wob_ref, n2_ref, wrh_ref, wrl_ref, br_ref,
                         x1_ref, h2_ref, ti_ref, tw_ref)


def _out_router_tile(x_ref, oa_ref, ob_ref, woa_ref, wob_ref, n2_ref, wrh_ref, wrl_ref, br_ref,
                     x1_ref, h2_ref, ti_ref, tw_ref):
    y = x_ref[...] + jnp.dot(oa_ref[...].astype(BF16), woa_ref[...], preferred_element_type=F32)
    y = y + jnp.dot(ob_ref[...].astype(BF16), wob_ref[...], preferred_element_type=F32)
    x1_ref[...] = y
    h2 = _rms(y, n2_ref[...])
    h2_ref[...] = h2
    logits = _mm3(_split(h2), (wrh_ref[...], wrl_ref[...])) + br_ref[...]
    lane = lax.broadcasted_iota(I32, logits.shape, 1)
    logits = jnp.where(lane < N_EXPERTS, logits, NEG)
    ti = jnp.zeros(logits.shape, I32)
    tw = jnp.zeros(logits.shape, F32)
    m0 = None
    for kk in range(TOP_K):
        m = jnp.max(logits, axis=-1, keepdims=True)
        idx = jnp.min(jnp.where(logits == m, lane, LANES), axis=-1, keepdims=True)
        if m0 is None:
            m0 = m
        ti = jnp.where(lane == kk, idx, ti)
        tw = jnp.where(lane == kk, jnp.exp(m - m0), tw)
        logits = jnp.where(lane == idx, NEG * 2.0, logits)
    tw_ref[...] = tw / jnp.sum(tw, axis=-1, keepdims=True)
    ti_ref[...] = ti


def _out_router_into_kernel(h2_all_ref, ti_all_ref, *refs, nvalid):
    del h2_all_ref, ti_all_ref
    _out_router_kernel(*refs, nvalid=nvalid)


def _out_router(x, oa, ob, wo_a, wo_b, norm2, wr_hi, wr_lo, b_router_p, tm, total_rows, row_off, into=None):
    T, D = x.shape
    w = oa.shape[1]
    off = row_off // tm
    nvalid = T // tm
    steps = nvalid if into is not None else -(-total_rows // tm)
    rowblk = lambda width: pl.BlockSpec((tm, width), lambda i: (jnp.minimum(i, nvalid - 1), 0))
    allblk = lambda width: pl.BlockSpec((tm, width), lambda i: (i + off, 0))
    full = lambda shape: pl.BlockSpec(shape, lambda i: (0,) * len(shape))
    in_specs = [rowblk(D), rowblk(w), rowblk(w), full((w, D)), full((w, D)), full((1, D)),
                full((D, LANES)), full((D, LANES)), full((1, LANES))]
    args = (x, oa, ob, wo_a, wo_b, norm2, wr_hi, wr_lo, b_router_p)
    body, aliases = _out_router_kernel, {}
    if into is not None:
        in_specs = [pl.BlockSpec(memory_space=pl.ANY)] * 2 + in_specs
        args = tuple(into) + args
        body, aliases = _out_router_into_kernel, {0: 1, 1: 2}
    return pl.pallas_call(
        functools.partial(body, nvalid=nvalid),
        grid=(steps,),
        in_specs=in_specs,
        out_specs=[rowblk(D), allblk(D), allblk(LANES), rowblk(LANES)],
        out_shape=[jax.ShapeDtypeStruct((T, D), F32), jax.ShapeDtypeStruct((total_rows, D), F32),
                   jax.ShapeDtypeStruct((total_rows, LANES), I32), jax.ShapeDtypeStruct((T, LANES), F32)],
        input_output_aliases=aliases,
        compiler_params=_cparams(("arbitrary",)),
        name="out_router",
    )(*args)


def _weight_ring_step(rt, w_hbm, wbuf, sem, tcols):
    te_ref, tv_ref, tf_ref, tnx_ref, tlast_ref, trun_ref, nr_ref = rt
    n = pl.program_id(0)
    m = pl.program_id(1)
    slot = lax.rem(n * nr_ref[0] + trun_ref[m], 2)

    def copy(e, nn, s):
        c0 = pl.multiple_of(nn * tcols, tcols)
        return pltpu.make_async_copy(w_hbm.at[e, :, pl.ds(c0, tcols)], wbuf.at[s], sem.at[s])

    @pl.when((n == 0) & (m == 0))
    def _():
        copy(te_ref[m], n, slot).start()

    copy(te_ref[m], n, slot).wait()
    last = tlast_ref[m] != 0

    @pl.when(jnp.logical_or(jnp.logical_not(last), n + 1 < pl.num_programs(0)))
    def _():
        copy(tnx_ref[m], jnp.where(last, n + 1, n), 1 - slot).start()

    return slot


def _moe_gate_up_kernel(te_ref, tv_ref, tf_ref, tnx_ref, tlast_ref, trun_ref, nr_ref,
                        x_ref, w_hbm, bg_ref, bu_ref, act_ref, wbuf, wc_scr, sem):
    rt = (te_ref, tv_ref, tf_ref, tnx_ref, tlast_ref, trun_ref, nr_ref)
    m = pl.program_id(1)
    tnw = wbuf.shape[2]
    grp = 2 * LANES
    ngrp = tnw // grp

    @pl.when(tv_ref[m] != 0)
    def _():
        @pl.when(tf_ref[m] != 0)
        def _():
            slot = _weight_ring_step(rt, w_hbm, wbuf, sem, tnw)
            src = lax.broadcasted_iota(I32, (grp, grp), 0)
            dst = lax.broadcasted_iota(I32, (grp, grp), 1)
            want = jnp.where(dst < LANES, 2 * dst, 2 * (dst - LANES) + 1)
            perm = jnp.where(src == want, 1.0, 0.0).astype(BF16)
            for g in range(ngrp):
                wt = wbuf[slot, :, g * grp:(g + 1) * grp].astype(BF16)
                wc_scr[:, g * grp:(g + 1) * grp] = jnp.dot(wt, perm, preferred_element_type=F32).astype(BF16)

        x = x_ref[...].astype(BF16)
        bg = bg_ref[0]
        bu = bu_ref[0]
        for g in range(ngrp):
            gu = jnp.dot(x, wc_scr[:, g * grp:(g + 1) * grp], preferred_element_type=F32)
            gate = jnp.minimum(gu[:, 0:LANES] + bg[:, g * LANES:(g + 1) * LANES], SWIGLU_LIMIT)
            up = jnp.clip(gu[:, LANES:grp] + bu[:, g * LANES:(g + 1) * LANES], -SWIGLU_LIMIT, SWIGLU_LIMIT)
            act_ref[:, g * LANES:(g + 1) * LANES] = (
                (up + 1.0) * (gate * _sigmoid_t(gate * SWIGLU_ALPHA))).astype(act_ref.dtype)

    @pl.when(tv_ref[m] == 0)
    def _():
        act_ref[...] = jnp.zeros(act_ref.shape, act_ref.dtype)


def _moe_gate_up(rt, xs, w_gate_up, bg, bu):
    tm, tnw = MOE_TM, MOE_TNW
    P, D = xs.shape
    F2 = w_gate_up.shape[2]
    imap_b = lambda n, m, te, *_: (te[m], 0, n)
    return pl.pallas_call(
        _moe_gate_up_kernel,
        grid_spec=pltpu.PrefetchScalarGridSpec(
            num_scalar_prefetch=len(rt),
            grid=(F2 // tnw, P // tm),
            in_specs=[
                pl.BlockSpec((tm, D), lambda n, m, *_: (m, 0)),
                pl.BlockSpec(memory_space=pl.ANY),
                pl.BlockSpec((1, 1, tnw // 2), imap_b),
                pl.BlockSpec((1, 1, tnw // 2), imap_b),
            ],
            out_specs=pl.BlockSpec((tm, tnw // 2), lambda n, m, *_: (m, n)),
            scratch_shapes=[pltpu.VMEM((2, D, tnw), F32), pltpu.VMEM((D, tnw), BF16),
                            pltpu.SemaphoreType.DMA((2,))],
        ),
        out_shape=jax.ShapeDtypeStruct((P, F2 // 2), BF16),
        compiler_params=_cparams(("arbitrary", "arbitrary")),
        name="moe_gate_up",
    )(*rt, xs, w_gate_up, bg, bu)


def _moe_down_kernel(te_ref, tv_ref, tf_ref, tnx_ref, tlast_ref, trun_ref, nr_ref,
                     a_ref, w_hbm, bd_ref, y_ref, wbuf, wc_scr, sem):
    rt = (te_ref, tv_ref, tf_ref, tnx_ref, tlast_ref, trun_ref, nr_ref)
    m = pl.program_id(1)

    @pl.when(tv_ref[m] != 0)
    def _():
        @pl.when(tf_ref[m] != 0)
        def _():
            slot = _weight_ring_step(rt, w_hbm, wbuf, sem, wbuf.shape[2])
            wc_scr[...] = wbuf[slot].astype(BF16)

        y_ref[...] = jnp.dot(a_ref[...], wc_scr[...], preferred_element_type=F32) + bd_ref[0]

    @pl.when(tv_ref[m] == 0)
    def _():
        y_ref[...] = jnp.zeros(y_ref.shape, y_ref.dtype)


def _moe_down(rt, act, w_down, bd):
    tm, tn = MOE_TM, MOE_TN
    P, F = act.shape
    D = w_down.shape[2]
    return pl.pallas_call(
        _moe_down_kernel,
        grid_spec=pltpu.PrefetchScalarGridSpec(
            num_scalar_prefetch=len(rt),
            grid=(D // tn, P // tm),
            in_specs=[
                pl.BlockSpec((tm, F), lambda n, m, *_: (m, 0)),
                pl.BlockSpec(memory_space=pl.ANY),
                pl.BlockSpec((1, 1, tn), lambda n, m, te, *_: (te[m], 0, n)),
            ],
            out_specs=pl.BlockSpec((tm, tn), lambda n, m, *_: (m, n)),
            scratch_shapes=[pltpu.VMEM((2, F, tn), F32), pltpu.VMEM((F, tn), BF16),
                            pltpu.SemaphoreType.DMA((2,))],
        ),
        out_shape=jax.ShapeDtypeStruct((P, D), F32),
        compiler_params=_cparams(("arbitrary", "arbitrary")),
        name="moe_down",
    )(*rt, act, w_down, bd)


def _combine_kernel(x1_ref, yg_ref, tw_ref, fn_ref, y_ref):
    tw = tw_ref[...]
    moe = tw[:, 0:1] * yg_ref[0]
    for kk in range(1, TOP_K):
        moe = moe + tw[:, kk:kk + 1] * yg_ref[kk]
    y_ref[...] = _rms(x1_ref[...] + moe, fn_ref[...])


def _combine(x1, yg, tw, final_norm, tm, row_off):
    T, D = x1.shape
    off = row_off // tm
    return pl.pallas_call(
        _combine_kernel,
        grid=(T // tm,),
        in_specs=[
            pl.BlockSpec((tm, D), lambda i: (i, 0)),
            pl.BlockSpec((TOP_K, tm, D), lambda i: (0, i + off, 0)),
            pl.BlockSpec((tm, LANES), lambda i: (i, 0)),
            pl.BlockSpec((1, D), lambda i: (0, 0)),
        ],
        out_specs=pl.BlockSpec((tm, D), lambda i: (i, 0)),
        out_shape=jax.ShapeDtypeStruct((T, D), F32),
        compiler_params=_cparams(("parallel",)),
        name="combine",
    )(x1, yg, tw, final_norm)


def _routing(top_i):
    T = top_i.shape[0]
    P = T * TOP_K
    tm = MOE_TM
    n_tiles = -(-P // tm) + N_EXPERTS
    e_flat = top_i.reshape(P)
    onehot = (e_flat[:, None] == jnp.arange(N_EXPERTS, dtype=I32)[None, :]).astype(I32)
    csum = jnp.cumsum(onehot, axis=0)
    rank = jnp.sum((csum - 1) * onehot, axis=1)
    counts = csum[-1]
    padded = ((counts + tm - 1) // tm) * tm
    ends = jnp.cumsum(padded)
    starts = ends - padded
    pos = starts[e_flat] + rank
    slot_token = jnp.zeros((n_tiles * tm,), I32).at[pos].set(jnp.arange(P, dtype=I32) // TOP_K)
    tile_start = jnp.arange(n_tiles, dtype=I32) * tm
    tile_valid = (tile_start < ends[-1]).astype(I32)
    tile_expert = jnp.minimum(jnp.sum((tile_start[:, None] >= ends[None, :]).astype(I32), axis=1), N_EXPERTS - 1)
    tile_expert = tile_expert.astype(I32)
    tile_first = jnp.concatenate([jnp.ones((1,), I32), (tile_expert[1:] != tile_expert[:-1]).astype(I32)])
    tile_first = tile_first * tile_valid
    tile_run = jnp.cumsum(tile_first) - 1
    n_runs = jnp.sum(tile_first).reshape(1)
    experts = jnp.arange(N_EXPERTS, dtype=I32)
    later = (counts > 0)[None, :] & (experts[None, :] > tile_expert[:, None])
    nxt = jnp.min(jnp.where(later, experts[None, :], N_EXPERTS), axis=1)
    tile_last = (nxt == N_EXPERTS).astype(I32)
    tile_next = jnp.where(nxt == N_EXPERTS, tile_expert[0], nxt).astype(I32)
    tables = (tile_expert, tile_valid, tile_first, tile_next, tile_last, tile_run.astype(I32), n_runs.astype(I32))
    return pos, slot_token, tables


def kernel(x_prompt, x_sample, state_gdn, state_conv, state_hgrn, lb_table, norm1, w_in, conv_w, A_log, dt_bias,
           gdn_norm, hg_norm, w_out, norm2, w_router, b_router, w_gate_up, b_gate_up, w_down, b_down, final_norm):
    depth = w_in.shape[0]
    assert depth == 1
    B, L, D = x_prompt.shape
    SB = x_sample.shape[0]
    assert x_sample.shape[1] == 1
    nqk = HEADS * HD
    cd = 3 * nqk

    lbs = jnp.cumsum(jax.nn.softmax(lb_table.astype(F32), axis=0), axis=0)
    lb = lbs[0:1]

    assert w_in.shape[2] == MAIN_COLS + 2 * HEADS and cd + nqk == MAIN_COLS // 2
    w_main, w_ba = _repack_w_in(w_in[0], 2 * HEADS)
    wo_a = w_out[0, :nqk].astype(BF16)
    wo_b = w_out[0, nqk:].astype(BF16)
    wr_hi, wr_lo = _split(jnp.pad(w_router[0], ((0, 0), (0, LANES - N_EXPERTS))))
    br_p = jnp.pad(b_router[0], (0, LANES - N_EXPERTS))[None, :]
    bg = b_gate_up[0, :, None, 0::2]
    bu = b_gate_up[0, :, None, 1::2]
    bd = b_down[0][:, None, :]
    alog_b = jnp.pad(A_log[0], (HEADS, LANES - 2 * HEADS))[None, :]
    dtb_b = jnp.pad(dt_bias[0], (HEADS, LANES - 2 * HEADS))[None, :]
    alog_row = jnp.pad(A_log[0], (0, LANES - HEADS))[None, :]
    dtb_row = jnp.pad(dt_bias[0], (0, LANES - HEADS))[None, :]
    n1 = norm1[0][None, :]
    n2 = norm2[0][None, :]
    gn = gdn_norm[0][None, :]
    hn = hg_norm[0][None, :]
    fnw = final_norm[None, :]
    cw = conv_w[0]

    xp = x_prompt.reshape(B * L, D)
    xs = x_sample.reshape(SB, D)

    proj_p, ba_p = _in_proj(xp, n1, w_main, w_ba, tm=1024, tn=1024)
    proj_s, ba_s = _in_proj(xs, n1, w_main, w_ba, tm=SB, tn=1024)

    oa_p, sg_p = _gdn_prompt(proj_p, ba_p, cw, alog_b, dtb_b, gn, B, L)
    ob_p, sh_p = _hgrn_prompt(proj_p, lb, hn, B, L)
    oa_s, ob_s, sg_s, sh_s = _decode(proj_s, ba_s, state_conv[0].reshape(SB, (CONV_K - 1) * cd), state_gdn[0],
                                     state_hgrn[0], cw, alog_row, dtb_row, lb, gn, hn)

    ntok = B * L + SB
    x1_p, h2, ti, tw_p = _out_router(xp, oa_p, ob_p, wo_a, wo_b, n2, wr_hi, wr_lo, br_p,
                                     tm=256, total_rows=ntok, row_off=0)
    x1_s, h2, ti, tw_s = _out_router(xs, oa_s, ob_s, wo_a, wo_b, n2, wr_hi, wr_lo, br_p,
                                     tm=SB, total_rows=ntok, row_off=B * L, into=(h2, ti))

    top_i = ti[:, :TOP_K]
    pos, slot_token, tables = _routing(top_i)
    xs_sorted = h2.at[slot_token].get(mode="promise_in_bounds")
    act = _moe_gate_up(tables, xs_sorted, w_gate_up[0], bg, bu)
    yslots = _moe_down(tables, act, w_down[0], bd)
    pos_kmajor = pos.reshape(B * L + SB, TOP_K).T.reshape(-1)
    yg = yslots.at[pos_kmajor].get(mode="promise_in_bounds").reshape(TOP_K, B * L + SB, D)

    y_p = _combine(x1_p, yg, tw_p, fnw, tm=256, row_off=0)
    y_s = _combine(x1_s, yg, tw_s, fnw, tm=SB, row_off=B * L)

    conv_p = proj_p.reshape(B, L, MAIN_COLS)[:, L - (CONV_K - 1):, :cd]
    conv_s = jnp.concatenate([state_conv[0][:, 1:, :], proj_s[:, None, :cd]], axis=1)
    return (y_p.reshape(B, L, D), y_s.reshape(SB, 1, D),
            sg_p[None], conv_p[None].astype(state_conv.dtype), sh_p[None],
            sg_s[None], conv_s[None].astype(state_conv.dtype), sh_s[None])
```

```python
import functools

import jax
import jax.numpy as jnp
from jax import lax
from jax.experimental import pallas as pl
from jax.experimental.pallas import tpu as pltpu

F32 = jnp.float32
BF16 = jnp.bfloat16
I32 = jnp.int32

EPS = 1e-6
HEADS = 8
HD = 128
CONV_K = 4
N_EXPERTS = 32
TOP_K = 4
SWIGLU_ALPHA = 1.702
SWIGLU_LIMIT = 7.0
LANES = 128
NEG = -1e30

VMEM_LIMIT = 56 * 1024 * 1024

QB, KB, VB, ZB, HQB, HFB, HIB, HGB = (i * HEADS for i in range(8))
MAIN_COLS = 8 * HEADS * HD

HEADS_PER_STEP = 2
GDN_CHUNK = 128
GDN_TB = 512
HG_CHUNK = 16
HG_TB = 512
HG_GROUP = 4
DEC_TB = 8
MOE_TM = 256
MOE_TN = 2048
MOE_TNW = 2048


def _cparams(sem):
    return pltpu.CompilerParams(dimension_semantics=sem, vmem_limit_bytes=VMEM_LIMIT)


def _mm(a, b):
    return jnp.dot(a.astype(BF16), b.astype(BF16), preferred_element_type=F32)


def _mm_nt(a, b):
    return lax.dot_general(a.astype(BF16), b.astype(BF16), (((1,), (1,)), ((), ())),
                           preferred_element_type=F32)


def _mm_tn(a, b):
    return lax.dot_general(a.astype(BF16), b.astype(BF16), (((0,), (0,)), ((), ())),
                           preferred_element_type=F32)


def _mmh(a, b):
    return jnp.dot(a, b, precision=lax.Precision.HIGHEST, preferred_element_type=F32)


def _split(a):
    hi = a.astype(BF16)
    return hi, (a - hi.astype(F32)).astype(BF16)


def _mm3(a, b):
    d = lambda x, y: jnp.dot(x, y, preferred_element_type=F32)
    return d(a[0], b[0]) + (d(a[0], b[1]) + d(a[1], b[0]))


def _mm_exact_lhs(l_bf16, x):
    d = lambda y: jnp.dot(l_bf16, y, preferred_element_type=F32)
    x0 = x.astype(BF16)
    r1 = x - x0.astype(F32)
    x1 = r1.astype(BF16)
    x2 = (r1 - x1.astype(F32)).astype(BF16)
    return d(x0) + (d(x1) + d(x2))


def _sigmoid(x):
    return 1.0 / (1.0 + jnp.exp(-x))


def _sigmoid_t(x):
    return 0.5 * jnp.tanh(0.5 * x) + 0.5


def _silu(x):
    return x * _sigmoid_t(x)


def _softplus(x):
    return jnp.maximum(x, 0.0) + jnp.log1p(jnp.exp(-jnp.abs(x)))


def _rms(x, w):
    return x * lax.rsqrt(jnp.mean(x * x, axis=-1, keepdims=True) + EPS) * w


def _l2n(x):
    return x * lax.rsqrt(jnp.sum(x * x, axis=-1, keepdims=True) + EPS)


def _in_proj_kernel(x_ref, nw_ref, wt_ref, wbat_ref, o_ref, ba_ref, h_scr):
    nt_dims = (((1,), (1,)), ((), ()))

    @pl.when(pl.program_id(1) == 0)
    def _():
        h = _rms(x_ref[...], nw_ref[...]).astype(BF16)
        h_scr[...] = h
        ba_ref[...] = lax.dot_general(h, wbat_ref[...], nt_dims, preferred_element_type=F32)

    o_ref[...] = lax.dot_general(h_scr[...], wt_ref[...], nt_dims, preferred_element_type=F32)


def _in_proj(x, norm_w, w_t, ba0, nba, tm, tn):
    T, D = x.shape
    N = w_t.shape[0] - nba
    assert ba0 % tn == 0 and nba % 16 == 0
    return pl.pallas_call(
        _in_proj_kernel,
        grid=(T // tm, N // tn),
        in_specs=[
            pl.BlockSpec((tm, D), lambda i, j: (i, 0)),
            pl.BlockSpec((1, D), lambda i, j: (0, 0)),
            pl.BlockSpec((pl.Element(tn), pl.Element(D)),
                         lambda i, j: (pl.multiple_of(j * tn + jnp.where(j * tn >= ba0, nba, 0), 16), 0)),
            pl.BlockSpec((pl.Element(LANES), pl.Element(D)), lambda i, j: (ba0, 0)),
        ],
        out_specs=[
            pl.BlockSpec((tm, tn), lambda i, j: (i, j)),
            pl.BlockSpec((tm, LANES), lambda i, j: (i, 0)),
        ],
        out_shape=[jax.ShapeDtypeStruct((T, N), F32), jax.ShapeDtypeStruct((T, LANES), F32)],
        scratch_shapes=[pltpu.VMEM((tm, D), BF16)],
        compiler_params=_cparams(("parallel", "arbitrary")),
        name="in_proj",
    )(x, norm_w, w_t, w_t)


def _unit_lower_inverse(As, ri, ci):
    n = As[0].shape[0]
    eye = (ri == ci).astype(F32)
    same16 = (ri // 16) == (ci // 16)
    s1 = [_split(jnp.where(same16, -A, 0.0)) for A in As]
    s2 = [_split(_mm3(s, s)) for s in s1]
    s4 = [_split(_mm3(s, s)) for s in s2]
    s8 = [_split(_mm3(s, s)) for s in s4]
    Ts = [eye + jnp.where(same16, -A, 0.0) for A in As]
    for sp in (s2, s4, s8):
        Ts = [T + _mm3(_split(T), s) for T, s in zip(Ts, sp)]
    size = 32
    while size <= n:
        off = ((ri // size) == (ci // size)) & ((ri // (size // 2)) != (ci // (size // 2)))
        bT = [T.astype(BF16) for T in Ts]
        TL = [_mm(bt, jnp.where(off, A, 0.0)) for bt, A in zip(bT, As)]
        Ts = [T - _mm(tl, bt) for T, tl, bt in zip(Ts, TL, bT)]
        size *= 2
    return Ts


def _gdn_prompt_kernel(q_ref, k_ref, v_ref, z_ref, ba_ref, cwq_ref, cwk_ref, cwv_ref, alog_ref, dtb_ref,
                       gn_ref, o_ref, s_ref, ubuf, s_scr, *, tb, chunk, hps):
    hg = pl.program_id(1)
    t = pl.program_id(2)

    @pl.when(t == 0)
    def _():
        ubuf[:, 0:8, :] = jnp.zeros((3, 8, hps * HD), F32)
        s_scr[...] = jnp.zeros((hps, HD, HD), F32)

    ubuf[0, 8:8 + tb, :] = q_ref[...]
    ubuf[1, 8:8 + tb, :] = k_ref[...]
    ubuf[2, 8:8 + tb, :] = v_ref[...]

    ba = ba_ref[...]
    lane = lax.broadcasted_iota(I32, (chunk, LANES), 1)
    beta_all = _sigmoid_t(ba)
    g_all = -jnp.exp(alog_ref[...]) * _softplus(ba + dtb_ref[...])

    ri = lax.broadcasted_iota(I32, (chunk, chunk), 0)
    ci = lax.broadcasted_iota(I32, (chunk, chunk), 1)
    causal = ci <= ri
    strict = ci < ri
    ltri = jnp.where(causal, 1.0, 0.0).astype(BF16)

    def conv(idx, w_ref, r0, c0):
        w = w_ref[:, c0:c0 + HD]
        acc = ubuf[idx, r0 + 5:r0 + 5 + chunk, c0:c0 + HD] * w[0:1, :]
        for j in range(1, CONV_K):
            acc = acc + ubuf[idx, r0 + 5 + j:r0 + 5 + j + chunk, c0:c0 + HD] * w[j:j + 1, :]
        return _silu(acc)

    nc = tb // chunk
    items = [(hh, c) for c in range(nc) for hh in range(hps)]
    qs, ks, vs, betas, gcols, decays, As = [], [], [], [], [], [], []
    gc_all = [_mm_exact_lhs(ltri, g_all[c * chunk:(c + 1) * chunk, :]) for c in range(nc)]
    for hh, c in items:
        r0, c0 = c * chunk, hh * HD
        h = hg * hps + hh
        q = _l2n(conv(0, cwq_ref, r0, c0)) * (HD ** -0.5)
        k = _l2n(conv(1, cwk_ref, r0, c0))
        v = conv(2, cwv_ref, r0, c0)
        beta = jnp.sum(jnp.where(lane == h, beta_all[r0:r0 + chunk, :], 0.0), axis=-1, keepdims=True)
        gcol = jnp.broadcast_to(
            jnp.sum(jnp.where(lane == h + HEADS, gc_all[c], 0.0), axis=-1, keepdims=True), (chunk, HD))
        decay = jnp.exp(jnp.where(causal, gcol - gcol.T, NEG))
        kb = k * beta
        qs.append(q); ks.append(k); vs.append(v); betas.append(beta); gcols.append(gcol); decays.append(decay)
        As.append(jnp.where(strict, _mm_nt(kb, k) * decay, 0.0))
    Ts = _unit_lower_inverse(As, ri, ci)
    us, ws, scs = [], [], []
    for n in range(len(items)):
        sT = _split(Ts[n])
        kb = ks[n] * betas[n]
        us.append(_mm3(sT, _split(vs[n] * betas[n])))
        ws.append(_mm3(sT, _split(kb * jnp.exp(gcols[n]))))
        scs.append(_mm_nt(qs[n], ks[n]) * decays[n])

    S = [s_scr[hh] for hh in range(hps)]
    for n, (hh, c) in enumerate(items):
        r0, c0 = c * chunk, hh * HD
        gcol = gcols[n]
        v_new = us[n] - _mm(ws[n], S[hh])
        o = _mm(qs[n] * jnp.exp(gcol), S[hh]) + _mm(scs[n], v_new)
        glast = gcol[chunk - 1:chunk, :]
        S[hh] = S[hh] * jnp.exp(glast) + _mm_tn(ks[n] * jnp.exp(glast - gcol), v_new)
        o_ref[r0:r0 + chunk, c0:c0 + HD] = (
            _rms(o, gn_ref[...]) * _silu(z_ref[r0:r0 + chunk, c0:c0 + HD])).astype(o_ref.dtype)
    for hh in range(hps):
        s_scr[hh] = S[hh]

    ubuf[:, 0:8, :] = ubuf[:, tb:tb + 8, :]

    @pl.when(t == pl.num_programs(2) - 1)
    def _():
        for hh in range(hps):
            s_ref[0, hh] = S[hh]


def _gdn_prompt(proj, ba, conv_w, alog_b, dtb_b, gdn_norm, B, L):
    tb, chunk, hps = GDN_TB, GDN_CHUNK, HEADS_PER_STEP
    nt = L // tb
    w = hps * HD
    row = lambda b, h, t: b * nt + t
    colspec = lambda base: pl.BlockSpec((tb, w), lambda b, h, t: (row(b, h, t), base // hps + h))
    cwspec = lambda base: pl.BlockSpec((CONV_K, w), lambda b, h, t: (0, base // hps + h))
    hvec = pl.BlockSpec((1, LANES), lambda b, h, t: (0, 0))
    return pl.pallas_call(
        functools.partial(_gdn_prompt_kernel, tb=tb, chunk=chunk, hps=hps),
        grid=(B, HEADS // hps, nt),
        in_specs=[
            colspec(QB), colspec(KB), colspec(VB), colspec(ZB),
            pl.BlockSpec((tb, LANES), lambda b, h, t: (row(b, h, t), 0)),
            cwspec(QB), cwspec(KB), cwspec(VB),
            hvec, hvec,
            pl.BlockSpec((1, HD), lambda b, h, t: (0, 0)),
        ],
        out_specs=[
            pl.BlockSpec((tb, w), lambda b, h, t: (row(b, h, t), h)),
            pl.BlockSpec((1, hps, HD, HD), lambda b, h, t: (b, h, 0, 0)),
        ],
        out_shape=[jax.ShapeDtypeStruct((B * L, HEADS * HD), BF16),
                   jax.ShapeDtypeStruct((B, HEADS, HD, HD), F32)],
        scratch_shapes=[pltpu.VMEM((3, tb + 8, w), F32), pltpu.VMEM((hps, HD, HD), F32)],
        compiler_params=_cparams(("parallel", "parallel", "arbitrary")),
        name="gdn_prompt",
    )(proj, proj, proj, proj, ba, conv_w, conv_w, conv_w, alog_b, dtb_b, gdn_norm)


def _hgrn_prompt_kernel(hq_ref, hf_ref, hi_ref, hg_ref, lb_ref, hn_ref, o_ref, s_ref,
                        g_scr, q_scr, k_scr, st_scr, *, tb, chunk, group, hps):
    t = pl.program_id(2)

    @pl.when(t == 0)
    def _():
        st_scr[...] = jnp.zeros((hps, HD, HD), F32)

    lb = lb_ref[...]
    hf = hf_ref[...]
    forget = lb + (1.0 - lb) * _sigmoid(hf)
    k_scr[...] = (1.0 - lb) * _sigmoid(-hf)
    q_scr[...] = _silu(hq_ref[...])
    ri = lax.broadcasted_iota(I32, (LANES, LANES), 0)
    ci = lax.broadcasted_iota(I32, (LANES, LANES), 1)
    lblk = jnp.where(((ri // chunk) == (ci // chunk)) & (ci <= ri), 1.0, 0.0).astype(BF16)
    lf = jnp.log(forget)
    for r in range(tb // LANES):
        g_scr[r * LANES:(r + 1) * LANES, :] = _mm_exact_lhs(lblk, lf[r * LANES:(r + 1) * LANES, :])

    half = chunk // 2
    row8 = lax.broadcasted_iota(I32, (half, HD), 0)
    lane8 = lax.broadcasted_iota(I32, (half, HD), 1)
    zpad = jnp.zeros((half, HD), F32)
    hn = hn_ref[...]

    def body(gi, sts_in):
        base = gi * (group * chunk)
        items = [(hh, cc) for cc in range(group) for hh in range(hps)]
        ni = range(len(items))
        rc = [(base + cc * chunk, hh * HD) for hh, cc in items]
        vs = [hi_ref[pl.ds(r0, chunk), c0:c0 + HD] for r0, c0 in rc]
        vts = [v.T for v in vs]
        blocks = []
        for r0, c0 in rc:
            for r in (r0, r0 + half):
                blocks.append((g_scr[pl.ds(r, half), c0:c0 + HD], q_scr[pl.ds(r, half), c0:c0 + HD],
                               k_scr[pl.ds(r, half), c0:c0 + HD]))
        accs = [zpad] * len(blocks)
        for j in range(half):
            for n, (Gb, qb, kb) in enumerate(blocks):
                e = jnp.exp(Gb - Gb[j:j + 1, :])
                col = jnp.sum(e * qb * kb[j:j + 1, :], axis=-1, keepdims=True)
                accs[n] = jnp.where(lane8 == (n % 2) * half + j, col, accs[n])
        accs = [jnp.where(row8 >= lane8 - (n % 2) * half, a, 0.0) for n, a in enumerate(accs)]
        Gs = [jnp.concatenate([blocks[2 * n][0], blocks[2 * n + 1][0]], axis=0) for n in ni]
        qs = [jnp.concatenate([blocks[2 * n][1], blocks[2 * n + 1][1]], axis=0) for n in ni]
        ks = [jnp.concatenate([blocks[2 * n][2], blocks[2 * n + 1][2]], axis=0) for n in ni]
        offd = []
        for n in ni:
            Ga, _, ka = blocks[2 * n]
            Gb, qb, _ = blocks[2 * n + 1]
            gmid = Ga[half - 1:half, :]
            kh = jnp.concatenate([ka * jnp.exp(gmid - Ga), zpad], axis=0)
            offd.append(_mm_nt(qb * jnp.exp(Gb - gmid), kh))
        glasts = [Gs[n][chunk - 1:chunk, :] for n in ni]
        incs = [_mm(vts[n], ks[n] * jnp.exp(glasts[n] - Gs[n])) for n in ni]
        intra = []
        for n in ni:
            s = jnp.concatenate([accs[2 * n][:, 0:chunk], accs[2 * n + 1][:, 0:chunk] + offd[n]], axis=0)
            intra.append(_mm(s, vs[n]))
        cur = list(sts_in)
        before = []
        for n, (hh, _) in enumerate(items):
            before.append(cur[hh])
            cur[hh] = cur[hh] * jnp.exp(glasts[n]) + incs[n]
        outs = [_mm_nt(qs[n] * jnp.exp(Gs[n]), before[n]) + intra[n] for n in ni]
        for n, (r0, c0) in enumerate(rc):
            o_ref[pl.ds(r0, chunk), c0:c0 + HD] = (
                _rms(outs[n], hn) * _silu(hg_ref[pl.ds(r0, chunk), c0:c0 + HD])).astype(o_ref.dtype)
        return cur

    sts = [st_scr[hh] for hh in range(hps)]
    for gi in range(tb // (group * chunk)):
        sts = body(gi, sts)
    for hh in range(hps):
        st_scr[hh] = sts[hh]

    @pl.when(t == pl.num_programs(2) - 1)
    def _():
        for hh in range(hps):
            s_ref[0, hh] = sts[hh].T


def _hgrn_prompt(proj, lb, hg_norm, B, L):
    tb, chunk, hps = HG_TB, HG_CHUNK, HEADS_PER_STEP
    nt = L // tb
    w = hps * HD
    row = lambda b, h, t: b * nt + t
    colspec = lambda base: pl.BlockSpec((tb, w), lambda b, h, t: (row(b, h, t), base // hps + h))
    return pl.pallas_call(
        functools.partial(_hgrn_prompt_kernel, tb=tb, chunk=chunk, group=HG_GROUP, hps=hps),
        grid=(B, HEADS // hps, nt),
        in_specs=[
            colspec(HQB), colspec(HFB), colspec(HIB), colspec(HGB),
            pl.BlockSpec((1, w), lambda b, h, t: (0, h)),
            pl.BlockSpec((1, HD), lambda b, h, t: (0, 0)),
        ],
        out_specs=[
            pl.BlockSpec((tb, w), lambda b, h, t: (row(b, h, t), h)),
            pl.BlockSpec((1, hps, HD, HD), lambda b, h, t: (b, h, 0, 0)),
        ],
        out_shape=[jax.ShapeDtypeStruct((B * L, HEADS * HD), BF16),
                   jax.ShapeDtypeStruct((B, HEADS, HD, HD), F32)],
        scratch_shapes=[pltpu.VMEM((tb, w), F32), pltpu.VMEM((tb, w), F32), pltpu.VMEM((tb, w), F32),
                        pltpu.VMEM((hps, HD, HD), F32)],
        compiler_params=_cparams(("parallel", "parallel", "arbitrary")),
        name="hgrn_prompt",
    )(proj, proj, proj, proj, lb, hg_norm)


def _column_broadcasts(x, sel):
    d = lambda y: lax.dot_general(y, sel, (((0,), (0,)), ((), ())), preferred_element_type=F32)
    hi, lo = _split(x)
    return d(hi) + d(lo)


def _decode_kernel(qkv_ref, z_ref, hq_ref, hf_ref, hi_ref, hg_ref, ba_ref, cs_ref, sg_ref, sh_ref,
                   cw_ref, alog_ref, dtb_ref, lb_ref, gn_ref, hn_ref,
                   oa_ref, ob_ref, sgo_ref, sho_ref, *, nb):
    cd = cw_ref.shape[1]
    cw = cw_ref[...]
    cs = cs_ref[...]
    acc = cs[:, 0:cd] * cw[0:1, :]
    acc = acc + cs[:, cd:2 * cd] * cw[1:2, :]
    acc = acc + cs[:, 2 * cd:3 * cd] * cw[2:3, :]
    acc = acc + qkv_ref[...] * cw[3:4, :]
    conv = _silu(acc)
    nqk = HEADS * HD
    ba = ba_ref[...]
    gn = gn_ref[...]
    hn = hn_ref[...]
    z = z_ref[...]
    hgate = hg_ref[...]

    qs, ks, vs = [], [], []
    for h in range(HEADS):
        qs.append(_l2n(conv[:, h * HD:(h + 1) * HD]) * (HD ** -0.5))
        ks.append(_l2n(conv[:, nqk + h * HD:nqk + (h + 1) * HD]))
        vs.append(conv[:, 2 * nqk + h * HD:2 * nqk + (h + 1) * HD])
    lb = lb_ref[...]
    hf = hf_ref[...]
    forget = lb + (1.0 - lb) * _sigmoid(hf)
    hk = (1.0 - lb) * _sigmoid(-hf)
    hq = _silu(hq_ref[...])
    hv = hi_ref[...]

    sel_r = lax.broadcasted_iota(I32, (nb, nb * HD), 0)
    sel_c = lax.broadcasted_iota(I32, (nb, nb * HD), 1)
    sel = jnp.where(sel_c // HD == sel_r, 1.0, 0.0).astype(BF16)

    for h in range(HEADS):
        hs = slice(h * HD, (h + 1) * HD)
        beta = _sigmoid(ba[:, h:h + 1])
        g = -jnp.exp(alog_ref[0:1, h:h + 1]) * _softplus(ba[:, HEADS + h:HEADS + h + 1] + dtb_ref[0:1, h:h + 1])
        eg = jnp.exp(g)
        qk = jnp.sum(qs[h] * ks[h], axis=-1, keepdims=True)
        kB = _column_broadcasts(ks[h], sel)
        qB = _column_broadcasts(qs[h], sel)
        fB = _column_broadcasts(forget[:, hs], sel)
        hkB = _column_broadcasts(hk[:, hs], sel)
        hqB = _column_broadcasts(hq[:, hs], sel)
        for b in range(nb):
            bs = slice(b * HD, (b + 1) * HD)
            S = sg_ref[b, h]
            kS = jnp.sum(S * kB[:, bs], axis=0, keepdims=True)
            qS = jnp.sum(S * qB[:, bs], axis=0, keepdims=True)
            egb = eg[b:b + 1, :]
            v_new = beta[b:b + 1, :] * (vs[h][b:b + 1, :] - egb * kS)
            o = egb * qS + qk[b:b + 1, :] * v_new
            sgo_ref[b, h] = S * egb + kB[:, bs] * v_new
            oa_ref[b:b + 1, hs] = _rms(o, gn) * _silu(z[b:b + 1, hs])
            Sh = sh_ref[b, h] * fB[:, bs] + hkB[:, bs] * hv[b:b + 1, hs]
            sho_ref[b, h] = Sh
            ob = jnp.sum(Sh * hqB[:, bs], axis=0, keepdims=True)
            ob_ref[b:b + 1, hs] = _rms(ob, hn) * _silu(hgate[b:b + 1, hs])


def _decode(proj, ba, conv_state2d, s_gdn, s_hg, conv_w, alog_row, dtb_row, lb, gdn_norm, hg_norm):
    nb = DEC_TB
    T = proj.shape[0]
    cd = conv_w.shape[1]
    w = HEADS * HD
    cblk = lambda width, idx: pl.BlockSpec((nb, width), lambda i: (i, idx))
    sblk = pl.BlockSpec((nb, HEADS, HD, HD), lambda i: (i, 0, 0, 0))
    full = lambda shape: pl.BlockSpec(shape, lambda i: (0,) * len(shape))
    return pl.pallas_call(
        functools.partial(_decode_kernel, nb=nb),
        grid=(T // nb,),
        in_specs=[
            cblk(cd, 0), cblk(w, ZB // HEADS), cblk(w, HQB // HEADS), cblk(w, HFB // HEADS),
            cblk(w, HIB // HEADS), cblk(w, HGB // HEADS),
            cblk(LANES, 0), cblk(3 * cd, 0), sblk, sblk,
            full((CONV_K, cd)), full((1, LANES)), full((1, LANES)), full((1, w)), full((1, HD)), full((1, HD)),
        ],
        out_specs=[cblk(w, 0), cblk(w, 0), sblk, sblk],
        out_shape=[jax.ShapeDtypeStruct((T, w), F32), jax.ShapeDtypeStruct((T, w), F32),
                   jax.ShapeDtypeStruct(s_gdn.shape, F32), jax.ShapeDtypeStruct(s_hg.shape, F32)],
        compiler_params=_cparams(("parallel",)),
        name="decode",
    )(proj, proj, proj, proj, proj, proj, ba, conv_state2d, s_gdn, s_hg,
      conv_w, alog_row, dtb_row, lb, gdn_norm, hg_norm)


def _out_router_kernel(x_ref, oa_ref, ob_ref, woa_ref, wob_ref, n2_ref, wrh_ref, wrl_ref, br_ref,
                       x1_ref, h2_ref, ti_ref, tw_ref, *, nvalid):
    i = pl.program_id(0)

    @pl.when(i >= nvalid)
    def _():
        h2_ref[...] = jnp.zeros(h2_ref.shape, h2_ref.dtype)
        ti_ref[...] = jnp.zeros(ti_ref.shape, ti_ref.dtype)

    @pl.when(i < nvalid)
    def _():
        _out_router_tile(x_ref, oa_ref, ob_ref, woa_ref, wob_ref, n2_ref, wrh_ref, wrl_ref, br_ref,
                         x1_ref, h2_ref, ti_ref, tw_ref)


def _out_router_tile(x_ref, oa_ref, ob_ref, woa_ref, wob_ref, n2_ref, wrh_ref, wrl_ref, br_ref,
                     x1_ref, h2_ref, ti_ref, tw_ref):
    y = x_ref[...] + jnp.dot(oa_ref[...].astype(BF16), woa_ref[...], preferred_element_type=F32)
    y = y + jnp.dot(ob_ref[...].astype(BF16), wob_ref[...], preferred_element_type=F32)
    x1_ref[...] = y
    h2 = _rms(y, n2_ref[...])
    h2_ref[...] = h2
    logits = _mm3(_split(h2), (wrh_ref[...], wrl_ref[...])) + br_ref[...]
    lane = lax.broadcasted_iota(I32, logits.shape, 1)
    logits = jnp.where(lane < N_EXPERTS, logits, NEG)
    ti = jnp.zeros(logits.shape, I32)
    tw = jnp.zeros(logits.shape, F32)
    m0 = None
    for kk in range(TOP_K):
        m = jnp.max(logits, axis=-1, keepdims=True)
        idx = jnp.min(jnp.where(logits == m, lane, LANES), axis=-1, keepdims=True)
        if m0 is None:
            m0 = m
        ti = jnp.where(lane == kk, idx, ti)
        tw = jnp.where(lane == kk, jnp.exp(m - m0), tw)
        logits = jnp.where(lane == idx, NEG * 2.0, logits)
    tw_ref[...] = tw / jnp.sum(tw, axis=-1, keepdims=True)
    ti_ref[...] = ti


def _out_router_into_kernel(h2_all_ref, ti_all_ref, *refs, nvalid):
    del h2_all_ref, ti_all_ref
    _out_router_kernel(*refs, nvalid=nvalid)


def _out_router(x, oa, ob, wo_a, wo_b, norm2, wr_hi, wr_lo, b_router_p, tm, total_rows, row_off, into=None):
    T, D = x.shape
    w = oa.shape[1]
    off = row_off // tm
    nvalid = T // tm
    steps = nvalid if into is not None else -(-total_rows // tm)
    rowblk = lambda width: pl.BlockSpec((tm, width), lambda i: (jnp.minimum(i, nvalid - 1), 0))
    allblk = lambda width: pl.BlockSpec((tm, width), lambda i: (i + off, 0))
    full = lambda shape: pl.BlockSpec(shape, lambda i: (0,) * len(shape))
    in_specs = [rowblk(D), rowblk(w), rowblk(w), full((w, D)), full((w, D)), full((1, D)),
                full((D, LANES)), full((D, LANES)), full((1, LANES))]
    args = (x, oa, ob, wo_a, wo_b, norm2, wr_hi, wr_lo, b_router_p)
    body, aliases = _out_router_kernel, {}
    if into is not None:
        in_specs = [pl.BlockSpec(memory_space=pl.ANY)] * 2 + in_specs
        args = tuple(into) + args
        body, aliases = _out_router_into_kernel, {0: 1, 1: 2}
    return pl.pallas_call(
        functools.partial(body, nvalid=nvalid),
        grid=(steps,),
        in_specs=in_specs,
        out_specs=[rowblk(D), allblk(D), allblk(LANES), rowblk(LANES)],
        out_shape=[jax.ShapeDtypeStruct((T, D), F32), jax.ShapeDtypeStruct((total_rows, D), F32),
                   jax.ShapeDtypeStruct((total_rows, LANES), I32), jax.ShapeDtypeStruct((T, LANES), F32)],
        input_output_aliases=aliases,
        compiler_params=_cparams(("arbitrary",)),
        name="out_router",
    )(*args)


def _weight_ring_step(rt, w_hbm, wbuf, sem, tcols):
    te_ref, tv_ref, tf_ref, tnx_ref, tlast_ref, trun_ref, nr_ref = rt
    n = pl.program_id(0)
    m = pl.program_id(1)
    slot = lax.rem(n * nr_ref[0] + trun_ref[m], 2)

    def copy(e, nn, s):
        c0 = pl.multiple_of(nn * tcols, tcols)
        return pltpu.make_async_copy(w_hbm.at[e, :, pl.ds(c0, tcols)], wbuf.at[s], sem.at[s])

    @pl.when((n == 0) & (m == 0))
    def _():
        copy(te_ref[m], n, slot).start()

    copy(te_ref[m], n, slot).wait()
    last = tlast_ref[m] != 0

    @pl.when(jnp.logical_or(jnp.logical_not(last), n + 1 < pl.num_programs(0)))
    def _():
        copy(tnx_ref[m], jnp.where(last, n + 1, n), 1 - slot).start()

    return slot


def _moe_gate_up_kernel(te_ref, tv_ref, tf_ref, tnx_ref, tlast_ref, trun_ref, nr_ref,
                        x_ref, w_hbm, bg_ref, bu_ref, act_ref, wbuf, wc_scr, sem):
    rt = (te_ref, tv_ref, tf_ref, tnx_ref, tlast_ref, trun_ref, nr_ref)
    m = pl.program_id(1)
    tnw = wbuf.shape[2]
    grp = 2 * LANES
    ngrp = tnw // grp

    @pl.when(tv_ref[m] != 0)
    def _():
        @pl.when(tf_ref[m] != 0)
        def _():
            slot = _weight_ring_step(rt, w_hbm, wbuf, sem, tnw)
            src = lax.broadcasted_iota(I32, (grp, grp), 0)
            dst = lax.broadcasted_iota(I32, (grp, grp), 1)
            want = jnp.where(dst < LANES, 2 * dst, 2 * (dst - LANES) + 1)
            perm = jnp.where(src == want, 1.0, 0.0).astype(BF16)
            for g in range(ngrp):
                wt = wbuf[slot, :, g * grp:(g + 1) * grp].astype(BF16)
                wc_scr[:, g * grp:(g + 1) * grp] = jnp.dot(wt, perm, preferred_element_type=F32).astype(BF16)

        x = x_ref[...].astype(BF16)
        bg = bg_ref[0]
        bu = bu_ref[0]
        for g in range(ngrp):
            gu = jnp.dot(x, wc_scr[:, g * grp:(g + 1) * grp], preferred_element_type=F32)
            gate = jnp.minimum(gu[:, 0:LANES] + bg[:, g * LANES:(g + 1) * LANES], SWIGLU_LIMIT)
            up = jnp.clip(gu[:, LANES:grp] + bu[:, g * LANES:(g + 1) * LANES], -SWIGLU_LIMIT, SWIGLU_LIMIT)
            act_ref[:, g * LANES:(g + 1) * LANES] = (
                (up + 1.0) * (gate * _sigmoid_t(gate * SWIGLU_ALPHA))).astype(act_ref.dtype)

    @pl.when(tv_ref[m] == 0)
    def _():
        act_ref[...] = jnp.zeros(act_ref.shape, act_ref.dtype)


def _moe_gate_up(rt, xs, w_gate_up, bg, bu):
    tm, tnw = MOE_TM, MOE_TNW
    P, D = xs.shape
    F2 = w_gate_up.shape[2]
    imap_b = lambda n, m, te, *_: (te[m], 0, n)
    return pl.pallas_call(
        _moe_gate_up_kernel,
        grid_spec=pltpu.PrefetchScalarGridSpec(
            num_scalar_prefetch=len(rt),
            grid=(F2 // tnw, P // tm),
            in_specs=[
                pl.BlockSpec((tm, D), lambda n, m, *_: (m, 0)),
                pl.BlockSpec(memory_space=pl.ANY),
                pl.BlockSpec((1, 1, tnw // 2), imap_b),
                pl.BlockSpec((1, 1, tnw // 2), imap_b),
            ],
            out_specs=pl.BlockSpec((tm, tnw // 2), lambda n, m, *_: (m, n)),
            scratch_shapes=[pltpu.VMEM((2, D, tnw), F32), pltpu.VMEM((D, tnw), BF16),
                            pltpu.SemaphoreType.DMA((2,))],
        ),
        out_shape=jax.ShapeDtypeStruct((P, F2 // 2), BF16),
        compiler_params=_cparams(("arbitrary", "arbitrary")),
        name="moe_gate_up",
    )(*rt, xs, w_gate_up, bg, bu)


def _moe_down_kernel(te_ref, tv_ref, tf_ref, tnx_ref, tlast_ref, trun_ref, nr_ref,
                     a_ref, w_hbm, bd_ref, y_ref, wbuf, wc_scr, sem):
    rt = (te_ref, tv_ref, tf_ref, tnx_ref, tlast_ref, trun_ref, nr_ref)
    m = pl.program_id(1)

    @pl.when(tv_ref[m] != 0)
    def _():
        @pl.when(tf_ref[m] != 0)
        def _():
            slot = _weight_ring_step(rt, w_hbm, wbuf, sem, wbuf.shape[2])
            wc_scr[...] = wbuf[slot].astype(BF16)

        y_ref[...] = jnp.dot(a_ref[...], wc_scr[...], preferred_element_type=F32) + bd_ref[0]

    @pl.when(tv_ref[m] == 0)
    def _():
        y_ref[...] = jnp.zeros(y_ref.shape, y_ref.dtype)


def _moe_down(rt, act, w_down, bd):
    tm, tn = MOE_TM, MOE_TN
    P, F = act.shape
    D = w_down.shape[2]
    return pl.pallas_call(
        _moe_down_kernel,
        grid_spec=pltpu.PrefetchScalarGridSpec(
            num_scalar_prefetch=len(rt),
            grid=(D // tn, P // tm),
            in_specs=[
                pl.BlockSpec((tm, F), lambda n, m, *_: (m, 0)),
                pl.BlockSpec(memory_space=pl.ANY),
                pl.BlockSpec((1, 1, tn), lambda n, m, te, *_: (te[m], 0, n)),
            ],
            out_specs=pl.BlockSpec((tm, tn), lambda n, m, *_: (m, n)),
            scratch_shapes=[pltpu.VMEM((2, F, tn), F32), pltpu.VMEM((F, tn), BF16),
                            pltpu.SemaphoreType.DMA((2,))],
        ),
        out_shape=jax.ShapeDtypeStruct((P, D), F32),
        compiler_params=_cparams(("arbitrary", "arbitrary")),
        name="moe_down",
    )(*rt, act, w_down, bd)


def _combine_kernel(x1_ref, yg_ref, tw_ref, fn_ref, y_ref):
    tw = tw_ref[...]
    moe = tw[:, 0:1] * yg_ref[0]
    for kk in range(1, TOP_K):
        moe = moe + tw[:, kk:kk + 1] * yg_ref[kk]
    y_ref[...] = _rms(x1_ref[...] + moe, fn_ref[...])


def _combine(x1, yg, tw, final_norm, tm, row_off):
    T, D = x1.shape
    off = row_off // tm
    return pl.pallas_call(
        _combine_kernel,
        grid=(T // tm,),
        in_specs=[
            pl.BlockSpec((tm, D), lambda i: (i, 0)),
            pl.BlockSpec((TOP_K, tm, D), lambda i: (0, i + off, 0)),
            pl.BlockSpec((tm, LANES), lambda i: (i, 0)),
            pl.BlockSpec((1, D), lambda i: (0, 0)),
        ],
        out_specs=pl.BlockSpec((tm, D), lambda i: (i, 0)),
        out_shape=jax.ShapeDtypeStruct((T, D), F32),
        compiler_params=_cparams(("parallel",)),
        name="combine",
    )(x1, yg, tw, final_norm)


def _routing(top_i):
    T = top_i.shape[0]
    P = T * TOP_K
    tm = MOE_TM
    n_tiles = -(-P // tm) + N_EXPERTS
    e_flat = top_i.reshape(P)
    onehot = (e_flat[:, None] == jnp.arange(N_EXPERTS, dtype=I32)[None, :]).astype(I32)
    csum = jnp.cumsum(onehot, axis=0)
    rank = jnp.sum((csum - 1) * onehot, axis=1)
    counts = csum[-1]
    padded = ((counts + tm - 1) // tm) * tm
    ends = jnp.cumsum(padded)
    starts = ends - padded
    pos = starts[e_flat] + rank
    slot_token = jnp.zeros((n_tiles * tm,), I32).at[pos].set(jnp.arange(P, dtype=I32) // TOP_K)
    tile_start = jnp.arange(n_tiles, dtype=I32) * tm
    tile_valid = (tile_start < ends[-1]).astype(I32)
    tile_expert = jnp.minimum(jnp.sum((tile_start[:, None] >= ends[None, :]).astype(I32), axis=1), N_EXPERTS - 1)
    tile_expert = tile_expert.astype(I32)
    tile_first = jnp.concatenate([jnp.ones((1,), I32), (tile_expert[1:] != tile_expert[:-1]).astype(I32)])
    tile_first = tile_first * tile_valid
    tile_run = jnp.cumsum(tile_first) - 1
    n_runs = jnp.sum(tile_first).reshape(1)
    experts = jnp.arange(N_EXPERTS, dtype=I32)
    later = (counts > 0)[None, :] & (experts[None, :] > tile_expert[:, None])
    nxt = jnp.min(jnp.where(later, experts[None, :], N_EXPERTS), axis=1)
    tile_last = (nxt == N_EXPERTS).astype(I32)
    tile_next = jnp.where(nxt == N_EXPERTS, tile_expert[0], nxt).astype(I32)
    tables = (tile_expert, tile_valid, tile_first, tile_next, tile_last, tile_run.astype(I32), n_runs.astype(I32))
    return pos, slot_token, tables


def kernel(x_prompt, x_sample, state_gdn, state_conv, state_hgrn, lb_table, norm1, w_in, conv_w, A_log, dt_bias,
           gdn_norm, hg_norm, w_out, norm2, w_router, b_router, w_gate_up, b_gate_up, w_down, b_down, final_norm):
    depth = w_in.shape[0]
    assert depth == 1
    B, L, D = x_prompt.shape
    SB = x_sample.shape[0]
    assert x_sample.shape[1] == 1
    nqk = HEADS * HD
    cd = 3 * nqk

    lbs = jnp.cumsum(jax.nn.softmax(lb_table.astype(F32), axis=0), axis=0)
    lb = lbs[0:1]

    assert w_in.shape[2] == MAIN_COLS + 2 * HEADS and cd + nqk == MAIN_COLS // 2
    ba0 = cd + nqk
    w_t = jnp.swapaxes(w_in[0], 0, 1).astype(BF16)
    wo_a = w_out[0, :nqk].astype(BF16)
    wo_b = w_out[0, nqk:].astype(BF16)
    wr_hi, wr_lo = _split(jnp.pad(w_router[0], ((0, 0), (0, LANES - N_EXPERTS))))
    br_p = jnp.pad(b_router[0], (0, LANES - N_EXPERTS))[None, :]
    bg = b_gate_up[0, :, None, 0::2]
    bu = b_gate_up[0, :, None, 1::2]
    bd = b_down[0][:, None, :]
    alog_b = jnp.pad(A_log[0], (HEADS, LANES - 2 * HEADS))[None, :]
    dtb_b = jnp.pad(dt_bias[0], (HEADS, LANES - 2 * HEADS))[None, :]
    alog_row = jnp.pad(A_log[0], (0, LANES - HEADS))[None, :]
    dtb_row = jnp.pad(dt_bias[0], (0, LANES - HEADS))[None, :]
    n1 = norm1[0][None, :]
    n2 = norm2[0][None, :]
    gn = gdn_norm[0][None, :]
    hn = hg_norm[0][None, :]
    fnw = final_norm[None, :]
    cw = conv_w[0]

    xp = x_prompt.reshape(B * L, D)
    xs = x_sample.reshape(SB, D)

    proj_p, ba_p = _in_proj(xp, n1, w_t, ba0, 2 * HEADS, tm=1024, tn=1024)
    proj_s, ba_s = _in_proj(xs, n1, w_t, ba0, 2 * HEADS, tm=SB, tn=1024)

    oa_p, sg_p = _gdn_prompt(proj_p, ba_p, cw, alog_b, dtb_b, gn, B, L)
    ob_p, sh_p = _hgrn_prompt(proj_p, lb, hn, B, L)
    oa_s, ob_s, sg_s, sh_s = _decode(proj_s, ba_s, state_conv[0].reshape(SB, (CONV_K - 1) * cd), state_gdn[0],
                                     state_hgrn[0], cw, alog_row, dtb_row, lb, gn, hn)

    ntok = B * L + SB
    x1_p, h2, ti, tw_p = _out_router(xp, oa_p, ob_p, wo_a, wo_b, n2, wr_hi, wr_lo, br_p,
                                     tm=512, total_rows=ntok, row_off=0)
    x1_s, h2, ti, tw_s = _out_router(xs, oa_s, ob_s, wo_a, wo_b, n2, wr_hi, wr_lo, br_p,
                                     tm=SB, total_rows=ntok, row_off=B * L, into=(h2, ti))

    top_i = ti[:, :TOP_K]
    pos, slot_token, tables = _routing(top_i)
    xs_sorted = h2.at[slot_token].get(mode="promise_in_bounds")
    act = _moe_gate_up(tables, xs_sorted, w_gate_up[0], bg, bu)
    yslots = _moe_down(tables, act, w_down[0], bd)
    pos_kmajor = pos.reshape(B * L + SB, TOP_K).T.reshape(-1)
    yg = yslots.at[pos_kmajor].get(mode="promise_in_bounds").reshape(TOP_K, B * L + SB, D)

    y_p = _combine(x1_p, yg, tw_p, fnw, tm=256, row_off=0)
    y_s = _combine(x1_s, yg, tw_s, fnw, tm=SB, row_off=B * L)

    conv_p = proj_p.reshape(B, L, MAIN_COLS)[:, L - (CONV_K - 1):, :cd]
    conv_s = jnp.concatenate([state_conv[0][:, 1:, :], proj_s[:, None, :cd]], axis=1)
    return (y_p.reshape(B, L, D), y_s.reshape(SB, 1, D),
            sg_p[None], conv_p[None].astype(state_conv.dtype), sh_p[None],
            sg_s[None], conv_s[None].astype(state_conv.dtype), sh_s[None])
```

```python
import functools

import jax
import jax.numpy as jnp
from jax import lax
from jax.experimental import pallas as pl
from jax.experimental.pallas import tpu as pltpu

F32 = jnp.float32
BF16 = jnp.bfloat16
I32 = jnp.int32

EPS = 1e-6
HEADS = 8
HD = 128
CONV_K = 4
N_EXPERTS = 32
TOP_K = 4
SWIGLU_ALPHA = 1.702
SWIGLU_LIMIT = 7.0
LANES = 128
NEG = -1e30

VMEM_LIMIT = 56 * 1024 * 1024

QB, KB, VB, ZB, HQB, HFB, HIB, HGB = (i * HEADS for i in range(8))
MAIN_COLS = 8 * HEADS * HD

HEADS_PER_STEP = 2
GDN_CHUNK = 128
GDN_TB = 512
HG_CHUNK = 16
HG_TB = 512
HG_GROUP = 4
DEC_TB = 8
MOE_TM = 256
MOE_TN = 2048
MOE_TNW = 2048
MOE_XSLOTS = 3


def _cparams(sem):
    return pltpu.CompilerParams(dimension_semantics=sem, vmem_limit_bytes=VMEM_LIMIT)


def _mm(a, b):
    return jnp.dot(a.astype(BF16), b.astype(BF16), preferred_element_type=F32)


def _mm_nt(a, b):
    return lax.dot_general(a.astype(BF16), b.astype(BF16), (((1,), (1,)), ((), ())),
                           preferred_element_type=F32)


def _mm_tn(a, b):
    return lax.dot_general(a.astype(BF16), b.astype(BF16), (((0,), (0,)), ((), ())),
                           preferred_element_type=F32)


def _mmh(a, b):
    return jnp.dot(a, b, precision=lax.Precision.HIGHEST, preferred_element_type=F32)


def _split(a):
    hi = a.astype(BF16)
    return hi, (a - hi.astype(F32)).astype(BF16)


def _mm3(a, b):
    d = lambda x, y: jnp.dot(x, y, preferred_element_type=F32)
    return d(a[0], b[0]) + (d(a[0], b[1]) + d(a[1], b[0]))


def _mm_exact_lhs(l_bf16, x):
    d = lambda y: jnp.dot(l_bf16, y, preferred_element_type=F32)
    x0 = x.astype(BF16)
    r1 = x - x0.astype(F32)
    x1 = r1.astype(BF16)
    x2 = (r1 - x1.astype(F32)).astype(BF16)
    return d(x0) + (d(x1) + d(x2))


def _sigmoid(x):
    return 1.0 / (1.0 + jnp.exp(-x))


def _sigmoid_t(x):
    return 0.5 * jnp.tanh(0.5 * x) + 0.5


def _silu(x):
    return x * _sigmoid_t(x)


def _softplus(x):
    return jnp.maximum(x, 0.0) + jnp.log1p(jnp.exp(-jnp.abs(x)))


def _rms(x, w):
    return x * lax.rsqrt(jnp.mean(x * x, axis=-1, keepdims=True) + EPS) * w


def _l2n(x):
    return x * lax.rsqrt(jnp.sum(x * x, axis=-1, keepdims=True) + EPS)


def _in_proj_kernel(x_ref, nw_ref, wt_ref, wbat_ref, o_ref, ba_ref, h_scr):
    nt_dims = (((1,), (1,)), ((), ()))

    @pl.when(pl.program_id(1) == 0)
    def _():
        h = _rms(x_ref[...], nw_ref[...]).astype(BF16)
        h_scr[...] = h
        ba_ref[...] = lax.dot_general(h, wbat_ref[...], nt_dims, preferred_element_type=F32)

    o_ref[...] = lax.dot_general(h_scr[...], wt_ref[...], nt_dims, preferred_element_type=F32)


def _in_proj(x, norm_w, w_t, ba0, nba, tm, tn):
    T, D = x.shape
    N = w_t.shape[0] - nba
    assert ba0 % tn == 0 and nba % 16 == 0
    return pl.pallas_call(
        _in_proj_kernel,
        grid=(T // tm, N // tn),
        in_specs=[
            pl.BlockSpec((tm, D), lambda i, j: (i, 0)),
            pl.BlockSpec((1, D), lambda i, j: (0, 0)),
            pl.BlockSpec((pl.Element(tn), pl.Element(D)),
                         lambda i, j: (pl.multiple_of(j * tn + jnp.where(j * tn >= ba0, nba, 0), 16), 0)),
            pl.BlockSpec((pl.Element(LANES), pl.Element(D)), lambda i, j: (ba0, 0)),
        ],
        out_specs=[
            pl.BlockSpec((tm, tn), lambda i, j: (i, j)),
            pl.BlockSpec((tm, LANES), lambda i, j: (i, 0)),
        ],
        out_shape=[jax.ShapeDtypeStruct((T, N), F32), jax.ShapeDtypeStruct((T, LANES), F32)],
        scratch_shapes=[pltpu.VMEM((tm, D), BF16)],
        compiler_params=_cparams(("parallel", "arbitrary")),
        name="in_proj",
    )(x, norm_w, w_t, w_t)


def _unit_lower_inverse(As, ri, ci):
    n = As[0].shape[0]
    eye = (ri == ci).astype(F32)
    same16 = (ri // 16) == (ci // 16)
    s1 = [_split(jnp.where(same16, -A, 0.0)) for A in As]
    s2 = [_split(_mm3(s, s)) for s in s1]
    s4 = [_split(_mm3(s, s)) for s in s2]
    s8 = [_split(_mm3(s, s)) for s in s4]
    Ts = [eye + jnp.where(same16, -A, 0.0) for A in As]
    for sp in (s2, s4, s8):
        Ts = [T + _mm3(_split(T), s) for T, s in zip(Ts, sp)]
    size = 32
    while size <= n:
        off = ((ri // size) == (ci // size)) & ((ri // (size // 2)) != (ci // (size // 2)))
        bT = [T.astype(BF16) for T in Ts]
        TL = [_mm(bt, jnp.where(off, A, 0.0)) for bt, A in zip(bT, As)]
        Ts = [T - _mm(tl, bt) for T, tl, bt in zip(Ts, TL, bT)]
        size *= 2
    return Ts


def _gdn_prompt_kernel(q_ref, k_ref, v_ref, z_ref, ba_ref, cwq_ref, cwk_ref, cwv_ref, alog_ref, dtb_ref,
                       gn_ref, o_ref, s_ref, ubuf, s_scr, *, tb, chunk, hps):
    hg = pl.program_id(1)
    t = pl.program_id(2)

    @pl.when(t == 0)
    def _():
        ubuf[:, 0:8, :] = jnp.zeros((3, 8, hps * HD), F32)
        s_scr[...] = jnp.zeros((hps, HD, HD), F32)

    ubuf[0, 8:8 + tb, :] = q_ref[...]
    ubuf[1, 8:8 + tb, :] = k_ref[...]
    ubuf[2, 8:8 + tb, :] = v_ref[...]

    ba = ba_ref[...]
    lane = lax.broadcasted_iota(I32, (chunk, LANES), 1)
    beta_all = _sigmoid_t(ba)
    g_all = -jnp.exp(alog_ref[...]) * _softplus(ba + dtb_ref[...])

    ri = lax.broadcasted_iota(I32, (chunk, chunk), 0)
    ci = lax.broadcasted_iota(I32, (chunk, chunk), 1)
    causal = ci <= ri
    strict = ci < ri
    ltri = jnp.where(causal, 1.0, 0.0).astype(BF16)

    def conv(idx, w_ref, r0, c0):
        w = w_ref[:, c0:c0 + HD]
        acc = ubuf[idx, r0 + 5:r0 + 5 + chunk, c0:c0 + HD] * w[0:1, :]
        for j in range(1, CONV_K):
            acc = acc + ubuf[idx, r0 + 5 + j:r0 + 5 + j + chunk, c0:c0 + HD] * w[j:j + 1, :]
        return _silu(acc)

    nc = tb // chunk
    items = [(hh, c) for c in range(nc) for hh in range(hps)]
    qs, ks, vs, betas, gcols, decays, As = [], [], [], [], [], [], []
    gc_all = [_mm_exact_lhs(ltri, g_all[c * chunk:(c + 1) * chunk, :]) for c in range(nc)]
    for hh, c in items:
        r0, c0 = c * chunk, hh * HD
        h = hg * hps + hh
        q = _l2n(conv(0, cwq_ref, r0, c0)) * (HD ** -0.5)
        k = _l2n(conv(1, cwk_ref, r0, c0))
        v = conv(2, cwv_ref, r0, c0)
        beta = jnp.sum(jnp.where(lane == h, beta_all[r0:r0 + chunk, :], 0.0), axis=-1, keepdims=True)
        gcol = jnp.broadcast_to(
            jnp.sum(jnp.where(lane == h + HEADS, gc_all[c], 0.0), axis=-1, keepdims=True), (chunk, HD))
        decay = jnp.exp(jnp.where(causal, gcol - gcol.T, NEG))
        kb = k * beta
        qs.append(q); ks.append(k); vs.append(v); betas.append(beta); gcols.append(gcol); decays.append(decay)
        As.append(jnp.where(strict, _mm_nt(kb, k) * decay, 0.0))
    Ts = _unit_lower_inverse(As, ri, ci)
    us, ws, scs = [], [], []
    for n in range(len(items)):
        sT = _split(Ts[n])
        kb = ks[n] * betas[n]
        us.append(_mm3(sT, _split(vs[n] * betas[n])))
        ws.append(_mm3(sT, _split(kb * jnp.exp(gcols[n]))))
        scs.append(_mm_nt(qs[n], ks[n]) * decays[n])

    S = [s_scr[hh] for hh in range(hps)]
    for n, (hh, c) in enumerate(items):
        r0, c0 = c * chunk, hh * HD
        gcol = gcols[n]
        v_new = us[n] - _mm(ws[n], S[hh])
        o = _mm(qs[n] * jnp.exp(gcol), S[hh]) + _mm(scs[n], v_new)
        glast = gcol[chunk - 1:chunk, :]
        S[hh] = S[hh] * jnp.exp(glast) + _mm_tn(ks[n] * jnp.exp(glast - gcol), v_new)
        o_ref[r0:r0 + chunk, c0:c0 + HD] = (
            _rms(o, gn_ref[...]) * _silu(z_ref[r0:r0 + chunk, c0:c0 + HD])).astype(o_ref.dtype)
    for hh in range(hps):
        s_scr[hh] = S[hh]

    ubuf[:, 0:8, :] = ubuf[:, tb:tb + 8, :]

    @pl.when(t == pl.num_programs(2) - 1)
    def _():
        for hh in range(hps):
            s_ref[0, hh] = S[hh]


def _gdn_prompt(proj, ba, conv_w, alog_b, dtb_b, gdn_norm, B, L):
    tb, chunk, hps = GDN_TB, GDN_CHUNK, HEADS_PER_STEP
    nt = L // tb
    w = hps * HD
    row = lambda b, h, t: b * nt + t
    colspec = lambda base: pl.BlockSpec((tb, w), lambda b, h, t: (row(b, h, t), base // hps + h))
    cwspec = lambda base: pl.BlockSpec((CONV_K, w), lambda b, h, t: (0, base // hps + h))
    hvec = pl.BlockSpec((1, LANES), lambda b, h, t: (0, 0))
    return pl.pallas_call(
        functools.partial(_gdn_prompt_kernel, tb=tb, chunk=chunk, hps=hps),
        grid=(B, HEADS // hps, nt),
        in_specs=[
            colspec(QB), colspec(KB), colspec(VB), colspec(ZB),
            pl.BlockSpec((tb, LANES), lambda b, h, t: (row(b, h, t), 0)),
            cwspec(QB), cwspec(KB), cwspec(VB),
            hvec, hvec,
            pl.BlockSpec((1, HD), lambda b, h, t: (0, 0)),
        ],
        out_specs=[
            pl.BlockSpec((tb, w), lambda b, h, t: (row(b, h, t), h)),
            pl.BlockSpec((1, hps, HD, HD), lambda b, h, t: (b, h, 0, 0)),
        ],
        out_shape=[jax.ShapeDtypeStruct((B * L, HEADS * HD), BF16),
                   jax.ShapeDtypeStruct((B, HEADS, HD, HD), F32)],
        scratch_shapes=[pltpu.VMEM((3, tb + 8, w), F32), pltpu.VMEM((hps, HD, HD), F32)],
        compiler_params=_cparams(("parallel", "parallel", "arbitrary")),
        name="gdn_prompt",
    )(proj, proj, proj, proj, ba, conv_w, conv_w, conv_w, alog_b, dtb_b, gdn_norm)


def _hgrn_prompt_kernel(hq_ref, hf_ref, hi_ref, hg_ref, lb_ref, hn_ref, o_ref, s_ref,
                        g_scr, q_scr, k_scr, st_scr, *, tb, chunk, group, hps):
    t = pl.program_id(2)

    @pl.when(t == 0)
    def _():
        st_scr[...] = jnp.zeros((hps, HD, HD), F32)

    lb = lb_ref[...]
    hf = hf_ref[...]
    forget = lb + (1.0 - lb) * _sigmoid(hf)
    k_scr[...] = (1.0 - lb) * _sigmoid(-hf)
    q_scr[...] = _silu(hq_ref[...])
    ri = lax.broadcasted_iota(I32, (LANES, LANES), 0)
    ci = lax.broadcasted_iota(I32, (LANES, LANES), 1)
    lblk = jnp.where(((ri // chunk) == (ci // chunk)) & (ci <= ri), 1.0, 0.0).astype(BF16)
    lf = jnp.log(forget)
    for r in range(tb // LANES):
        g_scr[r * LANES:(r + 1) * LANES, :] = _mm_exact_lhs(lblk, lf[r * LANES:(r + 1) * LANES, :])

    half = chunk // 2
    row8 = lax.broadcasted_iota(I32, (half, HD), 0)
    lane8 = lax.broadcasted_iota(I32, (half, HD), 1)
    zpad = jnp.zeros((half, HD), F32)
    hn = hn_ref[...]

    def body(gi, sts_in):
        base = gi * (group * chunk)
        items = [(hh, cc) for cc in range(group) for hh in range(hps)]
        ni = range(len(items))
        rc = [(base + cc * chunk, hh * HD) for hh, cc in items]
        vs = [hi_ref[pl.ds(r0, chunk), c0:c0 + HD] for r0, c0 in rc]
        vts = [v.T for v in vs]
        blocks = []
        for r0, c0 in rc:
            for r in (r0, r0 + half):
                blocks.append((g_scr[pl.ds(r, half), c0:c0 + HD], q_scr[pl.ds(r, half), c0:c0 + HD],
                               k_scr[pl.ds(r, half), c0:c0 + HD]))
        accs = [zpad] * len(blocks)
        for j in range(half):
            for n, (Gb, qb, kb) in enumerate(blocks):
                e = jnp.exp(Gb - Gb[j:j + 1, :])
                col = jnp.sum(e * qb * kb[j:j + 1, :], axis=-1, keepdims=True)
                accs[n] = jnp.where(lane8 == (n % 2) * half + j, col, accs[n])
        accs = [jnp.where(row8 >= lane8 - (n % 2) * half, a, 0.0) for n, a in enumerate(accs)]
        Gs = [jnp.concatenate([blocks[2 * n][0], blocks[2 * n + 1][0]], axis=0) for n in ni]
        qs = [jnp.concatenate([blocks[2 * n][1], blocks[2 * n + 1][1]], axis=0) for n in ni]
        ks = [jnp.concatenate([blocks[2 * n][2], blocks[2 * n + 1][2]], axis=0) for n in ni]
        offd = []
        for n in ni:
            Ga, _, ka = blocks[2 * n]
            Gb, qb, _ = blocks[2 * n + 1]
            gmid = Ga[half - 1:half, :]
            kh = jnp.concatenate([ka * jnp.exp(gmid - Ga), zpad], axis=0)
            offd.append(_mm_nt(qb * jnp.exp(Gb - gmid), kh))
        glasts = [Gs[n][chunk - 1:chunk, :] for n in ni]
        incs = [_mm(vts[n], ks[n] * jnp.exp(glasts[n] - Gs[n])) for n in ni]
        intra = []
        for n in ni:
            s = jnp.concatenate([accs[2 * n][:, 0:chunk], accs[2 * n + 1][:, 0:chunk] + offd[n]], axis=0)
            intra.append(_mm(s, vs[n]))
        cur = list(sts_in)
        before = []
        for n, (hh, _) in enumerate(items):
            before.append(cur[hh])
            cur[hh] = cur[hh] * jnp.exp(glasts[n]) + incs[n]
        outs = [_mm_nt(qs[n] * jnp.exp(Gs[n]), before[n]) + intra[n] for n in ni]
        for n, (r0, c0) in enumerate(rc):
            o_ref[pl.ds(r0, chunk), c0:c0 + HD] = (
                _rms(outs[n], hn) * _silu(hg_ref[pl.ds(r0, chunk), c0:c0 + HD])).astype(o_ref.dtype)
        return cur

    sts = [st_scr[hh] for hh in range(hps)]
    for gi in range(tb // (group * chunk)):
        sts = body(gi, sts)
    for hh in range(hps):
        st_scr[hh] = sts[hh]

    @pl.when(t == pl.num_programs(2) - 1)
    def _():
        for hh in range(hps):
            s_ref[0, hh] = sts[hh].T


def _hgrn_prompt(proj, lb, hg_norm, B, L):
    tb, chunk, hps = HG_TB, HG_CHUNK, HEADS_PER_STEP
    nt = L // tb
    w = hps * HD
    row = lambda b, h, t: b * nt + t
    colspec = lambda base: pl.BlockSpec((tb, w), lambda b, h, t: (row(b, h, t), base // hps + h))
    return pl.pallas_call(
        functools.partial(_hgrn_prompt_kernel, tb=tb, chunk=chunk, group=HG_GROUP, hps=hps),
        grid=(B, HEADS // hps, nt),
        in_specs=[
            colspec(HQB), colspec(HFB), colspec(HIB), colspec(HGB),
            pl.BlockSpec((1, w), lambda b, h, t: (0, h)),
            pl.BlockSpec((1, HD), lambda b, h, t: (0, 0)),
        ],
        out_specs=[
            pl.BlockSpec((tb, w), lambda b, h, t: (row(b, h, t), h)),
            pl.BlockSpec((1, hps, HD, HD), lambda b, h, t: (b, h, 0, 0)),
        ],
        out_shape=[jax.ShapeDtypeStruct((B * L, HEADS * HD), BF16),
                   jax.ShapeDtypeStruct((B, HEADS, HD, HD), F32)],
        scratch_shapes=[pltpu.VMEM((tb, w), F32), pltpu.VMEM((tb, w), F32), pltpu.VMEM((tb, w), F32),
                        pltpu.VMEM((hps, HD, HD), F32)],
        compiler_params=_cparams(("parallel", "parallel", "arbitrary")),
        name="hgrn_prompt",
    )(proj, proj, proj, proj, lb, hg_norm)


def _column_broadcasts(x, sel):
    d = lambda y: lax.dot_general(y, sel, (((0,), (0,)), ((), ())), preferred_element_type=F32)
    hi, lo = _split(x)
    return d(hi) + d(lo)


def _decode_kernel(qkv_ref, z_ref, hq_ref, hf_ref, hi_ref, hg_ref, ba_ref, cs_ref, sg_ref, sh_ref,
                   cw_ref, alog_ref, dtb_ref, lb_ref, gn_ref, hn_ref,
                   oa_ref, ob_ref, sgo_ref, sho_ref, *, nb):
    cd = cw_ref.shape[1]
    cw = cw_ref[...]
    cs = cs_ref[...]
    acc = cs[:, 0:cd] * cw[0:1, :]
    acc = acc + cs[:, cd:2 * cd] * cw[1:2, :]
    acc = acc + cs[:, 2 * cd:3 * cd] * cw[2:3, :]
    acc = acc + qkv_ref[...] * cw[3:4, :]
    conv = _silu(acc)
    nqk = HEADS * HD
    ba = ba_ref[...]
    gn = gn_ref[...]
    hn = hn_ref[...]
    z = z_ref[...]
    hgate = hg_ref[...]

    qs, ks, vs = [], [], []
    for h in range(HEADS):
        qs.append(_l2n(conv[:, h * HD:(h + 1) * HD]) * (HD ** -0.5))
        ks.append(_l2n(conv[:, nqk + h * HD:nqk + (h + 1) * HD]))
        vs.append(conv[:, 2 * nqk + h * HD:2 * nqk + (h + 1) * HD])
    lb = lb_ref[...]
    hf = hf_ref[...]
    forget = lb + (1.0 - lb) * _sigmoid(hf)
    hk = (1.0 - lb) * _sigmoid(-hf)
    hq = _silu(hq_ref[...])
    hv = hi_ref[...]

    sel_r = lax.broadcasted_iota(I32, (nb, nb * HD), 0)
    sel_c = lax.broadcasted_iota(I32, (nb, nb * HD), 1)
    sel = jnp.where(sel_c // HD == sel_r, 1.0, 0.0).astype(BF16)

    for h in range(HEADS):
        hs = slice(h * HD, (h + 1) * HD)
        beta = _sigmoid(ba[:, h:h + 1])
        g = -jnp.exp(alog_ref[0:1, h:h + 1]) * _softplus(ba[:, HEADS + h:HEADS + h + 1] + dtb_ref[0:1, h:h + 1])
        eg = jnp.exp(g)
        qk = jnp.sum(qs[h] * ks[h], axis=-1, keepdims=True)
        kB = _column_broadcasts(ks[h], sel)
        qB = _column_broadcasts(qs[h], sel)
        fB = _column_broadcasts(forget[:, hs], sel)
        hkB = _column_broadcasts(hk[:, hs], sel)
        hqB = _column_broadcasts(hq[:, hs], sel)
        for b in range(nb):
            bs = slice(b * HD, (b + 1) * HD)
            S = sg_ref[b, h]
            kS = jnp.sum(S * kB[:, bs], axis=0, keepdims=True)
            qS = jnp.sum(S * qB[:, bs], axis=0, keepdims=True)
            egb = eg[b:b + 1, :]
            v_new = beta[b:b + 1, :] * (vs[h][b:b + 1, :] - egb * kS)
            o = egb * qS + qk[b:b + 1, :] * v_new
            sgo_ref[b, h] = S * egb + kB[:, bs] * v_new
            oa_ref[b:b + 1, hs] = _rms(o, gn) * _silu(z[b:b + 1, hs])
            Sh = sh_ref[b, h] * fB[:, bs] + hkB[:, bs] * hv[b:b + 1, hs]
            sho_ref[b, h] = Sh
            ob = jnp.sum(Sh * hqB[:, bs], axis=0, keepdims=True)
            ob_ref[b:b + 1, hs] = _rms(ob, hn) * _silu(hgate[b:b + 1, hs])


def _decode(proj, ba, conv_state2d, s_gdn, s_hg, conv_w, alog_row, dtb_row, lb, gdn_norm, hg_norm):
    nb = DEC_TB
    T = proj.shape[0]
    cd = conv_w.shape[1]
    w = HEADS * HD
    cblk = lambda width, idx: pl.BlockSpec((nb, width), lambda i: (i, idx))
    sblk = pl.BlockSpec((nb, HEADS, HD, HD), lambda i: (i, 0, 0, 0))
    full = lambda shape: pl.BlockSpec(shape, lambda i: (0,) * len(shape))
    return pl.pallas_call(
        functools.partial(_decode_kernel, nb=nb),
        grid=(T // nb,),
        in_specs=[
            cblk(cd, 0), cblk(w, ZB // HEADS), cblk(w, HQB // HEADS), cblk(w, HFB // HEADS),
            cblk(w, HIB // HEADS), cblk(w, HGB // HEADS),
            cblk(LANES, 0), cblk(3 * cd, 0), sblk, sblk,
            full((CONV_K, cd)), full((1, LANES)), full((1, LANES)), full((1, w)), full((1, HD)), full((1, HD)),
        ],
        out_specs=[cblk(w, 0), cblk(w, 0), sblk, sblk],
        out_shape=[jax.ShapeDtypeStruct((T, w), F32), jax.ShapeDtypeStruct((T, w), F32),
                   jax.ShapeDtypeStruct(s_gdn.shape, F32), jax.ShapeDtypeStruct(s_hg.shape, F32)],
        compiler_params=_cparams(("parallel",)),
        name="decode",
    )(proj, proj, proj, proj, proj, proj, ba, conv_state2d, s_gdn, s_hg,
      conv_w, alog_row, dtb_row, lb, gdn_norm, hg_norm)


def _out_router_kernel(x_ref, oa_ref, ob_ref, woa_ref, wob_ref, n2_ref, wrh_ref, wrl_ref, br_ref,
                       x1_ref, h2_ref, ti_ref, tw_ref, *, nvalid):
    i = pl.program_id(0)

    @pl.when(i >= nvalid)
    def _():
        h2_ref[...] = jnp.zeros(h2_ref.shape, h2_ref.dtype)
        ti_ref[...] = jnp.zeros(ti_ref.shape, ti_ref.dtype)

    @pl.when(i < nvalid)
    def _():
        _out_router_tile(x_ref, oa_ref, ob_ref, woa_ref, wob_ref, n2_ref, wrh_ref, wrl_ref, br_ref,
                         x1_ref, h2_ref, ti_ref, tw_ref)


def _out_router_tile(x_ref, oa_ref, ob_ref, woa_ref, wob_ref, n2_ref, wrh_ref, wrl_ref, br_ref,
                     x1_ref, h2_ref, ti_ref, tw_ref):
    y = x_ref[...] + jnp.dot(oa_ref[...].astype(BF16), woa_ref[...], preferred_element_type=F32)
    y = y + jnp.dot(ob_ref[...].astype(BF16), wob_ref[...], preferred_element_type=F32)
    x1_ref[...] = y
    h2 = _rms(y, n2_ref[...])
    h2_ref[...] = h2
    logits = _mm3(_split(h2), (wrh_ref[...], wrl_ref[...])) + br_ref[...]
    lane = lax.broadcasted_iota(I32, logits.shape, 1)
    logits = jnp.where(lane < N_EXPERTS, logits, NEG)
    ti = jnp.zeros(logits.shape, I32)
    tw = jnp.zeros(logits.shape, F32)
    m0 = None
    for kk in range(TOP_K):
        m = jnp.max(logits, axis=-1, keepdims=True)
        idx = jnp.min(jnp.where(logits == m, lane, LANES), axis=-1, keepdims=True)
        if m0 is None:
            m0 = m
        ti = jnp.where(lane == kk, idx, ti)
        tw = jnp.where(lane == kk, jnp.exp(m - m0), tw)
        logits = jnp.where(lane == idx, NEG * 2.0, logits)
    tw_ref[...] = tw / jnp.sum(tw, axis=-1, keepdims=True)
    ti_ref[...] = ti


def _out_router_into_kernel(h2_all_ref, ti_all_ref, *refs, nvalid):
    del h2_all_ref, ti_all_ref
    _out_router_kernel(*refs, nvalid=nvalid)


def _out_router(x, oa, ob, wo_a, wo_b, norm2, wr_hi, wr_lo, b_router_p, tm, total_rows, row_off, into=None):
    T, D = x.shape
    w = oa.shape[1]
    off = row_off // tm
    nvalid = T // tm
    steps = nvalid if into is not None else -(-total_rows // tm)
    rowblk = lambda width: pl.BlockSpec((tm, width), lambda i: (jnp.minimum(i, nvalid - 1), 0))
    allblk = lambda width: pl.BlockSpec((tm, width), lambda i: (i + off, 0))
    full = lambda shape: pl.BlockSpec(shape, lambda i: (0,) * len(shape))
    in_specs = [rowblk(D), rowblk(w), rowblk(w), full((w, D)), full((w, D)), full((1, D)),
                full((D, LANES)), full((D, LANES)), full((1, LANES))]
    args = (x, oa, ob, wo_a, wo_b, norm2, wr_hi, wr_lo, b_router_p)
    body, aliases = _out_router_kernel, {}
    if into is not None:
        in_specs = [pl.BlockSpec(memory_space=pl.ANY)] * 2 + in_specs
        args = tuple(into) + args
        body, aliases = _out_router_into_kernel, {0: 1, 1: 2}
    return pl.pallas_call(
        functools.partial(body, nvalid=nvalid),
        grid=(steps,),
        in_specs=in_specs,
        out_specs=[rowblk(D), allblk(D), allblk(LANES), rowblk(LANES)],
        out_shape=[jax.ShapeDtypeStruct((T, D), F32), jax.ShapeDtypeStruct((total_rows, D), F32),
                   jax.ShapeDtypeStruct((total_rows, LANES), I32), jax.ShapeDtypeStruct((T, LANES), F32)],
        input_output_aliases=aliases,
        compiler_params=_cparams(("arbitrary",)),
        name="out_router",
    )(*args)


def _weight_ring_step(rt, w_hbm, wbuf, sem, tcols):
    te_ref, tv_ref, tf_ref, tnx_ref, tlast_ref, trun_ref, nr_ref = rt
    n = pl.program_id(0)
    m = pl.program_id(1)
    slot = lax.rem(n * nr_ref[0] + trun_ref[m], 2)

    def copy(e, nn, s):
        c0 = pl.multiple_of(nn * tcols, tcols)
        return pltpu.make_async_copy(w_hbm.at[e, :, pl.ds(c0, tcols)], wbuf.at[s], sem.at[s])

    @pl.when((n == 0) & (m == 0))
    def _():
        copy(te_ref[m], n, slot).start()

    copy(te_ref[m], n, slot).wait()
    last = tlast_ref[m] != 0

    @pl.when(jnp.logical_or(jnp.logical_not(last), n + 1 < pl.num_programs(0)))
    def _():
        copy(tnx_ref[m], jnp.where(last, n + 1, n), 1 - slot).start()

    return slot


def _tile_ring_step(x_hbm, xbuf, xsem):
    n = pl.program_id(0)
    m = pl.program_id(1)
    n_m = pl.num_programs(1)
    total = pl.num_programs(0) * n_m
    tm = xbuf.shape[1]
    nslot = xbuf.shape[0]
    s = n * n_m + m

    def copy(step):
        r0 = pl.multiple_of(lax.rem(step, n_m) * tm, tm)
        slot = lax.rem(step, nslot)
        return pltpu.make_async_copy(x_hbm.at[pl.ds(r0, tm), :], xbuf.at[slot], xsem.at[slot])

    @pl.when(s == 0)
    def _():
        for first in range(nslot - 1):
            copy(first).start()

    @pl.when(s + nslot - 1 < total)
    def _():
        copy(s + nslot - 1).start()

    copy(s).wait()
    return lax.rem(s, nslot)


def _moe_gate_up_kernel(te_ref, tv_ref, tf_ref, tnx_ref, tlast_ref, trun_ref, nr_ref,
                        x_hbm, w_hbm, bg_ref, bu_ref, act_ref, wbuf, wc_scr, sem, xbuf, xsem):
    rt = (te_ref, tv_ref, tf_ref, tnx_ref, tlast_ref, trun_ref, nr_ref)
    m = pl.program_id(1)
    tnw = wbuf.shape[2]
    xslot = _tile_ring_step(x_hbm, xbuf, xsem)
    grp = 2 * LANES
    ngrp = tnw // grp

    @pl.when(tv_ref[m] != 0)
    def _():
        @pl.when(tf_ref[m] != 0)
        def _():
            slot = _weight_ring_step(rt, w_hbm, wbuf, sem, tnw)
            src = lax.broadcasted_iota(I32, (grp, grp), 0)
            dst = lax.broadcasted_iota(I32, (grp, grp), 1)
            want = jnp.where(dst < LANES, 2 * dst, 2 * (dst - LANES) + 1)
            perm = jnp.where(src == want, 1.0, 0.0).astype(BF16)
            for g in range(ngrp):
                wt = wbuf[slot, :, g * grp:(g + 1) * grp].astype(BF16)
                wc_scr[:, g * grp:(g + 1) * grp] = jnp.dot(wt, perm, preferred_element_type=F32).astype(BF16)

        x = xbuf[xslot].astype(BF16)
        bg = bg_ref[0]
        bu = bu_ref[0]
        for g in range(ngrp):
            gu = jnp.dot(x, wc_scr[:, g * grp:(g + 1) * grp], preferred_element_type=F32)
            gate = jnp.minimum(gu[:, 0:LANES] + bg[:, g * LANES:(g + 1) * LANES], SWIGLU_LIMIT)
            up = jnp.clip(gu[:, LANES:grp] + bu[:, g * LANES:(g + 1) * LANES], -SWIGLU_LIMIT, SWIGLU_LIMIT)
            act_ref[:, g * LANES:(g + 1) * LANES] = (
                (up + 1.0) * (gate * _sigmoid_t(gate * SWIGLU_ALPHA))).astype(act_ref.dtype)

    @pl.when(tv_ref[m] == 0)
    def _():
        act_ref[...] = jnp.zeros(act_ref.shape, act_ref.dtype)


def _moe_gate_up(rt, xs, w_gate_up, bg, bu):
    tm, tnw = MOE_TM, MOE_TNW
    P, D = xs.shape
    F2 = w_gate_up.shape[2]
    imap_b = lambda n, m, te, *_: (te[m], 0, n)
    return pl.pallas_call(
        _moe_gate_up_kernel,
        grid_spec=pltpu.PrefetchScalarGridSpec(
            num_scalar_prefetch=len(rt),
            grid=(F2 // tnw, P // tm),
            in_specs=[
                pl.BlockSpec(memory_space=pl.ANY),
                pl.BlockSpec(memory_space=pl.ANY),
                pl.BlockSpec((1, 1, tnw // 2), imap_b),
                pl.BlockSpec((1, 1, tnw // 2), imap_b),
            ],
            out_specs=pl.BlockSpec((tm, tnw // 2), lambda n, m, *_: (m, n)),
            scratch_shapes=[pltpu.VMEM((2, D, tnw), F32), pltpu.VMEM((D, tnw), BF16),
                            pltpu.SemaphoreType.DMA((2,)),
                            pltpu.VMEM((MOE_XSLOTS, tm, D), xs.dtype), pltpu.SemaphoreType.DMA((MOE_XSLOTS,))],
        ),
        out_shape=jax.ShapeDtypeStruct((P, F2 // 2), BF16),
        compiler_params=_cparams(("arbitrary", "arbitrary")),
        name="moe_gate_up",
    )(*rt, xs, w_gate_up, bg, bu)


def _moe_down_kernel(te_ref, tv_ref, tf_ref, tnx_ref, tlast_ref, trun_ref, nr_ref,
                     a_hbm, w_hbm, bd_ref, y_ref, wbuf, wc_scr, sem, abuf, asem):
    rt = (te_ref, tv_ref, tf_ref, tnx_ref, tlast_ref, trun_ref, nr_ref)
    m = pl.program_id(1)
    aslot = _tile_ring_step(a_hbm, abuf, asem)

    @pl.when(tv_ref[m] != 0)
    def _():
        @pl.when(tf_ref[m] != 0)
        def _():
            slot = _weight_ring_step(rt, w_hbm, wbuf, sem, wbuf.shape[2])
            wc_scr[...] = wbuf[slot].astype(BF16)

        y_ref[...] = jnp.dot(abuf[aslot], wc_scr[...], preferred_element_type=F32) + bd_ref[0]

    @pl.when(tv_ref[m] == 0)
    def _():
        y_ref[...] = jnp.zeros(y_ref.shape, y_ref.dtype)


def _moe_down(rt, act, w_down, bd):
    tm, tn = MOE_TM, MOE_TN
    P, F = act.shape
    D = w_down.shape[2]
    return pl.pallas_call(
        _moe_down_kernel,
        grid_spec=pltpu.PrefetchScalarGridSpec(
            num_scalar_prefetch=len(rt),
            grid=(D // tn, P // tm),
            in_specs=[
                pl.BlockSpec(memory_space=pl.ANY),
                pl.BlockSpec(memory_space=pl.ANY),
                pl.BlockSpec((1, 1, tn), lambda n, m, te, *_: (te[m], 0, n)),
            ],
            out_specs=pl.BlockSpec((tm, tn), lambda n, m, *_: (m, n)),
            scratch_shapes=[pltpu.VMEM((2, F, tn), F32), pltpu.VMEM((F, tn), BF16),
                            pltpu.SemaphoreType.DMA((2,)),
                            pltpu.VMEM((MOE_XSLOTS, tm, F), act.dtype), pltpu.SemaphoreType.DMA((MOE_XSLOTS,))],
        ),
        out_shape=jax.ShapeDtypeStruct((P, D), F32),
        compiler_params=_cparams(("arbitrary", "arbitrary")),
        name="moe_down",
    )(*rt, act, w_down, bd)


def _combine_kernel(x1_ref, yg_ref, tw_ref, fn_ref, y_ref):
    tw = tw_ref[...]
    moe = tw[:, 0:1] * yg_ref[0]
    for kk in range(1, TOP_K):
        moe = moe + tw[:, kk:kk + 1] * yg_ref[kk]
    y_ref[...] = _rms(x1_ref[...] + moe, fn_ref[...])


def _combine(x1, yg, tw, final_norm, tm, row_off):
    T, D = x1.shape
    off = row_off // tm
    return pl.pallas_call(
        _combine_kernel,
        grid=(T // tm,),
        in_specs=[
            pl.BlockSpec((tm, D), lambda i: (i, 0)),
            pl.BlockSpec((TOP_K, tm, D), lambda i: (0, i + off, 0)),
            pl.BlockSpec((tm, LANES), lambda i: (i, 0)),
            pl.BlockSpec((1, D), lambda i: (0, 0)),
        ],
        out_specs=pl.BlockSpec((tm, D), lambda i: (i, 0)),
        out_shape=jax.ShapeDtypeStruct((T, D), F32),
        compiler_params=_cparams(("parallel",)),
        name="combine",
    )(x1, yg, tw, final_norm)


def _routing(top_i):
    T = top_i.shape[0]
    P = T * TOP_K
    tm = MOE_TM
    n_tiles = -(-P // tm) + N_EXPERTS
    e_flat = top_i.reshape(P)
    onehot = (e_flat[:, None] == jnp.arange(N_EXPERTS, dtype=I32)[None, :]).astype(I32)
    csum = jnp.cumsum(onehot, axis=0)
    rank = jnp.sum((csum - 1) * onehot, axis=1)
    counts = csum[-1]
    padded = ((counts + tm - 1) // tm) * tm
    ends = jnp.cumsum(padded)
    starts = ends - padded
    pos = starts[e_flat] + rank
    slot_token = jnp.zeros((n_tiles * tm,), I32).at[pos].set(jnp.arange(P, dtype=I32) // TOP_K)
    tile_start = jnp.arange(n_tiles, dtype=I32) * tm
    tile_valid = (tile_start < ends[-1]).astype(I32)
    tile_expert = jnp.minimum(jnp.sum((tile_start[:, None] >= ends[None, :]).astype(I32), axis=1), N_EXPERTS - 1)
    tile_expert = tile_expert.astype(I32)
    tile_first = jnp.concatenate([jnp.ones((1,), I32), (tile_expert[1:] != tile_expert[:-1]).astype(I32)])
    tile_first = tile_first * tile_valid
    tile_run = jnp.cumsum(tile_first) - 1
    n_runs = jnp.sum(tile_first).reshape(1)
    experts = jnp.arange(N_EXPERTS, dtype=I32)
    later = (counts > 0)[None, :] & (experts[None, :] > tile_expert[:, None])
    nxt = jnp.min(jnp.where(later, experts[None, :], N_EXPERTS), axis=1)
    tile_last = (nxt == N_EXPERTS).astype(I32)
    tile_next = jnp.where(nxt == N_EXPERTS, tile_expert[0], nxt).astype(I32)
    tables = (tile_expert, tile_valid, tile_first, tile_next, tile_last, tile_run.astype(I32), n_runs.astype(I32))
    return pos, slot_token, tables


def kernel(x_prompt, x_sample, state_gdn, state_conv, state_hgrn, lb_table, norm1, w_in, conv_w, A_log, dt_bias,
           gdn_norm, hg_norm, w_out, norm2, w_router, b_router, w_gate_up, b_gate_up, w_down, b_down, final_norm):
    depth = w_in.shape[0]
    assert depth == 1
    B, L, D = x_prompt.shape
    SB = x_sample.shape[0]
    assert x_sample.shape[1] == 1
    nqk = HEADS * HD
    cd = 3 * nqk

    lbs = jnp.cumsum(jax.nn.softmax(lb_table.astype(F32), axis=0), axis=0)
    lb = lbs[0:1]

    assert w_in.shape[2] == MAIN_COLS + 2 * HEADS and cd + nqk == MAIN_COLS // 2
    ba0 = cd + nqk
    w_t = jnp.swapaxes(w_in[0], 0, 1).astype(BF16)
    wo_a = w_out[0, :nqk].astype(BF16)
    wo_b = w_out[0, nqk:].astype(BF16)
    wr_hi, wr_lo = _split(jnp.pad(w_router[0], ((0, 0), (0, LANES - N_EXPERTS))))
    br_p = jnp.pad(b_router[0], (0, LANES - N_EXPERTS))[None, :]
    bg = b_gate_up[0, :, None, 0::2]
    bu = b_gate_up[0, :, None, 1::2]
    bd = b_down[0][:, None, :]
    alog_b = jnp.pad(A_log[0], (HEADS, LANES - 2 * HEADS))[None, :]
    dtb_b = jnp.pad(dt_bias[0], (HEADS, LANES - 2 * HEADS))[None, :]
    alog_row = jnp.pad(A_log[0], (0, LANES - HEADS))[None, :]
    dtb_row = jnp.pad(dt_bias[0], (0, LANES - HEADS))[None, :]
    n1 = norm1[0][None, :]
    n2 = norm2[0][None, :]
    gn = gdn_norm[0][None, :]
    hn = hg_norm[0][None, :]
    fnw = final_norm[None, :]
    cw = conv_w[0]

    xp = x_prompt.reshape(B * L, D)
    xs = x_sample.reshape(SB, D)

    proj_p, ba_p = _in_proj(xp, n1, w_t, ba0, 2 * HEADS, tm=1024, tn=1024)
    proj_s, ba_s = _in_proj(xs, n1, w_t, ba0, 2 * HEADS, tm=SB, tn=1024)

    oa_p, sg_p = _gdn_prompt(proj_p, ba_p, cw, alog_b, dtb_b, gn, B, L)
    ob_p, sh_p = _hgrn_prompt(proj_p, lb, hn, B, L)
    oa_s, ob_s, sg_s, sh_s = _decode(proj_s, ba_s, state_conv[0].reshape(SB, (CONV_K - 1) * cd), state_gdn[0],
                                     state_hgrn[0], cw, alog_row, dtb_row, lb, gn, hn)

    ntok = B * L + SB
    x1_p, h2, ti, tw_p = _out_router(xp, oa_p, ob_p, wo_a, wo_b, n2, wr_hi, wr_lo, br_p,
                                     tm=512, total_rows=ntok, row_off=0)
    x1_s, h2, ti, tw_s = _out_router(xs, oa_s, ob_s, wo_a, wo_b, n2, wr_hi, wr_lo, br_p,
                                     tm=SB, total_rows=ntok, row_off=B * L, into=(h2, ti))

    top_i = ti[:, :TOP_K]
    pos, slot_token, tables = _routing(top_i)
    xs_sorted = h2.at[slot_token].get(mode="promise_in_bounds")
    act = _moe_gate_up(tables, xs_sorted, w_gate_up[0], bg, bu)
    yslots = _moe_down(tables, act, w_down[0], bd)
    pos_kmajor = pos.reshape(B * L + SB, TOP_K).T.reshape(-1)
    yg = yslots.at[pos_kmajor].get(mode="promise_in_bounds").reshape(TOP_K, B * L + SB, D)

    y_p = _combine(x1_p, yg, tw_p, fnw, tm=256, row_off=0)
    y_s = _combine(x1_s, yg, tw_s, fnw, tm=SB, row_off=B * L)

    conv_p = proj_p.reshape(B, L, MAIN_COLS)[:, L - (CONV_K - 1):, :cd]
    conv_s = jnp.concatenate([state_conv[0][:, 1:, :], proj_s[:, None, :cd]], axis=1)
    return (y_p.reshape(B, L, D), y_s.reshape(SB, 1, D),
            sg_p[None], conv_p[None].astype(state_conv.dtype), sh_p[None],
            sg_s[None], conv_s[None].astype(state_conv.dtype), sh_s[None])
```

```python
import functools

import jax
import jax.numpy as jnp
from jax import lax
from jax.experimental import pallas as pl
from jax.experimental.pallas import tpu as pltpu

F32 = jnp.float32
BF16 = jnp.bfloat16
I32 = jnp.int32

EPS = 1e-6
HEADS = 8
HD = 128
CONV_K = 4
N_EXPERTS = 32
TOP_K = 4
SWIGLU_ALPHA = 1.702
SWIGLU_LIMIT = 7.0
LANES = 128
NEG = -1e30

VMEM_LIMIT = 56 * 1024 * 1024

QB, KB, VB, ZB, HQB, HFB, HIB, HGB = (i * HEADS for i in range(8))
MAIN_COLS = 8 * HEADS * HD

HEADS_PER_STEP = 2
GDN_CHUNK = 128
GDN_TB = 1024
HG_CHUNK = 16
HG_TB = 1024
HG_GROUP = 4
DEC_TB = 8
MOE_TM = 256
MOE_TN = 2048
MOE_TNW = 2048
MOE_XSLOTS = 3


def _cparams(sem):
    return pltpu.CompilerParams(dimension_semantics=sem, vmem_limit_bytes=VMEM_LIMIT)


def _mm(a, b):
    return jnp.dot(a.astype(BF16), b.astype(BF16), preferred_element_type=F32)


def _mm_nt(a, b):
    return lax.dot_general(a.astype(BF16), b.astype(BF16), (((1,), (1,)), ((), ())),
                           preferred_element_type=F32)


def _mm_tn(a, b):
    return lax.dot_general(a.astype(BF16), b.astype(BF16), (((0,), (0,)), ((), ())),
                           preferred_element_type=F32)


def _mmh(a, b):
    return jnp.dot(a, b, precision=lax.Precision.HIGHEST, preferred_element_type=F32)


def _split(a):
    hi = a.astype(BF16)
    return hi, (a - hi.astype(F32)).astype(BF16)


def _mm3(a, b):
    d = lambda x, y: jnp.dot(x, y, preferred_element_type=F32)
    return d(a[0], b[0]) + (d(a[0], b[1]) + d(a[1], b[0]))


def _mm_exact_lhs(l_bf16, x):
    d = lambda y: jnp.dot(l_bf16, y, preferred_element_type=F32)
    x0 = x.astype(BF16)
    r1 = x - x0.astype(F32)
    x1 = r1.astype(BF16)
    x2 = (r1 - x1.astype(F32)).astype(BF16)
    return d(x0) + (d(x1) + d(x2))


def _sigmoid(x):
    return 1.0 / (1.0 + jnp.exp(-x))


def _sigmoid_t(x):
    return 0.5 * jnp.tanh(0.5 * x) + 0.5


def _silu(x):
    return x * _sigmoid_t(x)


def _softplus(x):
    return jnp.maximum(x, 0.0) + jnp.log1p(jnp.exp(-jnp.abs(x)))


def _rms(x, w):
    return x * lax.rsqrt(jnp.mean(x * x, axis=-1, keepdims=True) + EPS) * w


def _l2n(x):
    return x * lax.rsqrt(jnp.sum(x * x, axis=-1, keepdims=True) + EPS)


def _in_proj_kernel(x_ref, nw_ref, wt_ref, wbat_ref, o_ref, ba_ref, h_scr):
    nt_dims = (((1,), (1,)), ((), ()))

    @pl.when(pl.program_id(1) == 0)
    def _():
        h = _rms(x_ref[...], nw_ref[...]).astype(BF16)
        h_scr[...] = h
        ba_ref[...] = lax.dot_general(h, wbat_ref[...], nt_dims, preferred_element_type=F32)

    o_ref[...] = lax.dot_general(h_scr[...], wt_ref[...], nt_dims, preferred_element_type=F32)


def _in_proj(x, norm_w, w_t, ba0, nba, tm, tn):
    T, D = x.shape
    N = w_t.shape[0] - nba
    assert ba0 % tn == 0 and nba % 16 == 0
    return pl.pallas_call(
        _in_proj_kernel,
        grid=(T // tm, N // tn),
        in_specs=[
            pl.BlockSpec((tm, D), lambda i, j: (i, 0)),
            pl.BlockSpec((1, D), lambda i, j: (0, 0)),
            pl.BlockSpec((pl.Element(tn), pl.Element(D)),
                         lambda i, j: (pl.multiple_of(j * tn + jnp.where(j * tn >= ba0, nba, 0), 16), 0)),
            pl.BlockSpec((pl.Element(LANES), pl.Element(D)), lambda i, j: (ba0, 0)),
        ],
        out_specs=[
            pl.BlockSpec((tm, tn), lambda i, j: (i, j)),
            pl.BlockSpec((tm, LANES), lambda i, j: (i, 0)),
        ],
        out_shape=[jax.ShapeDtypeStruct((T, N), F32), jax.ShapeDtypeStruct((T, LANES), F32)],
        scratch_shapes=[pltpu.VMEM((tm, D), BF16)],
        compiler_params=_cparams(("parallel", "arbitrary")),
        name="in_proj",
    )(x, norm_w, w_t, w_t)


def _unit_lower_inverse(As, ri, ci):
    n = As[0].shape[0]
    eye = (ri == ci).astype(F32)
    same16 = (ri // 16) == (ci // 16)
    s1 = [_split(jnp.where(same16, -A, 0.0)) for A in As]
    s2 = [_split(_mm3(s, s)) for s in s1]
    s4 = [_split(_mm3(s, s)) for s in s2]
    s8 = [_split(_mm3(s, s)) for s in s4]
    Ts = [eye + jnp.where(same16, -A, 0.0) for A in As]
    for sp in (s2, s4, s8):
        Ts = [T + _mm3(_split(T), s) for T, s in zip(Ts, sp)]
    size = 32
    while size <= n:
        off = ((ri // size) == (ci // size)) & ((ri // (size // 2)) != (ci // (size // 2)))
        bT = [T.astype(BF16) for T in Ts]
        TL = [_mm(bt, jnp.where(off, A, 0.0)) for bt, A in zip(bT, As)]
        Ts = [T - _mm(tl, bt) for T, tl, bt in zip(Ts, TL, bT)]
        size *= 2
    return Ts


def _gdn_prompt_kernel(q_ref, k_ref, v_ref, z_ref, ba_ref, cwq_ref, cwk_ref, cwv_ref, alog_ref, dtb_ref,
                       gn_ref, o_ref, s_ref, ubuf, s_scr, *, tb, chunk, hps):
    hg = pl.program_id(1)
    t = pl.program_id(2)

    @pl.when(t == 0)
    def _():
        ubuf[:, 0:8, :] = jnp.zeros((3, 8, hps * HD), F32)
        s_scr[...] = jnp.zeros((hps, HD, HD), F32)

    ubuf[0, 8:8 + tb, :] = q_ref[...]
    ubuf[1, 8:8 + tb, :] = k_ref[...]
    ubuf[2, 8:8 + tb, :] = v_ref[...]

    ba = ba_ref[...]
    lane = lax.broadcasted_iota(I32, (chunk, LANES), 1)
    beta_all = _sigmoid_t(ba)
    g_all = -jnp.exp(alog_ref[...]) * _softplus(ba + dtb_ref[...])

    ri = lax.broadcasted_iota(I32, (chunk, chunk), 0)
    ci = lax.broadcasted_iota(I32, (chunk, chunk), 1)
    causal = ci <= ri
    strict = ci < ri
    ltri = jnp.where(causal, 1.0, 0.0).astype(BF16)

    def conv(idx, w_ref, r0, c0):
        w = w_ref[:, c0:c0 + HD]
        acc = ubuf[idx, r0 + 5:r0 + 5 + chunk, c0:c0 + HD] * w[0:1, :]
        for j in range(1, CONV_K):
            acc = acc + ubuf[idx, r0 + 5 + j:r0 + 5 + j + chunk, c0:c0 + HD] * w[j:j + 1, :]
        return _silu(acc)

    nc = tb // chunk
    items = [(hh, c) for c in range(nc) for hh in range(hps)]
    qs, ks, vs, betas, gcols, decays, As = [], [], [], [], [], [], []
    gc_all = [_mm_exact_lhs(ltri, g_all[c * chunk:(c + 1) * chunk, :]) for c in range(nc)]
    for hh, c in items:
        r0, c0 = c * chunk, hh * HD
        h = hg * hps + hh
        q = _l2n(conv(0, cwq_ref, r0, c0)) * (HD ** -0.5)
        k = _l2n(conv(1, cwk_ref, r0, c0))
        v = conv(2, cwv_ref, r0, c0)
        beta = jnp.sum(jnp.where(lane == h, beta_all[r0:r0 + chunk, :], 0.0), axis=-1, keepdims=True)
        gcol = jnp.broadcast_to(
            jnp.sum(jnp.where(lane == h + HEADS, gc_all[c], 0.0), axis=-1, keepdims=True), (chunk, HD))
        decay = jnp.exp(jnp.where(causal, gcol - gcol.T, NEG))
        kb = k * beta
        qs.append(q); ks.append(k); vs.append(v); betas.append(beta); gcols.append(gcol); decays.append(decay)
        As.append(jnp.where(strict, _mm_nt(kb, k) * decay, 0.0))
    Ts = _unit_lower_inverse(As, ri, ci)
    us, ws, scs = [], [], []
    for n in range(len(items)):
        sT = _split(Ts[n])
        kb = ks[n] * betas[n]
        us.append(_mm3(sT, _split(vs[n] * betas[n])))
        ws.append(_mm3(sT, _split(kb * jnp.exp(gcols[n]))))
        scs.append(_mm_nt(qs[n], ks[n]) * decays[n])

    S = [s_scr[hh] for hh in range(hps)]
    for n, (hh, c) in enumerate(items):
        r0, c0 = c * chunk, hh * HD
        gcol = gcols[n]
        v_new = us[n] - _mm(ws[n], S[hh])
        o = _mm(qs[n] * jnp.exp(gcol), S[hh]) + _mm(scs[n], v_new)
        glast = gcol[chunk - 1:chunk, :]
        S[hh] = S[hh] * jnp.exp(glast) + _mm_tn(ks[n] * jnp.exp(glast - gcol), v_new)
        o_ref[r0:r0 + chunk, c0:c0 + HD] = (
            _rms(o, gn_ref[...]) * _silu(z_ref[r0:r0 + chunk, c0:c0 + HD])).astype(o_ref.dtype)
    for hh in range(hps):
        s_scr[hh] = S[hh]

    ubuf[:, 0:8, :] = ubuf[:, tb:tb + 8, :]

    @pl.when(t == pl.num_programs(2) - 1)
    def _():
        for hh in range(hps):
            s_ref[0, hh] = S[hh]


def _gdn_prompt(proj, ba, conv_w, alog_b, dtb_b, gdn_norm, B, L):
    tb, chunk, hps = GDN_TB, GDN_CHUNK, HEADS_PER_STEP
    nt = L // tb
    w = hps * HD
    row = lambda b, h, t: b * nt + t
    colspec = lambda base: pl.BlockSpec((tb, w), lambda b, h, t: (row(b, h, t), base // hps + h))
    cwspec = lambda base: pl.BlockSpec((CONV_K, w), lambda b, h, t: (0, base // hps + h))
    hvec = pl.BlockSpec((1, LANES), lambda b, h, t: (0, 0))
    return pl.pallas_call(
        functools.partial(_gdn_prompt_kernel, tb=tb, chunk=chunk, hps=hps),
        grid=(B, HEADS // hps, nt),
        in_specs=[
            colspec(QB), colspec(KB), colspec(VB), colspec(ZB),
            pl.BlockSpec((tb, LANES), lambda b, h, t: (row(b, h, t), 0)),
            cwspec(QB), cwspec(KB), cwspec(VB),
            hvec, hvec,
            pl.BlockSpec((1, HD), lambda b, h, t: (0, 0)),
        ],
        out_specs=[
            pl.BlockSpec((tb, w), lambda b, h, t: (row(b, h, t), h)),
            pl.BlockSpec((1, hps, HD, HD), lambda b, h, t: (b, h, 0, 0)),
        ],
        out_shape=[jax.ShapeDtypeStruct((B * L, HEADS * HD), BF16),
                   jax.ShapeDtypeStruct((B, HEADS, HD, HD), F32)],
        scratch_shapes=[pltpu.VMEM((3, tb + 8, w), F32), pltpu.VMEM((hps, HD, HD), F32)],
        compiler_params=_cparams(("parallel", "parallel", "arbitrary")),
        name="gdn_prompt",
    )(proj, proj, proj, proj, ba, conv_w, conv_w, conv_w, alog_b, dtb_b, gdn_norm)


def _hgrn_prompt_kernel(hq_ref, hf_ref, hi_ref, hg_ref, lb_ref, hn_ref, o_ref, s_ref,
                        g_scr, q_scr, k_scr, st_scr, *, tb, chunk, group, hps):
    t = pl.program_id(2)

    @pl.when(t == 0)
    def _():
        st_scr[...] = jnp.zeros((hps, HD, HD), F32)

    lb = lb_ref[...]
    hf = hf_ref[...]
    forget = lb + (1.0 - lb) * _sigmoid(hf)
    k_scr[...] = (1.0 - lb) * _sigmoid(-hf)
    q_scr[...] = _silu(hq_ref[...])
    ri = lax.broadcasted_iota(I32, (LANES, LANES), 0)
    ci = lax.broadcasted_iota(I32, (LANES, LANES), 1)
    lblk = jnp.where(((ri // chunk) == (ci // chunk)) & (ci <= ri), 1.0, 0.0).astype(BF16)
    lf = jnp.log(forget)
    for r in range(tb // LANES):
        g_scr[r * LANES:(r + 1) * LANES, :] = _mm_exact_lhs(lblk, lf[r * LANES:(r + 1) * LANES, :])

    half = chunk // 2
    row8 = lax.broadcasted_iota(I32, (half, HD), 0)
    lane8 = lax.broadcasted_iota(I32, (half, HD), 1)
    zpad = jnp.zeros((half, HD), F32)
    hn = hn_ref[...]

    def body(gi, sts_in):
        base = gi * (group * chunk)
        items = [(hh, cc) for cc in range(group) for hh in range(hps)]
        ni = range(len(items))
        rc = [(base + cc * chunk, hh * HD) for hh, cc in items]
        vs = [hi_ref[pl.ds(r0, chunk), c0:c0 + HD] for r0, c0 in rc]
        vts = [v.T for v in vs]
        blocks = []
        for r0, c0 in rc:
            for r in (r0, r0 + half):
                blocks.append((g_scr[pl.ds(r, half), c0:c0 + HD], q_scr[pl.ds(r, half), c0:c0 + HD],
                               k_scr[pl.ds(r, half), c0:c0 + HD]))
        accs = [zpad] * len(blocks)
        for j in range(half):
            for n, (Gb, qb, kb) in enumerate(blocks):
                e = jnp.exp(Gb - Gb[j:j + 1, :])
                col = jnp.sum(e * qb * kb[j:j + 1, :], axis=-1, keepdims=True)
                accs[n] = jnp.where(lane8 == (n % 2) * half + j, col, accs[n])
        accs = [jnp.where(row8 >= lane8 - (n % 2) * half, a, 0.0) for n, a in enumerate(accs)]
        Gs = [jnp.concatenate([blocks[2 * n][0], blocks[2 * n + 1][0]], axis=0) for n in ni]
        qs = [jnp.concatenate([blocks[2 * n][1], blocks[2 * n + 1][1]], axis=0) for n in ni]
        ks = [jnp.concatenate([blocks[2 * n][2], blocks[2 * n + 1][2]], axis=0) for n in ni]
        offd = []
        for n in ni:
            Ga, _, ka = blocks[2 * n]
            Gb, qb, _ = blocks[2 * n + 1]
            gmid = Ga[half - 1:half, :]
            kh = jnp.concatenate([ka * jnp.exp(gmid - Ga), zpad], axis=0)
            offd.append(_mm_nt(qb * jnp.exp(Gb - gmid), kh))
        glasts = [Gs[n][chunk - 1:chunk, :] for n in ni]
        incs = [_mm(vts[n], ks[n] * jnp.exp(glasts[n] - Gs[n])) for n in ni]
        intra = []
        for n in ni:
            s = jnp.concatenate([accs[2 * n][:, 0:chunk], accs[2 * n + 1][:, 0:chunk] + offd[n]], axis=0)
            intra.append(_mm(s, vs[n]))
        cur = list(sts_in)
        before = []
        for n, (hh, _) in enumerate(items):
            before.append(cur[hh])
            cur[hh] = cur[hh] * jnp.exp(glasts[n]) + incs[n]
        outs = [_mm_nt(qs[n] * jnp.exp(Gs[n]), before[n]) + intra[n] for n in ni]
        for n, (r0, c0) in enumerate(rc):
            o_ref[pl.ds(r0, chunk), c0:c0 + HD] = (
                _rms(outs[n], hn) * _silu(hg_ref[pl.ds(r0, chunk), c0:c0 + HD])).astype(o_ref.dtype)
        return cur

    sts = [st_scr[hh] for hh in range(hps)]
    for gi in range(tb // (group * chunk)):
        sts = body(gi, sts)
    for hh in range(hps):
        st_scr[hh] = sts[hh]

    @pl.when(t == pl.num_programs(2) - 1)
    def _():
        for hh in range(hps):
            s_ref[0, hh] = sts[hh].T


def _hgrn_prompt(proj, lb, hg_norm, B, L):
    tb, chunk, hps = HG_TB, HG_CHUNK, HEADS_PER_STEP
    nt = L // tb
    w = hps * HD
    row = lambda b, h, t: b * nt + t
    colspec = lambda base: pl.BlockSpec((tb, w), lambda b, h, t: (row(b, h, t), base // hps + h))
    return pl.pallas_call(
        functools.partial(_hgrn_prompt_kernel, tb=tb, chunk=chunk, group=HG_GROUP, hps=hps),
        grid=(B, HEADS // hps, nt),
        in_specs=[
            colspec(HQB), colspec(HFB), colspec(HIB), colspec(HGB),
            pl.BlockSpec((1, w), lambda b, h, t: (0, h)),
            pl.BlockSpec((1, HD), lambda b, h, t: (0, 0)),
        ],
        out_specs=[
            pl.BlockSpec((tb, w), lambda b, h, t: (row(b, h, t), h)),
            pl.BlockSpec((1, hps, HD, HD), lambda b, h, t: (b, h, 0, 0)),
        ],
        out_shape=[jax.ShapeDtypeStruct((B * L, HEADS * HD), BF16),
                   jax.ShapeDtypeStruct((B, HEADS, HD, HD), F32)],
        scratch_shapes=[pltpu.VMEM((tb, w), F32), pltpu.VMEM((tb, w), F32), pltpu.VMEM((tb, w), F32),
                        pltpu.VMEM((hps, HD, HD), F32)],
        compiler_params=_cparams(("parallel", "parallel", "arbitrary")),
        name="hgrn_prompt",
    )(proj, proj, proj, proj, lb, hg_norm)


def _column_broadcasts(x, sel):
    d = lambda y: lax.dot_general(y, sel, (((0,), (0,)), ((), ())), preferred_element_type=F32)
    hi, lo = _split(x)
    return d(hi) + d(lo)


def _decode_kernel(qkv_ref, z_ref, hq_ref, hf_ref, hi_ref, hg_ref, ba_ref, cs_ref, sg_ref, sh_ref,
                   cw_ref, alog_ref, dtb_ref, lb_ref, gn_ref, hn_ref,
                   oa_ref, ob_ref, sgo_ref, sho_ref, *, nb):
    cd = cw_ref.shape[1]
    cw = cw_ref[...]
    cs = cs_ref[...]
    acc = cs[:, 0:cd] * cw[0:1, :]
    acc = acc + cs[:, cd:2 * cd] * cw[1:2, :]
    acc = acc + cs[:, 2 * cd:3 * cd] * cw[2:3, :]
    acc = acc + qkv_ref[...] * cw[3:4, :]
    conv = _silu(acc)
    nqk = HEADS * HD
    ba = ba_ref[...]
    gn = gn_ref[...]
    hn = hn_ref[...]
    z = z_ref[...]
    hgate = hg_ref[...]

    qs, ks, vs = [], [], []
    for h in range(HEADS):
        qs.append(_l2n(conv[:, h * HD:(h + 1) * HD]) * (HD ** -0.5))
        ks.append(_l2n(conv[:, nqk + h * HD:nqk + (h + 1) * HD]))
        vs.append(conv[:, 2 * nqk + h * HD:2 * nqk + (h + 1) * HD])
    lb = lb_ref[...]
    hf = hf_ref[...]
    forget = lb + (1.0 - lb) * _sigmoid(hf)
    hk = (1.0 - lb) * _sigmoid(-hf)
    hq = _silu(hq_ref[...])
    hv = hi_ref[...]

    sel_r = lax.broadcasted_iota(I32, (nb, nb * HD), 0)
    sel_c = lax.broadcasted_iota(I32, (nb, nb * HD), 1)
    sel = jnp.where(sel_c // HD == sel_r, 1.0, 0.0).astype(BF16)

    for h in range(HEADS):
        hs = slice(h * HD, (h + 1) * HD)
        beta = _sigmoid(ba[:, h:h + 1])
        g = -jnp.exp(alog_ref[0:1, h:h + 1]) * _softplus(ba[:, HEADS + h:HEADS + h + 1] + dtb_ref[0:1, h:h + 1])
        eg = jnp.exp(g)
        qk = jnp.sum(qs[h] * ks[h], axis=-1, keepdims=True)
        kB = _column_broadcasts(ks[h], sel)
        qB = _column_broadcasts(qs[h], sel)
        fB = _column_broadcasts(forget[:, hs], sel)
        hkB = _column_broadcasts(hk[:, hs], sel)
        hqB = _column_broadcasts(hq[:, hs], sel)
        for b in range(nb):
            bs = slice(b * HD, (b + 1) * HD)
            S = sg_ref[b, h]
            kS = jnp.sum(S * kB[:, bs], axis=0, keepdims=True)
            qS = jnp.sum(S * qB[:, bs], axis=0, keepdims=True)
            egb = eg[b:b + 1, :]
            v_new = beta[b:b + 1, :] * (vs[h][b:b + 1, :] - egb * kS)
            o = egb * qS + qk[b:b + 1, :] * v_new
            sgo_ref[b, h] = S * egb + kB[:, bs] * v_new
            oa_ref[b:b + 1, hs] = _rms(o, gn) * _silu(z[b:b + 1, hs])
            Sh = sh_ref[b, h] * fB[:, bs] + hkB[:, bs] * hv[b:b + 1, hs]
            sho_ref[b, h] = Sh
            ob = jnp.sum(Sh * hqB[:, bs], axis=0, keepdims=True)
            ob_ref[b:b + 1, hs] = _rms(ob, hn) * _silu(hgate[b:b + 1, hs])


def _decode(proj, ba, conv_state2d, s_gdn, s_hg, conv_w, alog_row, dtb_row, lb, gdn_norm, hg_norm):
    nb = DEC_TB
    T = proj.shape[0]
    cd = conv_w.shape[1]
    w = HEADS * HD
    cblk = lambda width, idx: pl.BlockSpec((nb, width), lambda i: (i, idx))
    sblk = pl.BlockSpec((nb, HEADS, HD, HD), lambda i: (i, 0, 0, 0))
    full = lambda shape: pl.BlockSpec(shape, lambda i: (0,) * len(shape))
    return pl.pallas_call(
        functools.partial(_decode_kernel, nb=nb),
        grid=(T // nb,),
        in_specs=[
            cblk(cd, 0), cblk(w, ZB // HEADS), cblk(w, HQB // HEADS), cblk(w, HFB // HEADS),
            cblk(w, HIB // HEADS), cblk(w, HGB // HEADS),
            cblk(LANES, 0), cblk(3 * cd, 0), sblk, sblk,
            full((CONV_K, cd)), full((1, LANES)), full((1, LANES)), full((1, w)), full((1, HD)), full((1, HD)),
        ],
        out_specs=[cblk(w, 0), cblk(w, 0), sblk, sblk],
        out_shape=[jax.ShapeDtypeStruct((T, w), F32), jax.ShapeDtypeStruct((T, w), F32),
                   jax.ShapeDtypeStruct(s_gdn.shape, F32), jax.ShapeDtypeStruct(s_hg.shape, F32)],
        compiler_params=_cparams(("parallel",)),
        name="decode",
    )(proj, proj, proj, proj, proj, proj, ba, conv_state2d, s_gdn, s_hg,
      conv_w, alog_row, dtb_row, lb, gdn_norm, hg_norm)


def _out_router_kernel(x_ref, oa_ref, ob_ref, woa_ref, wob_ref, n2_ref, wrh_ref, wrl_ref, br_ref,
                       x1_ref, h2_ref, ti_ref, tw_ref, *, nvalid):
    i = pl.program_id(0)

    @pl.when(i >= nvalid)
    def _():
        h2_ref[...] = jnp.zeros(h2_ref.shape, h2_ref.dtype)
        ti_ref[...] = jnp.zeros(ti_ref.shape, ti_ref.dtype)

    @pl.when(i < nvalid)
    def _():
        _out_router_tile(x_ref, oa_ref, ob_ref, woa_ref, wob_ref, n2_ref, wrh_ref, wrl_ref, br_ref,
                         x1_ref, h2_ref, ti_ref, tw_ref)


def _out_router_tile(x_ref, oa_ref, ob_ref, woa_ref, wob_ref, n2_ref, wrh_ref, wrl_ref, br_ref,
                     x1_ref, h2_ref, ti_ref, tw_ref):
    y = x_ref[...] + jnp.dot(oa_ref[...].astype(BF16), woa_ref[...], preferred_element_type=F32)
    y = y + jnp.dot(ob_ref[...].astype(BF16), wob_ref[...], preferred_element_type=F32)
    x1_ref[...] = y
    h2 = _rms(y, n2_ref[...])
    h2_ref[...] = h2
    logits = _mm3(_split(h2), (wrh_ref[...], wrl_ref[...])) + br_ref[...]
    lane = lax.broadcasted_iota(I32, logits.shape, 1)
    logits = jnp.where(lane < N_EXPERTS, logits, NEG)
    ti = jnp.zeros(logits.shape, I32)
    tw = jnp.zeros(logits.shape, F32)
    m0 = None
    for kk in range(TOP_K):
        m = jnp.max(logits, axis=-1, keepdims=True)
        idx = jnp.min(jnp.where(logits == m, lane, LANES), axis=-1, keepdims=True)
        if m0 is None:
            m0 = m
        ti = jnp.where(lane == kk, idx, ti)
        tw = jnp.where(lane == kk, jnp.exp(m - m0), tw)
        logits = jnp.where(lane == idx, NEG * 2.0, logits)
    tw_ref[...] = tw / jnp.sum(tw, axis=-1, keepdims=True)
    ti_ref[...] = ti


def _out_router_into_kernel(h2_all_ref, ti_all_ref, *refs, nvalid):
    del h2_all_ref, ti_all_ref
    _out_router_kernel(*refs, nvalid=nvalid)


def _out_router(x, oa, ob, wo_a, wo_b, norm2, wr_hi, wr_lo, b_router_p, tm, total_rows, row_off, into=None):
    T, D = x.shape
    w = oa.shape[1]
    off = row_off // tm
    nvalid = T // tm
    steps = nvalid if into is not None else -(-total_rows // tm)
    rowblk = lambda width: pl.BlockSpec((tm, width), lambda i: (jnp.minimum(i, nvalid - 1), 0))
    allblk = lambda width: pl.BlockSpec((tm, width), lambda i: (i + off, 0))
    full = lambda shape: pl.BlockSpec(shape, lambda i: (0,) * len(shape))
    in_specs = [rowblk(D), rowblk(w), rowblk(w), full((w, D)), full((w, D)), full((1, D)),
                full((D, LANES)), full((D, LANES)), full((1, LANES))]
    args = (x, oa, ob, wo_a, wo_b, norm2, wr_hi, wr_lo, b_router_p)
    body, aliases = _out_router_kernel, {}
    if into is not None:
        in_specs = [pl.BlockSpec(memory_space=pl.ANY)] * 2 + in_specs
        args = tuple(into) + args
        body, aliases = _out_router_into_kernel, {0: 1, 1: 2}
    return pl.pallas_call(
        functools.partial(body, nvalid=nvalid),
        grid=(steps,),
        in_specs=in_specs,
        out_specs=[rowblk(D), allblk(D), allblk(LANES), rowblk(LANES)],
        out_shape=[jax.ShapeDtypeStruct((T, D), F32), jax.ShapeDtypeStruct((total_rows, D), F32),
                   jax.ShapeDtypeStruct((total_rows, LANES), I32), jax.ShapeDtypeStruct((T, LANES), F32)],
        input_output_aliases=aliases,
        compiler_params=_cparams(("arbitrary",)),
        name="out_router",
    )(*args)


def _weight_ring_step(rt, w_hbm, wbuf, sem, tcols):
    te_ref, tv_ref, tf_ref, tnx_ref, tlast_ref, trun_ref, nr_ref = rt
    n = pl.program_id(0)
    m = pl.program_id(1)
    slot = lax.rem(n * nr_ref[0] + trun_ref[m], 2)

    def copy(e, nn, s):
        c0 = pl.multiple_of(nn * tcols, tcols)
        return pltpu.make_async_copy(w_hbm.at[e, :, pl.ds(c0, tcols)], wbuf.at[s], sem.at[s])

    @pl.when((n == 0) & (m == 0))
    def _():
        copy(te_ref[m], n, slot).start()

    copy(te_ref[m], n, slot).wait()
    last = tlast_ref[m] != 0

    @pl.when(jnp.logical_or(jnp.logical_not(last), n + 1 < pl.num_programs(0)))
    def _():
        copy(tnx_ref[m], jnp.where(last, n + 1, n), 1 - slot).start()

    return slot


def _tile_ring_step(x_hbm, xbuf, xsem):
    n = pl.program_id(0)
    m = pl.program_id(1)
    n_m = pl.num_programs(1)
    total = pl.num_programs(0) * n_m
    tm = xbuf.shape[1]
    nslot = xbuf.shape[0]
    s = n * n_m + m

    def copy(step):
        r0 = pl.multiple_of(lax.rem(step, n_m) * tm, tm)
        slot = lax.rem(step, nslot)
        return pltpu.make_async_copy(x_hbm.at[pl.ds(r0, tm), :], xbuf.at[slot], xsem.at[slot])

    @pl.when(s == 0)
    def _():
        for first in range(nslot - 1):
            copy(first).start()

    @pl.when(s + nslot - 1 < total)
    def _():
        copy(s + nslot - 1).start()

    copy(s).wait()
    return lax.rem(s, nslot)


def _moe_gate_up_kernel(te_ref, tv_ref, tf_ref, tnx_ref, tlast_ref, trun_ref, nr_ref,
                        x_hbm, w_hbm, bg_ref, bu_ref, act_ref, wbuf, wc_scr, sem, xbuf, xsem):
    rt = (te_ref, tv_ref, tf_ref, tnx_ref, tlast_ref, trun_ref, nr_ref)
    m = pl.program_id(1)
    tnw = wbuf.shape[2]
    xslot = _tile_ring_step(x_hbm, xbuf, xsem)
    grp = 2 * LANES
    ngrp = tnw // grp

    @pl.when(tv_ref[m] != 0)
    def _():
        @pl.when(tf_ref[m] != 0)
        def _():
            slot = _weight_ring_step(rt, w_hbm, wbuf, sem, tnw)
            src = lax.broadcasted_iota(I32, (grp, grp), 0)
            dst = lax.broadcasted_iota(I32, (grp, grp), 1)
            want = jnp.where(dst < LANES, 2 * dst, 2 * (dst - LANES) + 1)
            perm = jnp.where(src == want, 1.0, 0.0).astype(BF16)
            for g in range(ngrp):
                wt = wbuf[slot, :, g * grp:(g + 1) * grp].astype(BF16)
                wc_scr[:, g * grp:(g + 1) * grp] = jnp.dot(wt, perm, preferred_element_type=F32).astype(BF16)

        x = xbuf[xslot].astype(BF16)
        bg = bg_ref[0]
        bu = bu_ref[0]
        for g in range(ngrp):
            gu = jnp.dot(x, wc_scr[:, g * grp:(g + 1) * grp], preferred_element_type=F32)
            gate = jnp.minimum(gu[:, 0:LANES] + bg[:, g * LANES:(g + 1) * LANES], SWIGLU_LIMIT)
            up = jnp.clip(gu[:, LANES:grp] + bu[:, g * LANES:(g + 1) * LANES], -SWIGLU_LIMIT, SWIGLU_LIMIT)
            act_ref[:, g * LANES:(g + 1) * LANES] = (
                (up + 1.0) * (gate * _sigmoid_t(gate * SWIGLU_ALPHA))).astype(act_ref.dtype)

    @pl.when(tv_ref[m] == 0)
    def _():
        act_ref[...] = jnp.zeros(act_ref.shape, act_ref.dtype)


def _moe_gate_up(rt, xs, w_gate_up, bg, bu):
    tm, tnw = MOE_TM, MOE_TNW
    P, D = xs.shape
    F2 = w_gate_up.shape[2]
    imap_b = lambda n, m, te, *_: (te[m], 0, n)
    return pl.pallas_call(
        _moe_gate_up_kernel,
        grid_spec=pltpu.PrefetchScalarGridSpec(
            num_scalar_prefetch=len(rt),
            grid=(F2 // tnw, P // tm),
            in_specs=[
                pl.BlockSpec(memory_space=pl.ANY),
                pl.BlockSpec(memory_space=pl.ANY),
                pl.BlockSpec((1, 1, tnw // 2), imap_b),
                pl.BlockSpec((1, 1, tnw // 2), imap_b),
            ],
            out_specs=pl.BlockSpec((tm, tnw // 2), lambda n, m, *_: (m, n)),
            scratch_shapes=[pltpu.VMEM((2, D, tnw), F32), pltpu.VMEM((D, tnw), BF16),
                            pltpu.SemaphoreType.DMA((2,)),
                            pltpu.VMEM((MOE_XSLOTS, tm, D), xs.dtype), pltpu.SemaphoreType.DMA((MOE_XSLOTS,))],
        ),
        out_shape=jax.ShapeDtypeStruct((P, F2 // 2), BF16),
        compiler_params=_cparams(("arbitrary", "arbitrary")),
        name="moe_gate_up",
    )(*rt, xs, w_gate_up, bg, bu)


def _moe_down_kernel(te_ref, tv_ref, tf_ref, tnx_ref, tlast_ref, trun_ref, nr_ref,
                     a_hbm, w_hbm, bd_ref, y_ref, wbuf, wc_scr, sem, abuf, asem):
    rt = (te_ref, tv_ref, tf_ref, tnx_ref, tlast_ref, trun_ref, nr_ref)
    m = pl.program_id(1)
    aslot = _tile_ring_step(a_hbm, abuf, asem)

    @pl.when(tv_ref[m] != 0)
    def _():
        @pl.when(tf_ref[m] != 0)
        def _():
            slot = _weight_ring_step(rt, w_hbm, wbuf, sem, wbuf.shape[2])
            wc_scr[...] = wbuf[slot].astype(BF16)

        y_ref[...] = jnp.dot(abuf[aslot], wc_scr[...], preferred_element_type=F32) + bd_ref[0]

    @pl.when(tv_ref[m] == 0)
    def _():
        y_ref[...] = jnp.zeros(y_ref.shape, y_ref.dtype)


def _moe_down(rt, act, w_down, bd):
    tm, tn = MOE_TM, MOE_TN
    P, F = act.shape
    D = w_down.shape[2]
    return pl.pallas_call(
        _moe_down_kernel,
        grid_spec=pltpu.PrefetchScalarGridSpec(
            num_scalar_prefetch=len(rt),
            grid=(D // tn, P // tm),
            in_specs=[
                pl.BlockSpec(memory_space=pl.ANY),
                pl.BlockSpec(memory_space=pl.ANY),
                pl.BlockSpec((1, 1, tn), lambda n, m, te, *_: (te[m], 0, n)),
            ],
            out_specs=pl.BlockSpec((tm, tn), lambda n, m, *_: (m, n)),
            scratch_shapes=[pltpu.VMEM((2, F, tn), F32), pltpu.VMEM((F, tn), BF16),
                            pltpu.SemaphoreType.DMA((2,)),
                            pltpu.VMEM((MOE_XSLOTS, tm, F), act.dtype), pltpu.SemaphoreType.DMA((MOE_XSLOTS,))],
        ),
        out_shape=jax.ShapeDtypeStruct((P, D), F32),
        compiler_params=_cparams(("arbitrary", "arbitrary")),
        name="moe_down",
    )(*rt, act, w_down, bd)


def _combine_kernel(x1_ref, yg_ref, tw_ref, fn_ref, y_ref):
    tw = tw_ref[...]
    moe = tw[:, 0:1] * yg_ref[0]
    for kk in range(1, TOP_K):
        moe = moe + tw[:, kk:kk + 1] * yg_ref[kk]
    y_ref[...] = _rms(x1_ref[...] + moe, fn_ref[...])


def _combine(x1, yg, tw, final_norm, tm, row_off):
    T, D = x1.shape
    off = row_off // tm
    return pl.pallas_call(
        _combine_kernel,
        grid=(T // tm,),
        in_specs=[
            pl.BlockSpec((tm, D), lambda i: (i, 0)),
            pl.BlockSpec((TOP_K, tm, D), lambda i: (0, i + off, 0)),
            pl.BlockSpec((tm, LANES), lambda i: (i, 0)),
            pl.BlockSpec((1, D), lambda i: (0, 0)),
        ],
        out_specs=pl.BlockSpec((tm, D), lambda i: (i, 0)),
        out_shape=jax.ShapeDtypeStruct((T, D), F32),
        compiler_params=_cparams(("parallel",)),
        name="combine",
    )(x1, yg, tw, final_norm)


def _routing(top_i):
    T = top_i.shape[0]
    P = T * TOP_K
    tm = MOE_TM
    n_tiles = -(-P // tm) + N_EXPERTS
    e_flat = top_i.reshape(P)
    onehot = (e_flat[:, None] == jnp.arange(N_EXPERTS, dtype=I32)[None, :]).astype(I32)
    csum = jnp.cumsum(onehot, axis=0)
    rank = jnp.sum((csum - 1) * onehot, axis=1)
    counts = csum[-1]
    padded = ((counts + tm - 1) // tm) * tm
    ends = jnp.cumsum(padded)
    starts = ends - padded
    pos = starts[e_flat] + rank
    slot_token = jnp.zeros((n_tiles * tm,), I32).at[pos].set(jnp.arange(P, dtype=I32) // TOP_K)
    tile_start = jnp.arange(n_tiles, dtype=I32) * tm
    tile_valid = (tile_start < ends[-1]).astype(I32)
    tile_expert = jnp.minimum(jnp.sum((tile_start[:, None] >= ends[None, :]).astype(I32), axis=1), N_EXPERTS - 1)
    tile_expert = tile_expert.astype(I32)
    tile_first = jnp.concatenate([jnp.ones((1,), I32), (tile_expert[1:] != tile_expert[:-1]).astype(I32)])
    tile_first = tile_first * tile_valid
    tile_run = jnp.cumsum(tile_first) - 1
    n_runs = jnp.sum(tile_first).reshape(1)
    experts = jnp.arange(N_EXPERTS, dtype=I32)
    later = (counts > 0)[None, :] & (experts[None, :] > tile_expert[:, None])
    nxt = jnp.min(jnp.where(later, experts[None, :], N_EXPERTS), axis=1)
    tile_last = (nxt == N_EXPERTS).astype(I32)
    tile_next = jnp.where(nxt == N_EXPERTS, tile_expert[0], nxt).astype(I32)
    tables = (tile_expert, tile_valid, tile_first, tile_next, tile_last, tile_run.astype(I32), n_runs.astype(I32))
    return pos, slot_token, tables


def kernel(x_prompt, x_sample, state_gdn, state_conv, state_hgrn, lb_table, norm1, w_in, conv_w, A_log, dt_bias,
           gdn_norm, hg_norm, w_out, norm2, w_router, b_router, w_gate_up, b_gate_up, w_down, b_down, final_norm):
    depth = w_in.shape[0]
    assert depth == 1
    B, L, D = x_prompt.shape
    SB = x_sample.shape[0]
    assert x_sample.shape[1] == 1
    nqk = HEADS * HD
    cd = 3 * nqk

    lbs = jnp.cumsum(jax.nn.softmax(lb_table.astype(F32), axis=0), axis=0)
    lb = lbs[0:1]

    assert w_in.shape[2] == MAIN_COLS + 2 * HEADS and cd + nqk == MAIN_COLS // 2
    ba0 = cd + nqk
    w_t = jnp.swapaxes(w_in[0], 0, 1).astype(BF16)
    wo_a = w_out[0, :nqk].astype(BF16)
    wo_b = w_out[0, nqk:].astype(BF16)
    wr_hi, wr_lo = _split(jnp.pad(w_router[0], ((0, 0), (0, LANES - N_EXPERTS))))
    br_p = jnp.pad(b_router[0], (0, LANES - N_EXPERTS))[None, :]
    bg = b_gate_up[0, :, None, 0::2]
    bu = b_gate_up[0, :, None, 1::2]
    bd = b_down[0][:, None, :]
    alog_b = jnp.pad(A_log[0], (HEADS, LANES - 2 * HEADS))[None, :]
    dtb_b = jnp.pad(dt_bias[0], (HEADS, LANES - 2 * HEADS))[None, :]
    alog_row = jnp.pad(A_log[0], (0, LANES - HEADS))[None, :]
    dtb_row = jnp.pad(dt_bias[0], (0, LANES - HEADS))[None, :]
    n1 = norm1[0][None, :]
    n2 = norm2[0][None, :]
    gn = gdn_norm[0][None, :]
    hn = hg_norm[0][None, :]
    fnw = final_norm[None, :]
    cw = conv_w[0]

    xp = x_prompt.reshape(B * L, D)
    xs = x_sample.reshape(SB, D)

    proj_p, ba_p = _in_proj(xp, n1, w_t, ba0, 2 * HEADS, tm=512, tn=2048)
    proj_s, ba_s = _in_proj(xs, n1, w_t, ba0, 2 * HEADS, tm=SB, tn=2048)

    oa_p, sg_p = _gdn_prompt(proj_p, ba_p, cw, alog_b, dtb_b, gn, B, L)
    ob_p, sh_p = _hgrn_prompt(proj_p, lb, hn, B, L)
    oa_s, ob_s, sg_s, sh_s = _decode(proj_s, ba_s, state_conv[0].reshape(SB, (CONV_K - 1) * cd), state_gdn[0],
                                     state_hgrn[0], cw, alog_row, dtb_row, lb, gn, hn)

    ntok = B * L + SB
    x1_p, h2, ti, tw_p = _out_router(xp, oa_p, ob_p, wo_a, wo_b, n2, wr_hi, wr_lo, br_p,
                                     tm=512, total_rows=ntok, row_off=0)
    x1_s, h2, ti, tw_s = _out_router(xs, oa_s, ob_s, wo_a, wo_b, n2, wr_hi, wr_lo, br_p,
                                     tm=SB, total_rows=ntok, row_off=B * L, into=(h2, ti))

    top_i = ti[:, :TOP_K]
    pos, slot_token, tables = _routing(top_i)
    xs_sorted = h2.at[slot_token].get(mode="promise_in_bounds")
    act = _moe_gate_up(tables, xs_sorted, w_gate_up[0], bg, bu)
    yslots = _moe_down(tables, act, w_down[0], bd)
    pos_kmajor = pos.reshape(B * L + SB, TOP_K).T.reshape(-1)
    yg = yslots.at[pos_kmajor].get(mode="promise_in_bounds").reshape(TOP_K, B * L + SB, D)

    y_p = _combine(x1_p, yg, tw_p, fnw, tm=256, row_off=0)
    y_s = _combine(x1_s, yg, tw_s, fnw, tm=SB, row_off=B * L)

    conv_p = proj_p.reshape(B, L, MAIN_COLS)[:, L - (CONV_K - 1):, :cd]
    conv_s = jnp.concatenate([state_conv[0][:, 1:, :], proj_s[:, None, :cd]], axis=1)
    return (y_p.reshape(B, L, D), y_s.reshape(SB, 1, D),
            sg_p[None], conv_p[None].astype(state_conv.dtype), sh_p[None],
            sg_s[None], conv_s[None].astype(state_conv.dtype), sh_s[None])
```

```python
import functools

import jax
import jax.numpy as jnp
from jax import lax
from jax.experimental import pallas as pl
from jax.experimental.pallas import tpu as pltpu

F32 = jnp.float32
BF16 = jnp.bfloat16
I32 = jnp.int32

EPS = 1e-6
HEADS = 8
HD = 128
CONV_K = 4
N_EXPERTS = 32
TOP_K = 4
SWIGLU_ALPHA = 1.702
SWIGLU_LIMIT = 7.0
LANES = 128
NEG = -1e30

VMEM_LIMIT = 56 * 1024 * 1024

QB, KB, VB, ZB, HQB, HFB, HIB, HGB = (i * HEADS for i in range(8))
MAIN_COLS = 8 * HEADS * HD

HEADS_PER_STEP = 2
GDN_CHUNK = 128
GDN_TB = 1024
HG_CHUNK = 16
HG_TB = 1024
HG_GROUP = 4
DEC_TB = 8
MOE_TM = 256
MOE_TN = 2048
MOE_TNW = 2048
MOE_XSLOTS = 3


def _cparams(sem):
    return pltpu.CompilerParams(dimension_semantics=sem, vmem_limit_bytes=VMEM_LIMIT)


def _mm(a, b):
    return jnp.dot(a.astype(BF16), b.astype(BF16), preferred_element_type=F32)


def _mm_nt(a, b):
    return lax.dot_general(a.astype(BF16), b.astype(BF16), (((1,), (1,)), ((), ())),
                           preferred_element_type=F32)


def _mm_tn(a, b):
    return lax.dot_general(a.astype(BF16), b.astype(BF16), (((0,), (0,)), ((), ())),
                           preferred_element_type=F32)


def _mmh(a, b):
    return jnp.dot(a, b, precision=lax.Precision.HIGHEST, preferred_element_type=F32)


def _split(a):
    hi = a.astype(BF16)
    return hi, (a - hi.astype(F32)).astype(BF16)


def _mm3(a, b):
    d = lambda x, y: jnp.dot(x, y, preferred_element_type=F32)
    return d(a[0], b[0]) + (d(a[0], b[1]) + d(a[1], b[0]))


def _mm_exact_lhs(l_bf16, x):
    d = lambda y: jnp.dot(l_bf16, y, preferred_element_type=F32)
    x0 = x.astype(BF16)
    r1 = x - x0.astype(F32)
    x1 = r1.astype(BF16)
    x2 = (r1 - x1.astype(F32)).astype(BF16)
    return d(x0) + (d(x1) + d(x2))


def _sigmoid(x):
    return 1.0 / (1.0 + jnp.exp(-x))


def _sigmoid_t(x):
    return 0.5 * jnp.tanh(0.5 * x) + 0.5


def _silu(x):
    return x * _sigmoid_t(x)


def _softplus(x):
    return jnp.maximum(x, 0.0) + jnp.log1p(jnp.exp(-jnp.abs(x)))


def _rms(x, w):
    return x * lax.rsqrt(jnp.mean(x * x, axis=-1, keepdims=True) + EPS) * w


def _l2n(x):
    return x * lax.rsqrt(jnp.sum(x * x, axis=-1, keepdims=True) + EPS)


def _in_proj_kernel(x_ref, nw_ref, wt_ref, wbat_ref, o_ref, ba_ref, h_scr):
    nt_dims = (((1,), (1,)), ((), ()))

    @pl.when(pl.program_id(1) == 0)
    def _():
        h = _rms(x_ref[...], nw_ref[...]).astype(BF16)
        h_scr[...] = h
        ba_ref[...] = lax.dot_general(h, wbat_ref[...], nt_dims, preferred_element_type=F32)

    o_ref[...] = lax.dot_general(h_scr[...], wt_ref[...], nt_dims, preferred_element_type=F32)


def _in_proj(x, norm_w, w_t, ba0, nba, tm, tn):
    T, D = x.shape
    N = w_t.shape[0] - nba
    assert ba0 % tn == 0 and nba % 16 == 0
    return pl.pallas_call(
        _in_proj_kernel,
        grid=(T // tm, N // tn),
        in_specs=[
            pl.BlockSpec((tm, D), lambda i, j: (i, 0)),
            pl.BlockSpec((1, D), lambda i, j: (0, 0)),
            pl.BlockSpec((pl.Element(tn), pl.Element(D)),
                         lambda i, j: (pl.multiple_of(j * tn + jnp.where(j * tn >= ba0, nba, 0), 16), 0)),
            pl.BlockSpec((pl.Element(LANES), pl.Element(D)), lambda i, j: (ba0, 0)),
        ],
        out_specs=[
            pl.BlockSpec((tm, tn), lambda i, j: (i, j)),
            pl.BlockSpec((tm, LANES), lambda i, j: (i, 0)),
        ],
        out_shape=[jax.ShapeDtypeStruct((T, N), F32), jax.ShapeDtypeStruct((T, LANES), F32)],
        scratch_shapes=[pltpu.VMEM((tm, D), BF16)],
        compiler_params=_cparams(("parallel", "arbitrary")),
        name="in_proj",
    )(x, norm_w, w_t, w_t)


def _unit_lower_inverse(As, ri, ci):
    n = As[0].shape[0]
    eye = (ri == ci).astype(F32)
    same16 = (ri // 16) == (ci // 16)
    s1 = [_split(jnp.where(same16, -A, 0.0)) for A in As]
    s2 = [_split(_mm3(s, s)) for s in s1]
    s4 = [_split(_mm3(s, s)) for s in s2]
    s8 = [_split(_mm3(s, s)) for s in s4]
    Ts = [eye + jnp.where(same16, -A, 0.0) for A in As]
    for sp in (s2, s4, s8):
        Ts = [T + _mm3(_split(T), s) for T, s in zip(Ts, sp)]
    size = 32
    while size <= n:
        off = ((ri // size) == (ci // size)) & ((ri // (size // 2)) != (ci // (size // 2)))
        bT = [T.astype(BF16) for T in Ts]
        TL = [_mm(bt, jnp.where(off, A, 0.0)) for bt, A in zip(bT, As)]
        Ts = [T - _mm(tl, bt) for T, tl, bt in zip(Ts, TL, bT)]
        size *= 2
    return Ts


def _gdn_prompt_kernel(q_ref, k_ref, v_ref, z_ref, ba_ref, cwq_ref, cwk_ref, cwv_ref, alog_ref, dtb_ref,
                       gn_ref, o_ref, s_ref, ubuf, s_scr, *, tb, chunk, hps):
    hg = pl.program_id(1)
    t = pl.program_id(2)

    @pl.when(t == 0)
    def _():
        ubuf[:, 0:8, :] = jnp.zeros((3, 8, hps * HD), F32)
        s_scr[...] = jnp.zeros((hps, HD, HD), F32)

    ubuf[0, 8:8 + tb, :] = q_ref[...]
    ubuf[1, 8:8 + tb, :] = k_ref[...]
    ubuf[2, 8:8 + tb, :] = v_ref[...]

    ba = ba_ref[...]
    lane = lax.broadcasted_iota(I32, (chunk, LANES), 1)
    beta_all = _sigmoid_t(ba)
    g_all = -jnp.exp(alog_ref[...]) * _softplus(ba + dtb_ref[...])

    ri = lax.broadcasted_iota(I32, (chunk, chunk), 0)
    ci = lax.broadcasted_iota(I32, (chunk, chunk), 1)
    causal = ci <= ri
    strict = ci < ri
    ltri = jnp.where(causal, 1.0, 0.0).astype(BF16)

    def conv(idx, w_ref, r0, c0):
        w = w_ref[:, c0:c0 + HD]
        acc = ubuf[idx, r0 + 5:r0 + 5 + chunk, c0:c0 + HD] * w[0:1, :]
        for j in range(1, CONV_K):
            acc = acc + ubuf[idx, r0 + 5 + j:r0 + 5 + j + chunk, c0:c0 + HD] * w[j:j + 1, :]
        return _silu(acc)

    nc = tb // chunk
    items = [(hh, c) for c in range(nc) for hh in range(hps)]
    qs, ks, vs, betas, gcols, decays, As = [], [], [], [], [], [], []
    gc_all = [_mm_exact_lhs(ltri, g_all[c * chunk:(c + 1) * chunk, :]) for c in range(nc)]
    for hh, c in items:
        r0, c0 = c * chunk, hh * HD
        h = hg * hps + hh
        q = _l2n(conv(0, cwq_ref, r0, c0)) * (HD ** -0.5)
        k = _l2n(conv(1, cwk_ref, r0, c0))
        v = conv(2, cwv_ref, r0, c0)
        beta = jnp.sum(jnp.where(lane == h, beta_all[r0:r0 + chunk, :], 0.0), axis=-1, keepdims=True)
        gcol = jnp.broadcast_to(
            jnp.sum(jnp.where(lane == h + HEADS, gc_all[c], 0.0), axis=-1, keepdims=True), (chunk, HD))
        decay = jnp.exp(jnp.where(causal, gcol - gcol.T, NEG))
        kb = k * beta
        qs.append(q); ks.append(k); vs.append(v); betas.append(beta); gcols.append(gcol); decays.append(decay)
        As.append(jnp.where(strict, _mm_nt(kb, k) * decay, 0.0))
    Ts = _unit_lower_inverse(As, ri, ci)
    us, ws, scs = [], [], []
    for n in range(len(items)):
        sT = _split(Ts[n])
        kb = ks[n] * betas[n]
        us.append(_mm3(sT, _split(vs[n] * betas[n])))
        ws.append(_mm3(sT, _split(kb * jnp.exp(gcols[n]))))
        scs.append(_mm_nt(qs[n], ks[n]) * decays[n])

    S = [s_scr[hh] for hh in range(hps)]
    for n, (hh, c) in enumerate(items):
        r0, c0 = c * chunk, hh * HD
        gcol = gcols[n]
        v_new = us[n] - _mm(ws[n], S[hh])
        o = _mm(qs[n] * jnp.exp(gcol), S[hh]) + _mm(scs[n], v_new)
        glast = gcol[chunk - 1:chunk, :]
        S[hh] = S[hh] * jnp.exp(glast) + _mm_tn(ks[n] * jnp.exp(glast - gcol), v_new)
        o_ref[r0:r0 + chunk, c0:c0 + HD] = (
            _rms(o, gn_ref[...]) * _silu(z_ref[r0:r0 + chunk, c0:c0 + HD])).astype(o_ref.dtype)
    for hh in range(hps):
        s_scr[hh] = S[hh]

    ubuf[:, 0:8, :] = ubuf[:, tb:tb + 8, :]

    @pl.when(t == pl.num_programs(2) - 1)
    def _():
        for hh in range(hps):
            s_ref[0, hh] = S[hh]


def _gdn_prompt(proj, ba, conv_w, alog_b, dtb_b, gdn_norm, B, L):
    tb, chunk, hps = GDN_TB, GDN_CHUNK, HEADS_PER_STEP
    nt = L // tb
    w = hps * HD
    row = lambda b, h, t: b * nt + t
    colspec = lambda base: pl.BlockSpec((tb, w), lambda b, h, t: (row(b, h, t), base // hps + h))
    cwspec = lambda base: pl.BlockSpec((CONV_K, w), lambda b, h, t: (0, base // hps + h))
    hvec = pl.BlockSpec((1, LANES), lambda b, h, t: (0, 0))
    return pl.pallas_call(
        functools.partial(_gdn_prompt_kernel, tb=tb, chunk=chunk, hps=hps),
        grid=(B, HEADS // hps, nt),
        in_specs=[
            colspec(QB), colspec(KB), colspec(VB), colspec(ZB),
            pl.BlockSpec((tb, LANES), lambda b, h, t: (row(b, h, t), 0)),
            cwspec(QB), cwspec(KB), cwspec(VB),
            hvec, hvec,
            pl.BlockSpec((1, HD), lambda b, h, t: (0, 0)),
        ],
        out_specs=[
            pl.BlockSpec((tb, w), lambda b, h, t: (row(b, h, t), h)),
            pl.BlockSpec((1, hps, HD, HD), lambda b, h, t: (b, h, 0, 0)),
        ],
        out_shape=[jax.ShapeDtypeStruct((B * L, HEADS * HD), BF16),
                   jax.ShapeDtypeStruct((B, HEADS, HD, HD), F32)],
        scratch_shapes=[pltpu.VMEM((3, tb + 8, w), F32), pltpu.VMEM((hps, HD, HD), F32)],
        compiler_params=_cparams(("parallel", "parallel", "arbitrary")),
        name="gdn_prompt",
    )(proj, proj, proj, proj, ba, conv_w, conv_w, conv_w, alog_b, dtb_b, gdn_norm)


def _hgrn_prompt_kernel(hq_ref, hf_ref, hi_ref, hg_ref, lb_ref, hn_ref, o_ref, s_ref,
                        g_scr, q_scr, k_scr, st_scr, *, tb, chunk, group, hps):
    t = pl.program_id(2)

    @pl.when(t == 0)
    def _():
        st_scr[...] = jnp.zeros((hps, HD, HD), F32)

    lb = lb_ref[...]
    hf = hf_ref[...]
    forget = lb + (1.0 - lb) * _sigmoid(hf)
    k_scr[...] = (1.0 - lb) * _sigmoid(-hf)
    q_scr[...] = _silu(hq_ref[...])
    ri = lax.broadcasted_iota(I32, (LANES, LANES), 0)
    ci = lax.broadcasted_iota(I32, (LANES, LANES), 1)
    lblk = jnp.where(((ri // chunk) == (ci // chunk)) & (ci <= ri), 1.0, 0.0).astype(BF16)
    lf = jnp.log(forget)
    for r in range(tb // LANES):
        g_scr[r * LANES:(r + 1) * LANES, :] = _mm_exact_lhs(lblk, lf[r * LANES:(r + 1) * LANES, :])

    half = chunk // 2
    row8 = lax.broadcasted_iota(I32, (half, HD), 0)
    lane8 = lax.broadcasted_iota(I32, (half, HD), 1)
    zpad = jnp.zeros((half, HD), F32)
    hn = hn_ref[...]

    def body(gi, sts_in):
        base = gi * (group * chunk)
        items = [(hh, cc) for cc in range(group) for hh in range(hps)]
        ni = range(len(items))
        rc = [(base + cc * chunk, hh * HD) for hh, cc in items]
        vs = [hi_ref[pl.ds(r0, chunk), c0:c0 + HD] for r0, c0 in rc]
        vts = [v.T for v in vs]
        blocks = []
        for r0, c0 in rc:
            for r in (r0, r0 + half):
                blocks.append((g_scr[pl.ds(r, half), c0:c0 + HD], q_scr[pl.ds(r, half), c0:c0 + HD],
                               k_scr[pl.ds(r, half), c0:c0 + HD]))
        accs = [zpad] * len(blocks)
        for j in range(half):
            for n, (Gb, qb, kb) in enumerate(blocks):
                e = jnp.exp(Gb - Gb[j:j + 1, :])
                col = jnp.sum(e * qb * kb[j:j + 1, :], axis=-1, keepdims=True)
                accs[n] = jnp.where(lane8 == (n % 2) * half + j, col, accs[n])
        accs = [jnp.where(row8 >= lane8 - (n % 2) * half, a, 0.0) for n, a in enumerate(accs)]
        Gs = [jnp.concatenate([blocks[2 * n][0], blocks[2 * n + 1][0]], axis=0) for n in ni]
        qs = [jnp.concatenate([blocks[2 * n][1], blocks[2 * n + 1][1]], axis=0) for n in ni]
        ks = [jnp.concatenate([blocks[2 * n][2], blocks[2 * n + 1][2]], axis=0) for n in ni]
        offd = []
        for n in ni:
            Ga, _, ka = blocks[2 * n]
            Gb, qb, _ = blocks[2 * n + 1]
            gmid = Ga[half - 1:half, :]
            kh = jnp.concatenate([ka * jnp.exp(gmid - Ga), zpad], axis=0)
            offd.append(_mm_nt(qb * jnp.exp(Gb - gmid), kh))
        glasts = [Gs[n][chunk - 1:chunk, :] for n in ni]
        incs = [_mm(vts[n], ks[n] * jnp.exp(glasts[n] - Gs[n])) for n in ni]
        intra = []
        for n in ni:
            s = jnp.concatenate([accs[2 * n][:, 0:chunk], accs[2 * n + 1][:, 0:chunk] + offd[n]], axis=0)
            intra.append(_mm(s, vs[n]))
        cur = list(sts_in)
        before = []
        for n, (hh, _) in enumerate(items):
            before.append(cur[hh])
            cur[hh] = cur[hh] * jnp.exp(glasts[n]) + incs[n]
        outs = [_mm_nt(qs[n] * jnp.exp(Gs[n]), before[n]) + intra[n] for n in ni]
        for n, (r0, c0) in enumerate(rc):
            o_ref[pl.ds(r0, chunk), c0:c0 + HD] = (
                _rms(outs[n], hn) * _silu(hg_ref[pl.ds(r0, chunk), c0:c0 + HD])).astype(o_ref.dtype)
        return cur

    sts = [st_scr[hh] for hh in range(hps)]
    for gi in range(tb // (group * chunk)):
        sts = body(gi, sts)
    for hh in range(hps):
        st_scr[hh] = sts[hh]

    @pl.when(t == pl.num_programs(2) - 1)
    def _():
        for hh in range(hps):
            s_ref[0, hh] = sts[hh].T


def _hgrn_prompt(proj, lb, hg_norm, B, L):
    tb, chunk, hps = HG_TB, HG_CHUNK, HEADS_PER_STEP
    nt = L // tb
    w = hps * HD
    row = lambda b, h, t: b * nt + t
    colspec = lambda base: pl.BlockSpec((tb, w), lambda b, h, t: (row(b, h, t), base // hps + h))
    return pl.pallas_call(
        functools.partial(_hgrn_prompt_kernel, tb=tb, chunk=chunk, group=HG_GROUP, hps=hps),
        grid=(B, HEADS // hps, nt),
        in_specs=[
            colspec(HQB), colspec(HFB), colspec(HIB), colspec(HGB),
            pl.BlockSpec((1, w), lambda b, h, t: (0, h)),
            pl.BlockSpec((1, HD), lambda b, h, t: (0, 0)),
        ],
        out_specs=[
            pl.BlockSpec((tb, w), lambda b, h, t: (row(b, h, t), h)),
            pl.BlockSpec((1, hps, HD, HD), lambda b, h, t: (b, h, 0, 0)),
        ],
        out_shape=[jax.ShapeDtypeStruct((B * L, HEADS * HD), BF16),
                   jax.ShapeDtypeStruct((B, HEADS, HD, HD), F32)],
        scratch_shapes=[pltpu.VMEM((tb, w), F32), pltpu.VMEM((tb, w), F32), pltpu.VMEM((tb, w), F32),
                        pltpu.VMEM((hps, HD, HD), F32)],
        compiler_params=_cparams(("parallel", "parallel", "arbitrary")),
        name="hgrn_prompt",
    )(proj, proj, proj, proj, lb, hg_norm)


def _column_broadcasts(x, sel):
    d = lambda y: lax.dot_general(y, sel, (((0,), (0,)), ((), ())), preferred_element_type=F32)
    hi, lo = _split(x)
    return d(hi) + d(lo)


def _decode_kernel(qkv_ref, z_ref, hq_ref, hf_ref, hi_ref, hg_ref, ba_ref, cs_ref, sg_ref, sh_ref,
                   cw_ref, alog_ref, dtb_ref, lb_ref, gn_ref, hn_ref,
                   oa_ref, ob_ref, sgo_ref, sho_ref, *, nb):
    cd = cw_ref.shape[1]
    cw = cw_ref[...]
    cs = cs_ref[...]
    acc = cs[:, 0:cd] * cw[0:1, :]
    acc = acc + cs[:, cd:2 * cd] * cw[1:2, :]
    acc = acc + cs[:, 2 * cd:3 * cd] * cw[2:3, :]
    acc = acc + qkv_ref[...] * cw[3:4, :]
    conv = _silu(acc)
    nqk = HEADS * HD
    ba = ba_ref[...]
    gn = gn_ref[...]
    hn = hn_ref[...]
    z = z_ref[...]
    hgate = hg_ref[...]

    qs, ks, vs = [], [], []
    for h in range(HEADS):
        qs.append(_l2n(conv[:, h * HD:(h + 1) * HD]) * (HD ** -0.5))
        ks.append(_l2n(conv[:, nqk + h * HD:nqk + (h + 1) * HD]))
        vs.append(conv[:, 2 * nqk + h * HD:2 * nqk + (h + 1) * HD])
    lb = lb_ref[...]
    hf = hf_ref[...]
    forget = lb + (1.0 - lb) * _sigmoid(hf)
    hk = (1.0 - lb) * _sigmoid(-hf)
    hq = _silu(hq_ref[...])
    hv = hi_ref[...]

    sel_r = lax.broadcasted_iota(I32, (nb, nb * HD), 0)
    sel_c = lax.broadcasted_iota(I32, (nb, nb * HD), 1)
    sel = jnp.where(sel_c // HD == sel_r, 1.0, 0.0).astype(BF16)

    for h in range(HEADS):
        hs = slice(h * HD, (h + 1) * HD)
        beta = _sigmoid(ba[:, h:h + 1])
        g = -jnp.exp(alog_ref[0:1, h:h + 1]) * _softplus(ba[:, HEADS + h:HEADS + h + 1] + dtb_ref[0:1, h:h + 1])
        eg = jnp.exp(g)
        qk = jnp.sum(qs[h] * ks[h], axis=-1, keepdims=True)
        kB = _column_broadcasts(ks[h], sel)
        qB = _column_broadcasts(qs[h], sel)
        fB = _column_broadcasts(forget[:, hs], sel)
        hkB = _column_broadcasts(hk[:, hs], sel)
        hqB = _column_broadcasts(hq[:, hs], sel)
        for b in range(nb):
            bs = slice(b * HD, (b + 1) * HD)
            S = sg_ref[b, h]
            kS = jnp.sum(S * kB[:, bs], axis=0, keepdims=True)
            qS = jnp.sum(S * qB[:, bs], axis=0, keepdims=True)
            egb = eg[b:b + 1, :]
            v_new = beta[b:b + 1, :] * (vs[h][b:b + 1, :] - egb * kS)
            o = egb * qS + qk[b:b + 1, :] * v_new
            sgo_ref[b, h] = S * egb + kB[:, bs] * v_new
            oa_ref[b:b + 1, hs] = _rms(o, gn) * _silu(z[b:b + 1, hs])
            Sh = sh_ref[b, h] * fB[:, bs] + hkB[:, bs] * hv[b:b + 1, hs]
            sho_ref[b, h] = Sh
            ob = jnp.sum(Sh * hqB[:, bs], axis=0, keepdims=True)
            ob_ref[b:b + 1, hs] = _rms(ob, hn) * _silu(hgate[b:b + 1, hs])


def _decode(proj, ba, conv_state2d, s_gdn, s_hg, conv_w, alog_row, dtb_row, lb, gdn_norm, hg_norm):
    nb = DEC_TB
    T = proj.shape[0]
    cd = conv_w.shape[1]
    w = HEADS * HD
    cblk = lambda width, idx: pl.BlockSpec((nb, width), lambda i: (i, idx))
    sblk = pl.BlockSpec((nb, HEADS, HD, HD), lambda i: (i, 0, 0, 0))
    full = lambda shape: pl.BlockSpec(shape, lambda i: (0,) * len(shape))
    return pl.pallas_call(
        functools.partial(_decode_kernel, nb=nb),
        grid=(T // nb,),
        in_specs=[
            cblk(cd, 0), cblk(w, ZB // HEADS), cblk(w, HQB // HEADS), cblk(w, HFB // HEADS),
            cblk(w, HIB // HEADS), cblk(w, HGB // HEADS),
            cblk(LANES, 0), cblk(3 * cd, 0), sblk, sblk,
            full((CONV_K, cd)), full((1, LANES)), full((1, LANES)), full((1, w)), full((1, HD)), full((1, HD)),
        ],
        out_specs=[cblk(w, 0), cblk(w, 0), sblk, sblk],
        out_shape=[jax.ShapeDtypeStruct((T, w), F32), jax.ShapeDtypeStruct((T, w), F32),
                   jax.ShapeDtypeStruct(s_gdn.shape, F32), jax.ShapeDtypeStruct(s_hg.shape, F32)],
        compiler_params=_cparams(("parallel",)),
        name="decode",
    )(proj, proj, proj, proj, proj, proj, ba, conv_state2d, s_gdn, s_hg,
      conv_w, alog_row, dtb_row, lb, gdn_norm, hg_norm)


def _out_router_kernel(x_ref, oa_ref, ob_ref, woa_ref, wob_ref, n2_ref, wrh_ref, wrl_ref, br_ref,
                       x1_ref, h2_ref, ti_ref, tw_ref, *, nvalid):
    i = pl.program_id(0)

    @pl.when(i >= nvalid)
    def _():
        h2_ref[...] = jnp.zeros(h2_ref.shape, h2_ref.dtype)
        ti_ref[...] = jnp.zeros(ti_ref.shape, ti_ref.dtype)

    @pl.when(i < nvalid)
    def _():
        _out_router_tile(x_ref, oa_ref, ob_ref, woa_ref, wob_ref, n2_ref, wrh_ref, wrl_ref, br_ref,
                         x1_ref, h2_ref, ti_ref, tw_ref)


def _out_router_tile(x_ref, oa_ref, ob_ref, woa_ref, wob_ref, n2_ref, wrh_ref, wrl_ref, br_ref,
                     x1_ref, h2_ref, ti_ref, tw_ref):
    y = x_ref[...] + jnp.dot(oa_ref[...].astype(BF16), woa_ref[...], preferred_element_type=F32)
    y = y + jnp.dot(ob_ref[...].astype(BF16), wob_ref[...], preferred_element_type=F32)
    x1_ref[...] = y
    h2 = _rms(y, n2_ref[...])
    h2_ref[...] = h2
    logits = _mm3(_split(h2), (wrh_ref[...], wrl_ref[...])) + br_ref[...]
    lane = lax.broadcasted_iota(I32, logits.shape, 1)
    logits = jnp.where(lane < N_EXPERTS, logits, NEG)
    ti = jnp.zeros(logits.shape, I32)
    tw = jnp.zeros(logits.shape, F32)
    m0 = None
    for kk in range(TOP_K):
        m = jnp.max(logits, axis=-1, keepdims=True)
        idx = jnp.min(jnp.where(logits == m, lane, LANES), axis=-1, keepdims=True)
        if m0 is None:
            m0 = m
        ti = jnp.where(lane == kk, idx, ti)
        tw = jnp.where(lane == kk, jnp.exp(m - m0), tw)
        logits = jnp.where(lane == idx, NEG * 2.0, logits)
    tw_ref[...] = tw / jnp.sum(tw, axis=-1, keepdims=True)
    ti_ref[...] = ti


def _out_router_into_kernel(h2_all_ref, ti_all_ref, *refs, nvalid):
    del h2_all_ref, ti_all_ref
    _out_router_kernel(*refs, nvalid=nvalid)


def _out_router(x, oa, ob, wo_a, wo_b, norm2, wr_hi, wr_lo, b_router_p, tm, total_rows, row_off, into=None):
    T, D = x.shape
    w = oa.shape[1]
    off = row_off // tm
    nvalid = T // tm
    steps = nvalid if into is not None else -(-total_rows // tm)
    rowblk = lambda width: pl.BlockSpec((tm, width), lambda i: (jnp.minimum(i, nvalid - 1), 0))
    allblk = lambda width: pl.BlockSpec((tm, width), lambda i: (i + off, 0))
    full = lambda shape: pl.BlockSpec(shape, lambda i: (0,) * len(shape))
    in_specs = [rowblk(D), rowblk(w), rowblk(w), full((w, D)), full((w, D)), full((1, D)),
                full((D, LANES)), full((D, LANES)), full((1, LANES))]
    args = (x, oa, ob, wo_a, wo_b, norm2, wr_hi, wr_lo, b_router_p)
    body, aliases = _out_router_kernel, {}
    if into is not None:
        in_specs = [pl.BlockSpec(memory_space=pl.ANY)] * 2 + in_specs
        args = tuple(into) + args
        body, aliases = _out_router_into_kernel, {0: 1, 1: 2}
    return pl.pallas_call(
        functools.partial(body, nvalid=nvalid),
        grid=(steps,),
        in_specs=in_specs,
        out_specs=[rowblk(D), allblk(D), allblk(LANES), rowblk(LANES)],
        out_shape=[jax.ShapeDtypeStruct((T, D), F32), jax.ShapeDtypeStruct((total_rows, D), F32),
                   jax.ShapeDtypeStruct((total_rows, LANES), I32), jax.ShapeDtypeStruct((T, LANES), F32)],
        input_output_aliases=aliases,
        compiler_params=_cparams(("arbitrary",)),
        name="out_router",
    )(*args)


def _weight_ring_step(rt, w_hbm, wbuf, sem, tcols):
    te_ref, tv_ref, tf_ref, tnx_ref, tlast_ref, trun_ref, nr_ref = rt
    n = pl.program_id(0)
    m = pl.program_id(1)
    slot = lax.rem(n * nr_ref[0] + trun_ref[m], 2)

    def copy(e, nn, s):
        c0 = pl.multiple_of(nn * tcols, tcols)
        return pltpu.make_async_copy(w_hbm.at[e, :, pl.ds(c0, tcols)], wbuf.at[s], sem.at[s])

    @pl.when((n == 0) & (m == 0))
    def _():
        copy(te_ref[m], n, slot).start()

    copy(te_ref[m], n, slot).wait()
    last = tlast_ref[m] != 0

    @pl.when(jnp.logical_or(jnp.logical_not(last), n + 1 < pl.num_programs(0)))
    def _():
        copy(tnx_ref[m], jnp.where(last, n + 1, n), 1 - slot).start()

    return slot


def _tile_ring_step(x_hbm, xbuf, xsem):
    n = pl.program_id(0)
    m = pl.program_id(1)
    n_m = pl.num_programs(1)
    total = pl.num_programs(0) * n_m
    tm = xbuf.shape[1]
    nslot = xbuf.shape[0]
    s = n * n_m + m

    def copy(step):
        r0 = pl.multiple_of(lax.rem(step, n_m) * tm, tm)
        slot = lax.rem(step, nslot)
        return pltpu.make_async_copy(x_hbm.at[pl.ds(r0, tm), :], xbuf.at[slot], xsem.at[slot])

    @pl.when(s == 0)
    def _():
        for first in range(nslot - 1):
            copy(first).start()

    @pl.when(s + nslot - 1 < total)
    def _():
        copy(s + nslot - 1).start()

    copy(s).wait()
    return lax.rem(s, nslot)


def _moe_gate_up_kernel(te_ref, tv_ref, tf_ref, tnx_ref, tlast_ref, trun_ref, nr_ref,
                        x_hbm, w_hbm, bg_ref, bu_ref, act_ref, wbuf, wc_scr, sem, xbuf, xsem):
    rt = (te_ref, tv_ref, tf_ref, tnx_ref, tlast_ref, trun_ref, nr_ref)
    m = pl.program_id(1)
    tnw = wbuf.shape[2]
    xslot = _tile_ring_step(x_hbm, xbuf, xsem)
    grp = 2 * LANES
    ngrp = tnw // grp

    @pl.when(tv_ref[m] != 0)
    def _():
        @pl.when(tf_ref[m] != 0)
        def _():
            slot = _weight_ring_step(rt, w_hbm, wbuf, sem, tnw)
            src = lax.broadcasted_iota(I32, (grp, grp), 0)
            dst = lax.broadcasted_iota(I32, (grp, grp), 1)
            want = jnp.where(dst < LANES, 2 * dst, 2 * (dst - LANES) + 1)
            perm = jnp.where(src == want, 1.0, 0.0).astype(BF16)
            for g in range(ngrp):
                wt = wbuf[slot, :, g * grp:(g + 1) * grp].astype(BF16)
                wc_scr[:, g * grp:(g + 1) * grp] = jnp.dot(wt, perm, preferred_element_type=F32).astype(BF16)

        x = xbuf[xslot].astype(BF16)
        bg = bg_ref[0]
        bu = bu_ref[0]
        for g in range(ngrp):
            gu = jnp.dot(x, wc_scr[:, g * grp:(g + 1) * grp], preferred_element_type=F32)
            gate = jnp.minimum(gu[:, 0:LANES] + bg[:, g * LANES:(g + 1) * LANES], SWIGLU_LIMIT)
            up = jnp.clip(gu[:, LANES:grp] + bu[:, g * LANES:(g + 1) * LANES], -SWIGLU_LIMIT, SWIGLU_LIMIT)
            act_ref[:, g * LANES:(g + 1) * LANES] = (
                (up + 1.0) * (gate * _sigmoid_t(gate * SWIGLU_ALPHA))).astype(act_ref.dtype)

    @pl.when(tv_ref[m] == 0)
    def _():
        act_ref[...] = jnp.zeros(act_ref.shape, act_ref.dtype)


def _moe_gate_up(rt, xs, w_gate_up, bg, bu):
    tm, tnw = MOE_TM, MOE_TNW
    P, D = xs.shape
    F2 = w_gate_up.shape[2]
    imap_b = lambda n, m, te, *_: (te[m], 0, n)
    return pl.pallas_call(
        _moe_gate_up_kernel,
        grid_spec=pltpu.PrefetchScalarGridSpec(
            num_scalar_prefetch=len(rt),
            grid=(F2 // tnw, P // tm),
            in_specs=[
                pl.BlockSpec(memory_space=pl.ANY),
                pl.BlockSpec(memory_space=pl.ANY),
                pl.BlockSpec((1, 1, tnw // 2), imap_b),
                pl.BlockSpec((1, 1, tnw // 2), imap_b),
            ],
            out_specs=pl.BlockSpec((tm, tnw // 2), lambda n, m, *_: (m, n)),
            scratch_shapes=[pltpu.VMEM((2, D, tnw), F32), pltpu.VMEM((D, tnw), BF16),
                            pltpu.SemaphoreType.DMA((2,)),
                            pltpu.VMEM((MOE_XSLOTS, tm, D), xs.dtype), pltpu.SemaphoreType.DMA((MOE_XSLOTS,))],
        ),
        out_shape=jax.ShapeDtypeStruct((P, F2 // 2), BF16),
        compiler_params=_cparams(("arbitrary", "arbitrary")),
        name="moe_gate_up",
    )(*rt, xs, w_gate_up, bg, bu)


def _moe_down_kernel(te_ref, tv_ref, tf_ref, tnx_ref, tlast_ref, trun_ref, nr_ref,
                     a_hbm, w_hbm, bd_ref, y_ref, wbuf, wc_scr, sem, abuf, asem):
    rt = (te_ref, tv_ref, tf_ref, tnx_ref, tlast_ref, trun_ref, nr_ref)
    m = pl.program_id(1)
    aslot = _tile_ring_step(a_hbm, abuf, asem)

    @pl.when(tv_ref[m] != 0)
    def _():
        @pl.when(tf_ref[m] != 0)
        def _():
            slot = _weight_ring_step(rt, w_hbm, wbuf, sem, wbuf.shape[2])
            wc_scr[...] = wbuf[slot].astype(BF16)

        y_ref[...] = jnp.dot(abuf[aslot], wc_scr[...], preferred_element_type=F32) + bd_ref[0]

    @pl.when(tv_ref[m] == 0)
    def _():
        y_ref[...] = jnp.zeros(y_ref.shape, y_ref.dtype)


def _moe_down(rt, act, w_down, bd):
    tm, tn = MOE_TM, MOE_TN
    P, F = act.shape
    D = w_down.shape[2]
    return pl.pallas_call(
        _moe_down_kernel,
        grid_spec=pltpu.PrefetchScalarGridSpec(
            num_scalar_prefetch=len(rt),
            grid=(D // tn, P // tm),
            in_specs=[
                pl.BlockSpec(memory_space=pl.ANY),
                pl.BlockSpec(memory_space=pl.ANY),
                pl.BlockSpec((1, 1, tn), lambda n, m, te, *_: (te[m], 0, n)),
            ],
            out_specs=pl.BlockSpec((tm, tn), lambda n, m, *_: (m, n)),
            scratch_shapes=[pltpu.VMEM((2, F, tn), F32), pltpu.VMEM((F, tn), BF16),
                            pltpu.SemaphoreType.DMA((2,)),
                            pltpu.VMEM((MOE_XSLOTS, tm, F), act.dtype), pltpu.SemaphoreType.DMA((MOE_XSLOTS,))],
        ),
        out_shape=jax.ShapeDtypeStruct((P, D), F32),
        compiler_params=_cparams(("arbitrary", "arbitrary")),
        name="moe_down",
    )(*rt, act, w_down, bd)


def _combine_kernel(x1_ref, yg_ref, tw_ref, fn_ref, y_ref):
    tw = tw_ref[...]
    moe = tw[:, 0:1] * yg_ref[0]
    for kk in range(1, TOP_K):
        moe = moe + tw[:, kk:kk + 1] * yg_ref[kk]
    y_ref[...] = _rms(x1_ref[...] + moe, fn_ref[...])


def _combine(x1, yg, tw, final_norm, tm, row_off):
    T, D = x1.shape
    off = row_off // tm
    return pl.pallas_call(
        _combine_kernel,
        grid=(T // tm,),
        in_specs=[
            pl.BlockSpec((tm, D), lambda i: (i, 0)),
            pl.BlockSpec((TOP_K, tm, D), lambda i: (0, i + off, 0)),
            pl.BlockSpec((tm, LANES), lambda i: (i, 0)),
            pl.BlockSpec((1, D), lambda i: (0, 0)),
        ],
        out_specs=pl.BlockSpec((tm, D), lambda i: (i, 0)),
        out_shape=jax.ShapeDtypeStruct((T, D), F32),
        compiler_params=_cparams(("parallel",)),
        name="combine",
    )(x1, yg, tw, final_norm)


def _routing(top_i):
    T = top_i.shape[0]
    P = T * TOP_K
    tm = MOE_TM
    n_tiles = -(-P // tm) + N_EXPERTS
    e_flat = top_i.reshape(P)
    onehot = (e_flat[:, None] == jnp.arange(N_EXPERTS, dtype=I32)[None, :]).astype(I32)
    csum = jnp.cumsum(onehot, axis=0)
    rank = jnp.sum((csum - 1) * onehot, axis=1)
    counts = csum[-1]
    padded = ((counts + tm - 1) // tm) * tm
    ends = jnp.cumsum(padded)
    starts = ends - padded
    pos = starts[e_flat] + rank
    slot_token = jnp.zeros((n_tiles * tm,), I32).at[pos].set(
        jnp.arange(P, dtype=I32) // TOP_K, unique_indices=True, mode="promise_in_bounds")
    tile_start = jnp.arange(n_tiles, dtype=I32) * tm
    tile_valid = (tile_start < ends[-1]).astype(I32)
    tile_expert = jnp.minimum(jnp.sum((tile_start[:, None] >= ends[None, :]).astype(I32), axis=1), N_EXPERTS - 1)
    tile_expert = tile_expert.astype(I32)
    tile_first = jnp.concatenate([jnp.ones((1,), I32), (tile_expert[1:] != tile_expert[:-1]).astype(I32)])
    tile_first = tile_first * tile_valid
    tile_run = jnp.cumsum(tile_first) - 1
    n_runs = jnp.sum(tile_first).reshape(1)
    experts = jnp.arange(N_EXPERTS, dtype=I32)
    later = (counts > 0)[None, :] & (experts[None, :] > tile_expert[:, None])
    nxt = jnp.min(jnp.where(later, experts[None, :], N_EXPERTS), axis=1)
    tile_last = (nxt == N_EXPERTS).astype(I32)
    tile_next = jnp.where(nxt == N_EXPERTS, tile_expert[0], nxt).astype(I32)
    tables = (tile_expert, tile_valid, tile_first, tile_next, tile_last, tile_run.astype(I32), n_runs.astype(I32))
    return pos, slot_token, tables


def kernel(x_prompt, x_sample, state_gdn, state_conv, state_hgrn, lb_table, norm1, w_in, conv_w, A_log, dt_bias,
           gdn_norm, hg_norm, w_out, norm2, w_router, b_router, w_gate_up, b_gate_up, w_down, b_down, final_norm):
    depth = w_in.shape[0]
    assert depth == 1
    B, L, D = x_prompt.shape
    SB = x_sample.shape[0]
    assert x_sample.shape[1] == 1
    nqk = HEADS * HD
    cd = 3 * nqk

    lbs = jnp.cumsum(jax.nn.softmax(lb_table.astype(F32), axis=0), axis=0)
    lb = lbs[0:1]

    assert w_in.shape[2] == MAIN_COLS + 2 * HEADS and cd + nqk == MAIN_COLS // 2
    ba0 = cd + nqk
    w_t = jnp.swapaxes(w_in[0], 0, 1).astype(BF16)
    wo_a = w_out[0, :nqk].astype(BF16)
    wo_b = w_out[0, nqk:].astype(BF16)
    wr_hi, wr_lo = _split(jnp.pad(w_router[0], ((0, 0), (0, LANES - N_EXPERTS))))
    br_p = jnp.pad(b_router[0], (0, LANES - N_EXPERTS))[None, :]
    bg = b_gate_up[0, :, None, 0::2]
    bu = b_gate_up[0, :, None, 1::2]
    bd = b_down[0][:, None, :]
    alog_b = jnp.pad(A_log[0], (HEADS, LANES - 2 * HEADS))[None, :]
    dtb_b = jnp.pad(dt_bias[0], (HEADS, LANES - 2 * HEADS))[None, :]
    alog_row = jnp.pad(A_log[0], (0, LANES - HEADS))[None, :]
    dtb_row = jnp.pad(dt_bias[0], (0, LANES - HEADS))[None, :]
    n1 = norm1[0][None, :]
    n2 = norm2[0][None, :]
    gn = gdn_norm[0][None, :]
    hn = hg_norm[0][None, :]
    fnw = final_norm[None, :]
    cw = conv_w[0]

    xp = x_prompt.reshape(B * L, D)
    xs = x_sample.reshape(SB, D)

    proj_p, ba_p = _in_proj(xp, n1, w_t, ba0, 2 * HEADS, tm=1024, tn=1024)
    proj_s, ba_s = _in_proj(xs, n1, w_t, ba0, 2 * HEADS, tm=SB, tn=1024)

    oa_p, sg_p = _gdn_prompt(proj_p, ba_p, cw, alog_b, dtb_b, gn, B, L)
    ob_p, sh_p = _hgrn_prompt(proj_p, lb, hn, B, L)
    oa_s, ob_s, sg_s, sh_s = _decode(proj_s, ba_s, state_conv[0].reshape(SB, (CONV_K - 1) * cd), state_gdn[0],
                                     state_hgrn[0], cw, alog_row, dtb_row, lb, gn, hn)

    ntok = B * L + SB
    x1_p, h2, ti, tw_p = _out_router(xp, oa_p, ob_p, wo_a, wo_b, n2, wr_hi, wr_lo, br_p,
                                     tm=512, total_rows=ntok, row_off=0)
    x1_s, h2, ti, tw_s = _out_router(xs, oa_s, ob_s, wo_a, wo_b, n2, wr_hi, wr_lo, br_p,
                                     tm=SB, total_rows=ntok, row_off=B * L, into=(h2, ti))

    top_i = ti[:, :TOP_K]
    pos, slot_token, tables = _routing(top_i)
    xs_sorted = h2.at[slot_token].get(mode="promise_in_bounds")
    act = _moe_gate_up(tables, xs_sorted, w_gate_up[0], bg, bu)
    yslots = _moe_down(tables, act, w_down[0], bd)
    pos_kmajor = pos.reshape(B * L + SB, TOP_K).T.reshape(-1)
    yg = yslots.at[pos_kmajor].get(mode="promise_in_bounds").reshape(TOP_K, B * L + SB, D)

    y_p = _combine(x1_p, yg, tw_p, fnw, tm=256, row_off=0)
    y_s = _combine(x1_s, yg, tw_s, fnw, tm=SB, row_off=B * L)

    conv_p = proj_p.reshape(B, L, MAIN_COLS)[:, L - (CONV_K - 1):, :cd]
    conv_s = jnp.concatenate([state_conv[0][:, 1:, :], proj_s[:, None, :cd]], axis=1)
    return (y_p.reshape(B, L, D), y_s.reshape(SB, 1, D),
            sg_p[None], conv_p[None].astype(state_conv.dtype), sh_p[None],
            sg_s[None], conv_s[None].astype(state_conv.dtype), sh_s[None])
```

```python
import functools

import jax
import jax.numpy as jnp
from jax import lax
from jax.experimental import pallas as pl
from jax.experimental.pallas import tpu as pltpu

F32 = jnp.float32
BF16 = jnp.bfloat16
I32 = jnp.int32

EPS = 1e-6
HEADS = 8
HD = 128
CONV_K = 4
N_EXPERTS = 32
TOP_K = 4
SWIGLU_ALPHA = 1.702
SWIGLU_LIMIT = 7.0
LANES = 128
NEG = -1e30

VMEM_LIMIT = 56 * 1024 * 1024

QB, KB, VB, ZB, HQB, HFB, HIB, HGB = (i * HEADS for i in range(8))
MAIN_COLS = 8 * HEADS * HD

HEADS_PER_STEP = 2
GDN_CHUNK = 128
GDN_TB = 1024
HG_CHUNK = 16
HG_TB = 1024
HG_GROUP = 4
DEC_TB = 8
MOE_TM = 256
MOE_TN = 2048
MOE_TNW = 2048
MOE_XSLOTS = 3


def _cparams(sem):
    return pltpu.CompilerParams(dimension_semantics=sem, vmem_limit_bytes=VMEM_LIMIT)


def _mm(a, b):
    return jnp.dot(a.astype(BF16), b.astype(BF16), preferred_element_type=F32)


def _mm_nt(a, b):
    return lax.dot_general(a.astype(BF16), b.astype(BF16), (((1,), (1,)), ((), ())),
                           preferred_element_type=F32)


def _mm_tn(a, b):
    return lax.dot_general(a.astype(BF16), b.astype(BF16), (((0,), (0,)), ((), ())),
                           preferred_element_type=F32)


def _mmh(a, b):
    return jnp.dot(a, b, precision=lax.Precision.HIGHEST, preferred_element_type=F32)


def _split(a):
    hi = a.astype(BF16)
    return hi, (a - hi.astype(F32)).astype(BF16)


def _mm3(a, b):
    d = lambda x, y: jnp.dot(x, y, preferred_element_type=F32)
    return d(a[0], b[0]) + (d(a[0], b[1]) + d(a[1], b[0]))


def _mm_exact_lhs(l_bf16, x):
    d = lambda y: jnp.dot(l_bf16, y, preferred_element_type=F32)
    x0 = x.astype(BF16)
    r1 = x - x0.astype(F32)
    x1 = r1.astype(BF16)
    x2 = (r1 - x1.astype(F32)).astype(BF16)
    return d(x0) + (d(x1) + d(x2))


def _sigmoid(x):
    return 1.0 / (1.0 + jnp.exp(-x))


def _sigmoid_t(x):
    return 0.5 * jnp.tanh(0.5 * x) + 0.5


def _silu(x):
    return x * _sigmoid_t(x)


def _softplus(x):
    return jnp.maximum(x, 0.0) + jnp.log1p(jnp.exp(-jnp.abs(x)))


def _rms(x, w):
    return x * lax.rsqrt(jnp.mean(x * x, axis=-1, keepdims=True) + EPS) * w


def _l2n(x):
    return x * lax.rsqrt(jnp.sum(x * x, axis=-1, keepdims=True) + EPS)


def _in_proj_kernel(x_ref, nw_ref, wt_ref, wbat_ref, o_ref, ba_ref, h_scr):
    nt_dims = (((1,), (1,)), ((), ()))

    @pl.when(pl.program_id(1) == 0)
    def _():
        h = _rms(x_ref[...], nw_ref[...]).astype(BF16)
        h_scr[...] = h
        ba_ref[...] = lax.dot_general(h, wbat_ref[...], nt_dims, preferred_element_type=F32)

    o_ref[...] = lax.dot_general(h_scr[...], wt_ref[...], nt_dims, preferred_element_type=F32)


def _in_proj(x, norm_w, w_t, ba0, nba, tm, tn):
    T, D = x.shape
    N = w_t.shape[0] - nba
    assert ba0 % tn == 0 and nba % 16 == 0
    return pl.pallas_call(
        _in_proj_kernel,
        grid=(T // tm, N // tn),
        in_specs=[
            pl.BlockSpec((tm, D), lambda i, j: (i, 0)),
            pl.BlockSpec((1, D), lambda i, j: (0, 0)),
            pl.BlockSpec((pl.Element(tn), pl.Element(D)),
                         lambda i, j: (pl.multiple_of(j * tn + jnp.where(j * tn >= ba0, nba, 0), 16), 0)),
            pl.BlockSpec((pl.Element(LANES), pl.Element(D)), lambda i, j: (ba0, 0)),
        ],
        out_specs=[
            pl.BlockSpec((tm, tn), lambda i, j: (i, j)),
            pl.BlockSpec((tm, LANES), lambda i, j: (i, 0)),
        ],
        out_shape=[jax.ShapeDtypeStruct((T, N), F32), jax.ShapeDtypeStruct((T, LANES), F32)],
        scratch_shapes=[pltpu.VMEM((tm, D), BF16)],
        compiler_params=_cparams(("parallel", "arbitrary")),
        name="in_proj",
    )(x, norm_w, w_t, w_t)


def _unit_lower_inverse(As, ri, ci):
    n = As[0].shape[0]
    eye = (ri == ci).astype(F32)
    same16 = (ri // 16) == (ci // 16)
    s1 = [_split(jnp.where(same16, -A, 0.0)) for A in As]
    s2 = [_split(_mm3(s, s)) for s in s1]
    s4 = [_split(_mm3(s, s)) for s in s2]
    s8 = [_split(_mm3(s, s)) for s in s4]
    Ts = [eye + jnp.where(same16, -A, 0.0) for A in As]
    for sp in (s2, s4, s8):
        Ts = [T + _mm3(_split(T), s) for T, s in zip(Ts, sp)]
    size = 32
    while size <= n:
        off = ((ri // size) == (ci // size)) & ((ri // (size // 2)) != (ci // (size // 2)))
        bT = [T.astype(BF16) for T in Ts]
        TL = [_mm(bt, jnp.where(off, A, 0.0)) for bt, A in zip(bT, As)]
        Ts = [T - _mm(tl, bt) for T, tl, bt in zip(Ts, TL, bT)]
        size *= 2
    return Ts


def _gdn_prompt_kernel(q_ref, k_ref, v_ref, z_ref, ba_ref, cwq_ref, cwk_ref, cwv_ref, alog_ref, dtb_ref,
                       gn_ref, o_ref, s_ref, ubuf, s_scr, *, tb, chunk, hps):
    hg = pl.program_id(1)
    t = pl.program_id(2)

    @pl.when(t == 0)
    def _():
        ubuf[:, 0:8, :] = jnp.zeros((3, 8, hps * HD), F32)
        s_scr[...] = jnp.zeros((hps, HD, HD), F32)

    ubuf[0, 8:8 + tb, :] = q_ref[...]
    ubuf[1, 8:8 + tb, :] = k_ref[...]
    ubuf[2, 8:8 + tb, :] = v_ref[...]

    ba = ba_ref[...]
    lane = lax.broadcasted_iota(I32, (chunk, LANES), 1)
    beta_all = _sigmoid_t(ba)
    g_all = -jnp.exp(alog_ref[...]) * _softplus(ba + dtb_ref[...])

    ri = lax.broadcasted_iota(I32, (chunk, chunk), 0)
    ci = lax.broadcasted_iota(I32, (chunk, chunk), 1)
    causal = ci <= ri
    strict = ci < ri
    ltri = jnp.where(causal, 1.0, 0.0).astype(BF16)

    def conv(idx, w_ref, r0, c0):
        w = w_ref[:, c0:c0 + HD]
        acc = ubuf[idx, r0 + 5:r0 + 5 + chunk, c0:c0 + HD] * w[0:1, :]
        for j in range(1, CONV_K):
            acc = acc + ubuf[idx, r0 + 5 + j:r0 + 5 + j + chunk, c0:c0 + HD] * w[j:j + 1, :]
        return _silu(acc)

    nc = tb // chunk
    items = [(hh, c) for c in range(nc) for hh in range(hps)]
    qs, ks, vs, betas, gcols, decays, As = [], [], [], [], [], [], []
    gc_all = [_mm_exact_lhs(ltri, g_all[c * chunk:(c + 1) * chunk, :]) for c in range(nc)]
    for hh, c in items:
        r0, c0 = c * chunk, hh * HD
        h = hg * hps + hh
        q = _l2n(conv(0, cwq_ref, r0, c0)) * (HD ** -0.5)
        k = _l2n(conv(1, cwk_ref, r0, c0))
        v = conv(2, cwv_ref, r0, c0)
        beta = jnp.sum(jnp.where(lane == h, beta_all[r0:r0 + chunk, :], 0.0), axis=-1, keepdims=True)
        gcol = jnp.broadcast_to(
            jnp.sum(jnp.where(lane == h + HEADS, gc_all[c], 0.0), axis=-1, keepdims=True), (chunk, HD))
        decay = jnp.exp(jnp.where(causal, gcol - gcol.T, NEG))
        kb = k * beta
        qs.append(q); ks.append(k); vs.append(v); betas.append(beta); gcols.append(gcol); decays.append(decay)
        As.append(jnp.where(strict, _mm_nt(kb, k) * decay, 0.0))
    Ts = _unit_lower_inverse(As, ri, ci)
    us, ws, scs = [], [], []
    for n in range(len(items)):
        sT = _split(Ts[n])
        kb = ks[n] * betas[n]
        us.append(_mm3(sT, _split(vs[n] * betas[n])))
        ws.append(_mm3(sT, _split(kb * jnp.exp(gcols[n]))))
        scs.append(_mm_nt(qs[n], ks[n]) * decays[n])

    S = [s_scr[hh] for hh in range(hps)]
    for n, (hh, c) in enumerate(items):
        r0, c0 = c * chunk, hh * HD
        gcol = gcols[n]
        v_new = us[n] - _mm(ws[n], S[hh])
        o = _mm(qs[n] * jnp.exp(gcol), S[hh]) + _mm(scs[n], v_new)
        glast = gcol[chunk - 1:chunk, :]
        S[hh] = S[hh] * jnp.exp(glast) + _mm_tn(ks[n] * jnp.exp(glast - gcol), v_new)
        o_ref[r0:r0 + chunk, c0:c0 + HD] = (
            _rms(o, gn_ref[...]) * _silu(z_ref[r0:r0 + chunk, c0:c0 + HD])).astype(o_ref.dtype)
    for hh in range(hps):
        s_scr[hh] = S[hh]

    ubuf[:, 0:8, :] = ubuf[:, tb:tb + 8, :]

    @pl.when(t == pl.num_programs(2) - 1)
    def _():
        for hh in range(hps):
            s_ref[0, hh] = S[hh]


def _gdn_prompt(proj, ba, conv_w, alog_b, dtb_b, gdn_norm, B, L):
    tb, chunk, hps = GDN_TB, GDN_CHUNK, HEADS_PER_STEP
    nt = L // tb
    w = hps * HD
    row = lambda b, h, t: b * nt + t
    colspec = lambda base: pl.BlockSpec((tb, w), lambda b, h, t: (row(b, h, t), base // hps + h))
    cwspec = lambda base: pl.BlockSpec((CONV_K, w), lambda b, h, t: (0, base // hps + h))
    hvec = pl.BlockSpec((1, LANES), lambda b, h, t: (0, 0))
    return pl.pallas_call(
        functools.partial(_gdn_prompt_kernel, tb=tb, chunk=chunk, hps=hps),
        grid=(B, HEADS // hps, nt),
        in_specs=[
            colspec(QB), colspec(KB), colspec(VB), colspec(ZB),
            pl.BlockSpec((tb, LANES), lambda b, h, t: (row(b, h, t), 0)),
            cwspec(QB), cwspec(KB), cwspec(VB),
            hvec, hvec,
            pl.BlockSpec((1, HD), lambda b, h, t: (0, 0)),
        ],
        out_specs=[
            pl.BlockSpec((tb, w), lambda b, h, t: (row(b, h, t), h)),
            pl.BlockSpec((1, hps, HD, HD), lambda b, h, t: (b, h, 0, 0)),
        ],
        out_shape=[jax.ShapeDtypeStruct((B * L, HEADS * HD), BF16),
                   jax.ShapeDtypeStruct((B, HEADS, HD, HD), F32)],
        scratch_shapes=[pltpu.VMEM((3, tb + 8, w), F32), pltpu.VMEM((hps, HD, HD), F32)],
        compiler_params=_cparams(("parallel", "parallel", "arbitrary")),
        name="gdn_prompt",
    )(proj, proj, proj, proj, ba, conv_w, conv_w, conv_w, alog_b, dtb_b, gdn_norm)


def _hgrn_prompt_kernel(hq_ref, hf_ref, hi_ref, hg_ref, lb_ref, hn_ref, o_ref, s_ref,
                        g_scr, q_scr, k_scr, st_scr, *, tb, chunk, group, hps):
    t = pl.program_id(2)

    @pl.when(t == 0)
    def _():
        st_scr[...] = jnp.zeros((hps, HD, HD), F32)

    lb = lb_ref[...]
    hf = hf_ref[...]
    forget = lb + (1.0 - lb) * _sigmoid(hf)
    k_scr[...] = (1.0 - lb) * _sigmoid(-hf)
    q_scr[...] = _silu(hq_ref[...])
    ri = lax.broadcasted_iota(I32, (LANES, LANES), 0)
    ci = lax.broadcasted_iota(I32, (LANES, LANES), 1)
    lblk = jnp.where(((ri // chunk) == (ci // chunk)) & (ci <= ri), 1.0, 0.0).astype(BF16)
    lf = jnp.log(forget)
    for r in range(tb // LANES):
        g_scr[r * LANES:(r + 1) * LANES, :] = _mm_exact_lhs(lblk, lf[r * LANES:(r + 1) * LANES, :])

    half = chunk // 2
    row8 = lax.broadcasted_iota(I32, (half, HD), 0)
    lane8 = lax.broadcasted_iota(I32, (half, HD), 1)
    zpad = jnp.zeros((half, HD), F32)
    hn = hn_ref[...]

    def body(gi, sts_in):
        base = gi * (group * chunk)
        items = [(hh, cc) for cc in range(group) for hh in range(hps)]
        ni = range(len(items))
        rc = [(base + cc * chunk, hh * HD) for hh, cc in items]
        vs = [hi_ref[pl.ds(r0, chunk), c0:c0 + HD] for r0, c0 in rc]
        vts = [v.T for v in vs]
        blocks = []
        for r0, c0 in rc:
            for r in (r0, r0 + half):
                blocks.append((g_scr[pl.ds(r, half), c0:c0 + HD], q_scr[pl.ds(r, half), c0:c0 + HD],
                               k_scr[pl.ds(r, half), c0:c0 + HD]))
        accs = [zpad] * len(blocks)
        for j in range(half):
            for n, (Gb, qb, kb) in enumerate(blocks):
                e = jnp.exp(Gb - Gb[j:j + 1, :])
                col = jnp.sum(e * qb * kb[j:j + 1, :], axis=-1, keepdims=True)
                accs[n] = jnp.where(lane8 == (n % 2) * half + j, col, accs[n])
        accs = [jnp.where(row8 >= lane8 - (n % 2) * half, a, 0.0) for n, a in enumerate(accs)]
        Gs = [jnp.concatenate([blocks[2 * n][0], blocks[2 * n + 1][0]], axis=0) for n in ni]
        qs = [jnp.concatenate([blocks[2 * n][1], blocks[2 * n + 1][1]], axis=0) for n in ni]
        ks = [jnp.concatenate([blocks[2 * n][2], blocks[2 * n + 1][2]], axis=0) for n in ni]
        offd = []
        for n in ni:
            Ga, _, ka = blocks[2 * n]
            Gb, qb, _ = blocks[2 * n + 1]
            gmid = Ga[half - 1:half, :]
            kh = jnp.concatenate([ka * jnp.exp(gmid - Ga), zpad], axis=0)
            offd.append(_mm_nt(qb * jnp.exp(Gb - gmid), kh))
        glasts = [Gs[n][chunk - 1:chunk, :] for n in ni]
        incs = [_mm(vts[n], ks[n] * jnp.exp(glasts[n] - Gs[n])) for n in ni]
        intra = []
        for n in ni:
            s = jnp.concatenate([accs[2 * n][:, 0:chunk], accs[2 * n + 1][:, 0:chunk] + offd[n]], axis=0)
            intra.append(_mm(s, vs[n]))
        cur = list(sts_in)
        before = []
        for n, (hh, _) in enumerate(items):
            before.append(cur[hh])
            cur[hh] = cur[hh] * jnp.exp(glasts[n]) + incs[n]
        outs = [_mm_nt(qs[n] * jnp.exp(Gs[n]), before[n]) + intra[n] for n in ni]
        for n, (r0, c0) in enumerate(rc):
            o_ref[pl.ds(r0, chunk), c0:c0 + HD] = (
                _rms(outs[n], hn) * _silu(hg_ref[pl.ds(r0, chunk), c0:c0 + HD])).astype(o_ref.dtype)
        return cur

    sts = [st_scr[hh] for hh in range(hps)]
    for gi in range(tb // (group * chunk)):
        sts = body(gi, sts)
    for hh in range(hps):
        st_scr[hh] = sts[hh]

    @pl.when(t == pl.num_programs(2) - 1)
    def _():
        for hh in range(hps):
            s_ref[0, hh] = sts[hh].T


def _hgrn_prompt(proj, lb, hg_norm, B, L):
    tb, chunk, hps = HG_TB, HG_CHUNK, HEADS_PER_STEP
    nt = L // tb
    w = hps * HD
    row = lambda b, h, t: b * nt + t
    colspec = lambda base: pl.BlockSpec((tb, w), lambda b, h, t: (row(b, h, t), base // hps + h))
    return pl.pallas_call(
        functools.partial(_hgrn_prompt_kernel, tb=tb, chunk=chunk, group=HG_GROUP, hps=hps),
        grid=(B, HEADS // hps, nt),
        in_specs=[
            colspec(HQB), colspec(HFB), colspec(HIB), colspec(HGB),
            pl.BlockSpec((1, w), lambda b, h, t: (0, h)),
            pl.BlockSpec((1, HD), lambda b, h, t: (0, 0)),
        ],
        out_specs=[
            pl.BlockSpec((tb, w), lambda b, h, t: (row(b, h, t), h)),
            pl.BlockSpec((1, hps, HD, HD), lambda b, h, t: (b, h, 0, 0)),
        ],
        out_shape=[jax.ShapeDtypeStruct((B * L, HEADS * HD), BF16),
                   jax.ShapeDtypeStruct((B, HEADS, HD, HD), F32)],
        scratch_shapes=[pltpu.VMEM((tb, w), F32), pltpu.VMEM((tb, w), F32), pltpu.VMEM((tb, w), F32),
                        pltpu.VMEM((hps, HD, HD), F32)],
        compiler_params=_cparams(("parallel", "parallel", "arbitrary")),
        name="hgrn_prompt",
    )(proj, proj, proj, proj, lb, hg_norm)


def _column_broadcasts(x, sel):
    d = lambda y: lax.dot_general(y, sel, (((0,), (0,)), ((), ())), preferred_element_type=F32)
    hi, lo = _split(x)
    return d(hi) + d(lo)


def _decode_kernel(qkv_ref, z_ref, hq_ref, hf_ref, hi_ref, hg_ref, ba_ref, cs_ref, sg_ref, sh_ref,
                   cw_ref, alog_ref, dtb_ref, lb_ref, gn_ref, hn_ref,
                   oa_ref, ob_ref, sgo_ref, sho_ref, *, nb):
    cd = cw_ref.shape[1]
    cw = cw_ref[...]
    cs = cs_ref[...]
    acc = cs[:, 0:cd] * cw[0:1, :]
    acc = acc + cs[:, cd:2 * cd] * cw[1:2, :]
    acc = acc + cs[:, 2 * cd:3 * cd] * cw[2:3, :]
    acc = acc + qkv_ref[...] * cw[3:4, :]
    conv = _silu(acc)
    nqk = HEADS * HD
    ba = ba_ref[...]
    gn = gn_ref[...]
    hn = hn_ref[...]
    z = z_ref[...]
    hgate = hg_ref[...]

    qs, ks, vs = [], [], []
    for h in range(HEADS):
        qs.append(_l2n(conv[:, h * HD:(h + 1) * HD]) * (HD ** -0.5))
        ks.append(_l2n(conv[:, nqk + h * HD:nqk + (h + 1) * HD]))
        vs.append(conv[:, 2 * nqk + h * HD:2 * nqk + (h + 1) * HD])
    lb = lb_ref[...]
    hf = hf_ref[...]
    forget = lb + (1.0 - lb) * _sigmoid(hf)
    hk = (1.0 - lb) * _sigmoid(-hf)
    hq = _silu(hq_ref[...])
    hv = hi_ref[...]

    sel_r = lax.broadcasted_iota(I32, (nb, nb * HD), 0)
    sel_c = lax.broadcasted_iota(I32, (nb, nb * HD), 1)
    sel = jnp.where(sel_c // HD == sel_r, 1.0, 0.0).astype(BF16)

    for h in range(HEADS):
        hs = slice(h * HD, (h + 1) * HD)
        beta = _sigmoid(ba[:, h:h + 1])
        g = -jnp.exp(alog_ref[0:1, h:h + 1]) * _softplus(ba[:, HEADS + h:HEADS + h + 1] + dtb_ref[0:1, h:h + 1])
        eg = jnp.exp(g)
        qk = jnp.sum(qs[h] * ks[h], axis=-1, keepdims=True)
        kB = _column_broadcasts(ks[h], sel)
        qB = _column_broadcasts(qs[h], sel)
        fB = _column_broadcasts(forget[:, hs], sel)
        hkB = _column_broadcasts(hk[:, hs], sel)
        hqB = _column_broadcasts(hq[:, hs], sel)
        for b in range(nb):
            bs = slice(b * HD, (b + 1) * HD)
            S = sg_ref[b, h]
            kS = jnp.sum(S * kB[:, bs], axis=0, keepdims=True)
            qS = jnp.sum(S * qB[:, bs], axis=0, keepdims=True)
            egb = eg[b:b + 1, :]
            v_new = beta[b:b + 1, :] * (vs[h][b:b + 1, :] - egb * kS)
            o = egb * qS + qk[b:b + 1, :] * v_new
            sgo_ref[b, h] = S * egb + kB[:, bs] * v_new
            oa_ref[b:b + 1, hs] = _rms(o, gn) * _silu(z[b:b + 1, hs])
            Sh = sh_ref[b, h] * fB[:, bs] + hkB[:, bs] * hv[b:b + 1, hs]
            sho_ref[b, h] = Sh
            ob = jnp.sum(Sh * hqB[:, bs], axis=0, keepdims=True)
            ob_ref[b:b + 1, hs] = _rms(ob, hn) * _silu(hgate[b:b + 1, hs])


def _decode(proj, ba, conv_state2d, s_gdn, s_hg, conv_w, alog_row, dtb_row, lb, gdn_norm, hg_norm):
    nb = DEC_TB
    T = proj.shape[0]
    cd = conv_w.shape[1]
    w = HEADS * HD
    cblk = lambda width, idx: pl.BlockSpec((nb, width), lambda i: (i, idx))
    sblk = pl.BlockSpec((nb, HEADS, HD, HD), lambda i: (i, 0, 0, 0))
    full = lambda shape: pl.BlockSpec(shape, lambda i: (0,) * len(shape))
    return pl.pallas_call(
        functools.partial(_decode_kernel, nb=nb),
        grid=(T // nb,),
        in_specs=[
            cblk(cd, 0), cblk(w, ZB // HEADS), cblk(w, HQB // HEADS), cblk(w, HFB // HEADS),
            cblk(w, HIB // HEADS), cblk(w, HGB // HEADS),
            cblk(LANES, 0), cblk(3 * cd, 0), sblk, sblk,
            full((CONV_K, cd)), full((1, LANES)), full((1, LANES)), full((1, w)), full((1, HD)), full((1, HD)),
        ],
        out_specs=[cblk(w, 0), cblk(w, 0), sblk, sblk],
        out_shape=[jax.ShapeDtypeStruct((T, w), F32), jax.ShapeDtypeStruct((T, w), F32),
                   jax.ShapeDtypeStruct(s_gdn.shape, F32), jax.ShapeDtypeStruct(s_hg.shape, F32)],
        compiler_params=_cparams(("parallel",)),
        name="decode",
    )(proj, proj, proj, proj, proj, proj, ba, conv_state2d, s_gdn, s_hg,
      conv_w, alog_row, dtb_row, lb, gdn_norm, hg_norm)


def _out_router_kernel(x_ref, oa_ref, ob_ref, woa_ref, wob_ref, n2_ref, wrh_ref, wrl_ref, br_ref,
                       x1_ref, h2_ref, ti_ref, tw_ref, *, nvalid):
    i = pl.program_id(0)

    @pl.when(i >= nvalid)
    def _():
        h2_ref[...] = jnp.zeros(h2_ref.shape, h2_ref.dtype)
        ti_ref[...] = jnp.zeros(ti_ref.shape, ti_ref.dtype)

    @pl.when(i < nvalid)
    def _():
        _out_router_tile(x_ref, oa_ref, ob_ref, woa_ref, wob_ref, n2_ref, wrh_ref, wrl_ref, br_ref,
                         x1_ref, h2_ref, ti_ref, tw_ref)


def _out_router_tile(x_ref, oa_ref, ob_ref, woa_ref, wob_ref, n2_ref, wrh_ref, wrl_ref, br_ref,
                     x1_ref, h2_ref, ti_ref, tw_ref):
    y = x_ref[...] + jnp.dot(oa_ref[...].astype(BF16), woa_ref[...], preferred_element_type=F32)
    y = y + jnp.dot(ob_ref[...].astype(BF16), wob_ref[...], preferred_element_type=F32)
    x1_ref[...] = y
    h2 = _rms(y, n2_ref[...])
    h2_ref[...] = h2
    logits = _mm3(_split(h2), (wrh_ref[...], wrl_ref[...])) + br_ref[...]
    lane = lax.broadcasted_iota(I32, logits.shape, 1)
    logits = jnp.where(lane < N_EXPERTS, logits, NEG)
    ti = jnp.zeros(logits.shape, I32)
    tw = jnp.zeros(logits.shape, F32)
    m0 = None
    for kk in range(TOP_K):
        m = jnp.max(logits, axis=-1, keepdims=True)
        idx = jnp.min(jnp.where(logits == m, lane, LANES), axis=-1, keepdims=True)
        if m0 is None:
            m0 = m
        ti = jnp.where(lane == kk, idx, ti)
        tw = jnp.where(lane == kk, jnp.exp(m - m0), tw)
        logits = jnp.where(lane == idx, NEG * 2.0, logits)
    tw_ref[...] = tw / jnp.sum(tw, axis=-1, keepdims=True)
    ti_ref[...] = ti


def _out_router_into_kernel(h2_all_ref, ti_all_ref, *refs, nvalid):
    del h2_all_ref, ti_all_ref
    _out_router_kernel(*refs, nvalid=nvalid)


def _out_router(x, oa, ob, wo_a, wo_b, norm2, wr_hi, wr_lo, b_router_p, tm, total_rows, row_off, into=None):
    T, D = x.shape
    w = oa.shape[1]
    off = row_off // tm
    nvalid = T // tm
    steps = nvalid if into is not None else -(-total_rows // tm)
    rowblk = lambda width: pl.BlockSpec((tm, width), lambda i: (jnp.minimum(i, nvalid - 1), 0))
    allblk = lambda width: pl.BlockSpec((tm, width), lambda i: (i + off, 0))
    full = lambda shape: pl.BlockSpec(shape, lambda i: (0,) * len(shape))
    in_specs = [rowblk(D), rowblk(w), rowblk(w), full((w, D)), full((w, D)), full((1, D)),
                full((D, LANES)), full((D, LANES)), full((1, LANES))]
    args = (x, oa, ob, wo_a, wo_b, norm2, wr_hi, wr_lo, b_router_p)
    body, aliases = _out_router_kernel, {}
    if into is not None:
        in_specs = [pl.BlockSpec(memory_space=pl.ANY)] * 2 + in_specs
        args = tuple(into) + args
        body, aliases = _out_router_into_kernel, {0: 1, 1: 2}
    return pl.pallas_call(
        functools.partial(body, nvalid=nvalid),
        grid=(steps,),
        in_specs=in_specs,
        out_specs=[rowblk(D), allblk(D), allblk(LANES), rowblk(LANES)],
        out_shape=[jax.ShapeDtypeStruct((T, D), F32), jax.ShapeDtypeStruct((total_rows, D), F32),
                   jax.ShapeDtypeStruct((total_rows, LANES), I32), jax.ShapeDtypeStruct((T, LANES), F32)],
        input_output_aliases=aliases,
        compiler_params=_cparams(("arbitrary",)),
        name="out_router",
    )(*args)


def _weight_ring_step(rt, w_hbm, wbuf, sem, tcols):
    te_ref, tv_ref, tf_ref, tnx_ref, tlast_ref, trun_ref, nr_ref = rt
    n = pl.program_id(0)
    m = pl.program_id(1)
    slot = lax.rem(n * nr_ref[0] + trun_ref[m], 2)

    def copy(e, nn, s):
        c0 = pl.multiple_of(nn * tcols, tcols)
        return pltpu.make_async_copy(w_hbm.at[e, :, pl.ds(c0, tcols)], wbuf.at[s], sem.at[s])

    @pl.when((n == 0) & (m == 0))
    def _():
        copy(te_ref[m], n, slot).start()

    copy(te_ref[m], n, slot).wait()
    last = tlast_ref[m] != 0

    @pl.when(jnp.logical_or(jnp.logical_not(last), n + 1 < pl.num_programs(0)))
    def _():
        copy(tnx_ref[m], jnp.where(last, n + 1, n), 1 - slot).start()

    return slot


def _tile_ring_step(x_hbm, xbuf, xsem):
    n = pl.program_id(0)
    m = pl.program_id(1)
    n_m = pl.num_programs(1)
    total = pl.num_programs(0) * n_m
    tm = xbuf.shape[1]
    nslot = xbuf.shape[0]
    s = n * n_m + m

    def copy(step):
        r0 = pl.multiple_of(lax.rem(step, n_m) * tm, tm)
        slot = lax.rem(step, nslot)
        return pltpu.make_async_copy(x_hbm.at[pl.ds(r0, tm), :], xbuf.at[slot], xsem.at[slot])

    @pl.when(s == 0)
    def _():
        for first in range(nslot - 1):
            copy(first).start()

    @pl.when(s + nslot - 1 < total)
    def _():
        copy(s + nslot - 1).start()

    copy(s).wait()
    return lax.rem(s, nslot)


def _moe_gate_up_kernel(te_ref, tv_ref, tf_ref, tnx_ref, tlast_ref, trun_ref, nr_ref,
                        x_hbm, w_hbm, bg_ref, bu_ref, act_ref, wbuf, wc_scr, sem, xbuf, xsem):
    rt = (te_ref, tv_ref, tf_ref, tnx_ref, tlast_ref, trun_ref, nr_ref)
    m = pl.program_id(1)
    tnw = wbuf.shape[2]
    xslot = _tile_ring_step(x_hbm, xbuf, xsem)
    grp = 2 * LANES
    ngrp = tnw // grp

    @pl.when(tv_ref[m] != 0)
    def _():
        @pl.when(tf_ref[m] != 0)
        def _():
            slot = _weight_ring_step(rt, w_hbm, wbuf, sem, tnw)
            src = lax.broadcasted_iota(I32, (grp, grp), 0)
            dst = lax.broadcasted_iota(I32, (grp, grp), 1)
            want = jnp.where(dst < LANES, 2 * dst, 2 * (dst - LANES) + 1)
            perm = jnp.where(src == want, 1.0, 0.0).astype(BF16)
            for g in range(ngrp):
                wt = wbuf[slot, :, g * grp:(g + 1) * grp].astype(BF16)
                wc_scr[:, g * grp:(g + 1) * grp] = jnp.dot(wt, perm, preferred_element_type=F32).astype(BF16)

        x = xbuf[xslot].astype(BF16)
        bg = bg_ref[0]
        bu = bu_ref[0]
        for g in range(ngrp):
            gu = jnp.dot(x, wc_scr[:, g * grp:(g + 1) * grp], preferred_element_type=F32)
            gate = jnp.minimum(gu[:, 0:LANES] + bg[:, g * LANES:(g + 1) * LANES], SWIGLU_LIMIT)
            up = jnp.clip(gu[:, LANES:grp] + bu[:, g * LANES:(g + 1) * LANES], -SWIGLU_LIMIT, SWIGLU_LIMIT)
            act_ref[:, g * LANES:(g + 1) * LANES] = (
                (up + 1.0) * (gate * _sigmoid_t(gate * SWIGLU_ALPHA))).astype(act_ref.dtype)

    @pl.when(tv_ref[m] == 0)
    def _():
        act_ref[...] = jnp.zeros(act_ref.shape, act_ref.dtype)


def _moe_gate_up(rt, xs, w_gate_up, bg, bu):
    tm, tnw = MOE_TM, MOE_TNW
    P, D = xs.shape
    F2 = w_gate_up.shape[2]
    imap_b = lambda n, m, te, *_: (te[m], 0, n)
    return pl.pallas_call(
        _moe_gate_up_kernel,
        grid_spec=pltpu.PrefetchScalarGridSpec(
            num_scalar_prefetch=len(rt),
            grid=(F2 // tnw, P // tm),
            in_specs=[
                pl.BlockSpec(memory_space=pl.ANY),
                pl.BlockSpec(memory_space=pl.ANY),
                pl.BlockSpec((1, 1, tnw // 2), imap_b),
                pl.BlockSpec((1, 1, tnw // 2), imap_b),
            ],
            out_specs=pl.BlockSpec((tm, tnw // 2), lambda n, m, *_: (m, n)),
            scratch_shapes=[pltpu.VMEM((2, D, tnw), F32), pltpu.VMEM((D, tnw), BF16),
                            pltpu.SemaphoreType.DMA((2,)),
                            pltpu.VMEM((MOE_XSLOTS, tm, D), xs.dtype), pltpu.SemaphoreType.DMA((MOE_XSLOTS,))],
        ),
        out_shape=jax.ShapeDtypeStruct((P, F2 // 2), BF16),
        compiler_params=_cparams(("arbitrary", "arbitrary")),
        name="moe_gate_up",
    )(*rt, xs, w_gate_up, bg, bu)


def _moe_down_kernel(te_ref, tv_ref, tf_ref, tnx_ref, tlast_ref, trun_ref, nr_ref,
                     a_hbm, w_hbm, bd_ref, y_ref, wbuf, wc_scr, sem, abuf, asem):
    rt = (te_ref, tv_ref, tf_ref, tnx_ref, tlast_ref, trun_ref, nr_ref)
    m = pl.program_id(1)
    aslot = _tile_ring_step(a_hbm, abuf, asem)

    @pl.when(tv_ref[m] != 0)
    def _():
        @pl.when(tf_ref[m] != 0)
        def _():
            slot = _weight_ring_step(rt, w_hbm, wbuf, sem, wbuf.shape[2])
            wc_scr[...] = wbuf[slot].astype(BF16)

        y_ref[...] = jnp.dot(abuf[aslot], wc_scr[...], preferred_element_type=F32) + bd_ref[0]

    @pl.when(tv_ref[m] == 0)
    def _():
        y_ref[...] = jnp.zeros(y_ref.shape, y_ref.dtype)


def _moe_down(rt, act, w_down, bd):
    tm, tn = MOE_TM, MOE_TN
    P, F = act.shape
    D = w_down.shape[2]
    return pl.pallas_call(
        _moe_down_kernel,
        grid_spec=pltpu.PrefetchScalarGridSpec(
            num_scalar_prefetch=len(rt),
            grid=(D // tn, P // tm),
            in_specs=[
                pl.BlockSpec(memory_space=pl.ANY),
                pl.BlockSpec(memory_space=pl.ANY),
                pl.BlockSpec((1, 1, tn), lambda n, m, te, *_: (te[m], 0, n)),
            ],
            out_specs=pl.BlockSpec((tm, tn), lambda n, m, *_: (m, n)),
            scratch_shapes=[pltpu.VMEM((2, F, tn), F32), pltpu.VMEM((F, tn), BF16),
                            pltpu.SemaphoreType.DMA((2,)),
                            pltpu.VMEM((MOE_XSLOTS, tm, F), act.dtype), pltpu.SemaphoreType.DMA((MOE_XSLOTS,))],
        ),
        out_shape=jax.ShapeDtypeStruct((P, D), F32),
        compiler_params=_cparams(("arbitrary", "arbitrary")),
        name="moe_down",
    )(*rt, act, w_down, bd)


def _combine_kernel(x1_ref, yg_ref, tw_ref, fn_ref, y_ref):
    tw = tw_ref[...]
    moe = tw[:, 0:1] * yg_ref[0]
    for kk in range(1, TOP_K):
        moe = moe + tw[:, kk:kk + 1] * yg_ref[kk]
    y_ref[...] = _rms(x1_ref[...] + moe, fn_ref[...])


def _combine(x1, yg, tw, final_norm, tm, row_off):
    T, D = x1.shape
    off = row_off // tm
    return pl.pallas_call(
        _combine_kernel,
        grid=(T // tm,),
        in_specs=[
            pl.BlockSpec((tm, D), lambda i: (i, 0)),
            pl.BlockSpec((TOP_K, tm, D), lambda i: (0, i + off, 0)),
            pl.BlockSpec((tm, LANES), lambda i: (i, 0)),
            pl.BlockSpec((1, D), lambda i: (0, 0)),
        ],
        out_specs=pl.BlockSpec((tm, D), lambda i: (i, 0)),
        out_shape=jax.ShapeDtypeStruct((T, D), F32),
        compiler_params=_cparams(("parallel",)),
        name="combine",
    )(x1, yg, tw, final_norm)


def _routing(top_i):
    T = top_i.shape[0]
    P = T * TOP_K
    tm = MOE_TM
    n_tiles = -(-P // tm) + N_EXPERTS
    e_flat = top_i.reshape(P)
    onehot = (e_flat[:, None] == jnp.arange(N_EXPERTS, dtype=I32)[None, :]).astype(I32)
    csum = jnp.cumsum(onehot, axis=0)
    rank = jnp.sum((csum - 1) * onehot, axis=1)
    counts = csum[-1]
    padded = ((counts + tm - 1) // tm) * tm
    ends = jnp.cumsum(padded)
    starts = ends - padded
    pos = starts[e_flat] + rank
    tile_start = jnp.arange(n_tiles, dtype=I32) * tm
    tile_valid = (tile_start < ends[-1]).astype(I32)
    tile_expert = jnp.minimum(jnp.sum((tile_start[:, None] >= ends[None, :]).astype(I32), axis=1), N_EXPERTS - 1)
    tile_expert = tile_expert.astype(I32)
    order = jnp.argsort(e_flat, stable=True).astype(I32)
    tok_sorted = jnp.concatenate([order // TOP_K, jnp.zeros((tm,), I32)])
    group_start = jnp.cumsum(counts) - counts
    tile_src = jnp.where(tile_valid != 0, group_start[tile_expert] + tile_start - starts[tile_expert], 0)
    slot_token = jax.vmap(lambda o: lax.dynamic_slice(tok_sorted, (o,), (tm,)))(tile_src).reshape(-1)
    tile_first = jnp.concatenate([jnp.ones((1,), I32), (tile_expert[1:] != tile_expert[:-1]).astype(I32)])
    tile_first = tile_first * tile_valid
    tile_run = jnp.cumsum(tile_first) - 1
    n_runs = jnp.sum(tile_first).reshape(1)
    experts = jnp.arange(N_EXPERTS, dtype=I32)
    later = (counts > 0)[None, :] & (experts[None, :] > tile_expert[:, None])
    nxt = jnp.min(jnp.where(later, experts[None, :], N_EXPERTS), axis=1)
    tile_last = (nxt == N_EXPERTS).astype(I32)
    tile_next = jnp.where(nxt == N_EXPERTS, tile_expert[0], nxt).astype(I32)
    tables = (tile_expert, tile_valid, tile_first, tile_next, tile_last, tile_run.astype(I32), n_runs.astype(I32))
    return pos, slot_token, tables


def kernel(x_prompt, x_sample, state_gdn, state_conv, state_hgrn, lb_table, norm1, w_in, conv_w, A_log, dt_bias,
           gdn_norm, hg_norm, w_out, norm2, w_router, b_router, w_gate_up, b_gate_up, w_down, b_down, final_norm):
    depth = w_in.shape[0]
    assert depth == 1
    B, L, D = x_prompt.shape
    SB = x_sample.shape[0]
    assert x_sample.shape[1] == 1
    nqk = HEADS * HD
    cd = 3 * nqk

    lbs = jnp.cumsum(jax.nn.softmax(lb_table.astype(F32), axis=0), axis=0)
    lb = lbs[0:1]

    assert w_in.shape[2] == MAIN_COLS + 2 * HEADS and cd + nqk == MAIN_COLS // 2
    ba0 = cd + nqk
    w_t = jnp.swapaxes(w_in[0], 0, 1).astype(BF16)
    wo_a = w_out[0, :nqk].astype(BF16)
    wo_b = w_out[0, nqk:].astype(BF16)
    wr_hi, wr_lo = _split(jnp.pad(w_router[0], ((0, 0), (0, LANES - N_EXPERTS))))
    br_p = jnp.pad(b_router[0], (0, LANES - N_EXPERTS))[None, :]
    bg = b_gate_up[0, :, None, 0::2]
    bu = b_gate_up[0, :, None, 1::2]
    bd = b_down[0][:, None, :]
    alog_b = jnp.pad(A_log[0], (HEADS, LANES - 2 * HEADS))[None, :]
    dtb_b = jnp.pad(dt_bias[0], (HEADS, LANES - 2 * HEADS))[None, :]
    alog_row = jnp.pad(A_log[0], (0, LANES - HEADS))[None, :]
    dtb_row = jnp.pad(dt_bias[0], (0, LANES - HEADS))[None, :]
    n1 = norm1[0][None, :]
    n2 = norm2[0][None, :]
    gn = gdn_norm[0][None, :]
    hn = hg_norm[0][None, :]
    fnw = final_norm[None, :]
    cw = conv_w[0]

    xp = x_prompt.reshape(B * L, D)
    xs = x_sample.reshape(SB, D)

    proj_p, ba_p = _in_proj(xp, n1, w_t, ba0, 2 * HEADS, tm=1024, tn=1024)
    proj_s, ba_s = _in_proj(xs, n1, w_t, ba0, 2 * HEADS, tm=SB, tn=1024)

    oa_p, sg_p = _gdn_prompt(proj_p, ba_p, cw, alog_b, dtb_b, gn, B, L)
    ob_p, sh_p = _hgrn_prompt(proj_p, lb, hn, B, L)
    oa_s, ob_s, sg_s, sh_s = _decode(proj_s, ba_s, state_conv[0].reshape(SB, (CONV_K - 1) * cd), state_gdn[0],
                                     state_hgrn[0], cw, alog_row, dtb_row, lb, gn, hn)

    ntok = B * L + SB
    x1_p, h2, ti, tw_p = _out_router(xp, oa_p, ob_p, wo_a, wo_b, n2, wr_hi, wr_lo, br_p,
                                     tm=512, total_rows=ntok, row_off=0)
    x1_s, h2, ti, tw_s = _out_router(xs, oa_s, ob_s, wo_a, wo_b, n2, wr_hi, wr_lo, br_p,
                                     tm=SB, total_rows=ntok, row_off=B * L, into=(h2, ti))

    top_i = ti[:, :TOP_K]
    pos, slot_token, tables = _routing(top_i)
    xs_sorted = h2.at[slot_token].get(mode="promise_in_bounds")
    act = _moe_gate_up(tables, xs_sorted, w_gate_up[0], bg, bu)
    yslots = _moe_down(tables, act, w_down[0], bd)
    pos_kmajor = pos.reshape(B * L + SB, TOP_K).T.reshape(-1)
    yg = yslots.at[pos_kmajor].get(mode="promise_in_bounds").reshape(TOP_K, B * L + SB, D)

    y_p = _combine(x1_p, yg, tw_p, fnw, tm=256, row_off=0)
    y_s = _combine(x1_s, yg, tw_s, fnw, tm=SB, row_off=B * L)

    conv_p = proj_p.reshape(B, L, MAIN_COLS)[:, L - (CONV_K - 1):, :cd]
    conv_s = jnp.concatenate([state_conv[0][:, 1:, :], proj_s[:, None, :cd]], axis=1)
    return (y_p.reshape(B, L, D), y_s.reshape(SB, 1, D),
            sg_p[None], conv_p[None].astype(state_conv.dtype), sh_p[None],
            sg_s[None], conv_s[None].astype(state_conv.dtype), sh_s[None])
```

```python
import functools

import jax
import jax.numpy as jnp
from jax import lax
from jax.experimental import pallas as pl
from jax.experimental.pallas import tpu as pltpu

F32 = jnp.float32
BF16 = jnp.bfloat16
I32 = jnp.int32

EPS = 1e-6
HEADS = 8
HD = 128
CONV_K = 4
N_EXPERTS = 32
TOP_K = 4
SWIGLU_ALPHA = 1.702
SWIGLU_LIMIT = 7.0
LANES = 128
NEG = -1e30

VMEM_LIMIT = 56 * 1024 * 1024

QB, KB, VB, ZB, HQB, HFB, HIB, HGB = (i * HEADS for i in range(8))
MAIN_COLS = 8 * HEADS * HD

HEADS_PER_STEP = 2
GDN_CHUNK = 128
GDN_TB = 1024
HG_CHUNK = 16
HG_TB = 1024
HG_GROUP = 4
DEC_TB = 8
MOE_TM = 256
MOE_TN = 2048
MOE_TNW = 2048
MOE_XSLOTS = 3


def _cparams(sem):
    return pltpu.CompilerParams(dimension_semantics=sem, vmem_limit_bytes=VMEM_LIMIT)


def _mm(a, b):
    return jnp.dot(a.astype(BF16), b.astype(BF16), preferred_element_type=F32)


def _mm_nt(a, b):
    return lax.dot_general(a.astype(BF16), b.astype(BF16), (((1,), (1,)), ((), ())),
                           preferred_element_type=F32)


def _mm_tn(a, b):
    return lax.dot_general(a.astype(BF16), b.astype(BF16), (((0,), (0,)), ((), ())),
                           preferred_element_type=F32)


def _mmh(a, b):
    return jnp.dot(a, b, precision=lax.Precision.HIGHEST, preferred_element_type=F32)


def _split(a):
    hi = a.astype(BF16)
    return hi, (a - hi.astype(F32)).astype(BF16)


def _mm3(a, b):
    d = lambda x, y: jnp.dot(x, y, preferred_element_type=F32)
    return d(a[0], b[0]) + (d(a[0], b[1]) + d(a[1], b[0]))


def _mm_exact_lhs(l_bf16, x):
    d = lambda y: jnp.dot(l_bf16, y, preferred_element_type=F32)
    x0 = x.astype(BF16)
    r1 = x - x0.astype(F32)
    x1 = r1.astype(BF16)
    x2 = (r1 - x1.astype(F32)).astype(BF16)
    return d(x0) + (d(x1) + d(x2))


def _sigmoid(x):
    return 1.0 / (1.0 + jnp.exp(-x))


def _sigmoid_t(x):
    return 0.5 * jnp.tanh(0.5 * x) + 0.5


def _silu(x):
    return x * _sigmoid_t(x)


def _softplus(x):
    return jnp.maximum(x, 0.0) + jnp.log1p(jnp.exp(-jnp.abs(x)))


def _rms(x, w):
    return x * lax.rsqrt(jnp.mean(x * x, axis=-1, keepdims=True) + EPS) * w


def _l2n(x):
    return x * lax.rsqrt(jnp.sum(x * x, axis=-1, keepdims=True) + EPS)


def _in_proj_kernel(x_ref, nw_ref, wt_ref, wbat_ref, o_ref, ba_ref, h_scr):
    nt_dims = (((1,), (1,)), ((), ()))

    @pl.when(pl.program_id(1) == 0)
    def _():
        h = _rms(x_ref[...], nw_ref[...]).astype(BF16)
        h_scr[...] = h
        ba_ref[...] = lax.dot_general(h, wbat_ref[...], nt_dims, preferred_element_type=F32)

    o_ref[...] = lax.dot_general(h_scr[...], wt_ref[...], nt_dims, preferred_element_type=F32)


def _in_proj(x, norm_w, w_t, ba0, nba, tm, tn):
    T, D = x.shape
    N = w_t.shape[0] - nba
    assert ba0 % tn == 0 and nba % 16 == 0
    return pl.pallas_call(
        _in_proj_kernel,
        grid=(T // tm, N // tn),
        in_specs=[
            pl.BlockSpec((tm, D), lambda i, j: (i, 0)),
            pl.BlockSpec((1, D), lambda i, j: (0, 0)),
            pl.BlockSpec((pl.Element(tn), pl.Element(D)),
                         lambda i, j: (pl.multiple_of(j * tn + jnp.where(j * tn >= ba0, nba, 0), 16), 0)),
            pl.BlockSpec((pl.Element(LANES), pl.Element(D)), lambda i, j: (ba0, 0)),
        ],
        out_specs=[
            pl.BlockSpec((tm, tn), lambda i, j: (i, j)),
            pl.BlockSpec((tm, LANES), lambda i, j: (i, 0)),
        ],
        out_shape=[jax.ShapeDtypeStruct((T, N), F32), jax.ShapeDtypeStruct((T, LANES), F32)],
        scratch_shapes=[pltpu.VMEM((tm, D), BF16)],
        compiler_params=_cparams(("parallel", "arbitrary")),
        name="in_proj",
    )(x, norm_w, w_t, w_t)


def _unit_lower_inverse(As, ri, ci):
    n = As[0].shape[0]
    eye = (ri == ci).astype(F32)
    same16 = (ri // 16) == (ci // 16)
    s1 = [_split(jnp.where(same16, -A, 0.0)) for A in As]
    s2 = [_split(_mm3(s, s)) for s in s1]
    s4 = [_split(_mm3(s, s)) for s in s2]
    s8 = [_split(_mm3(s, s)) for s in s4]
    Ts = [eye + jnp.where(same16, -A, 0.0) for A in As]
    for sp in (s2, s4, s8):
        Ts = [T + _mm3(_split(T), s) for T, s in zip(Ts, sp)]
    size = 32
    while size <= n:
        off = ((ri // size) == (ci // size)) & ((ri // (size // 2)) != (ci // (size // 2)))
        bT = [T.astype(BF16) for T in Ts]
        TL = [_mm(bt, jnp.where(off, A, 0.0)) for bt, A in zip(bT, As)]
        Ts = [T - _mm(tl, bt) for T, tl, bt in zip(Ts, TL, bT)]
        size *= 2
    return Ts


def _gdn_prompt_kernel(q_ref, k_ref, v_ref, z_ref, ba_ref, cwq_ref, cwk_ref, cwv_ref, alog_ref, dtb_ref,
                       gn_ref, o_ref, s_ref, ubuf, s_scr, *, tb, chunk, hps):
    hg = pl.program_id(1)
    t = pl.program_id(2)

    @pl.when(t == 0)
    def _():
        ubuf[:, 0:8, :] = jnp.zeros((3, 8, hps * HD), F32)
        s_scr[...] = jnp.zeros((hps, HD, HD), F32)

    ubuf[0, 8:8 + tb, :] = q_ref[...]
    ubuf[1, 8:8 + tb, :] = k_ref[...]
    ubuf[2, 8:8 + tb, :] = v_ref[...]

    ba = ba_ref[...]
    lane = lax.broadcasted_iota(I32, (chunk, LANES), 1)
    beta_all = _sigmoid_t(ba)
    g_all = -jnp.exp(alog_ref[...]) * _softplus(ba + dtb_ref[...])

    ri = lax.broadcasted_iota(I32, (chunk, chunk), 0)
    ci = lax.broadcasted_iota(I32, (chunk, chunk), 1)
    causal = ci <= ri
    strict = ci < ri
    ltri = jnp.where(causal, 1.0, 0.0).astype(BF16)

    def conv(idx, w_ref, r0, c0):
        w = w_ref[:, c0:c0 + HD]
        acc = ubuf[idx, r0 + 5:r0 + 5 + chunk, c0:c0 + HD] * w[0:1, :]
        for j in range(1, CONV_K):
            acc = acc + ubuf[idx, r0 + 5 + j:r0 + 5 + j + chunk, c0:c0 + HD] * w[j:j + 1, :]
        return _silu(acc)

    nc = tb // chunk
    items = [(hh, c) for c in range(nc) for hh in range(hps)]
    qs, ks, vs, betas, gcols, decays, As = [], [], [], [], [], [], []
    gc_all = [_mm_exact_lhs(ltri, g_all[c * chunk:(c + 1) * chunk, :]) for c in range(nc)]
    for hh, c in items:
        r0, c0 = c * chunk, hh * HD
        h = hg * hps + hh
        q = _l2n(conv(0, cwq_ref, r0, c0)) * (HD ** -0.5)
        k = _l2n(conv(1, cwk_ref, r0, c0))
        v = conv(2, cwv_ref, r0, c0)
        beta = jnp.sum(jnp.where(lane == h, beta_all[r0:r0 + chunk, :], 0.0), axis=-1, keepdims=True)
        gcol = jnp.broadcast_to(
            jnp.sum(jnp.where(lane == h + HEADS, gc_all[c], 0.0), axis=-1, keepdims=True), (chunk, HD))
        decay = jnp.exp(jnp.where(causal, gcol - gcol.T, NEG))
        kb = k * beta
        qs.append(q); ks.append(k); vs.append(v); betas.append(beta); gcols.append(gcol); decays.append(decay)
        As.append(jnp.where(strict, _mm_nt(kb, k) * decay, 0.0))
    Ts = _unit_lower_inverse(As, ri, ci)
    us, ws, scs = [], [], []
    for n in range(len(items)):
        sT = _split(Ts[n])
        kb = ks[n] * betas[n]
        us.append(_mm3(sT, _split(vs[n] * betas[n])))
        ws.append(_mm3(sT, _split(kb * jnp.exp(gcols[n]))))
        scs.append(_mm_nt(qs[n], ks[n]) * decays[n])

    S = [s_scr[hh] for hh in range(hps)]
    for n, (hh, c) in enumerate(items):
        r0, c0 = c * chunk, hh * HD
        gcol = gcols[n]
        v_new = us[n] - _mm(ws[n], S[hh])
        o = _mm(qs[n] * jnp.exp(gcol), S[hh]) + _mm(scs[n], v_new)
        glast = gcol[chunk - 1:chunk, :]
        S[hh] = S[hh] * jnp.exp(glast) + _mm_tn(ks[n] * jnp.exp(glast - gcol), v_new)
        o_ref[r0:r0 + chunk, c0:c0 + HD] = (
            _rms(o, gn_ref[...]) * _silu(z_ref[r0:r0 + chunk, c0:c0 + HD])).astype(o_ref.dtype)
    for hh in range(hps):
        s_scr[hh] = S[hh]

    ubuf[:, 0:8, :] = ubuf[:, tb:tb + 8, :]

    @pl.when(t == pl.num_programs(2) - 1)
    def _():
        for hh in range(hps):
            s_ref[0, hh] = S[hh]


def _gdn_prompt(proj, ba, conv_w, alog_b, dtb_b, gdn_norm, B, L):
    tb, chunk, hps = GDN_TB, GDN_CHUNK, HEADS_PER_STEP
    nt = L // tb
    w = hps * HD
    row = lambda b, h, t: b * nt + t
    colspec = lambda base: pl.BlockSpec((tb, w), lambda b, h, t: (row(b, h, t), base // hps + h))
    cwspec = lambda base: pl.BlockSpec((CONV_K, w), lambda b, h, t: (0, base // hps + h))
    hvec = pl.BlockSpec((1, LANES), lambda b, h, t: (0, 0))
    return pl.pallas_call(
        functools.partial(_gdn_prompt_kernel, tb=tb, chunk=chunk, hps=hps),
        grid=(B, HEADS // hps, nt),
        in_specs=[
            colspec(QB), colspec(KB), colspec(VB), colspec(ZB),
            pl.BlockSpec((tb, LANES), lambda b, h, t: (row(b, h, t), 0)),
            cwspec(QB), cwspec(KB), cwspec(VB),
            hvec, hvec,
            pl.BlockSpec((1, HD), lambda b, h, t: (0, 0)),
        ],
        out_specs=[
            pl.BlockSpec((tb, w), lambda b, h, t: (row(b, h, t), h)),
            pl.BlockSpec((1, hps, HD, HD), lambda b, h, t: (b, h, 0, 0)),
        ],
        out_shape=[jax.ShapeDtypeStruct((B * L, HEADS * HD), BF16),
                   jax.ShapeDtypeStruct((B, HEADS, HD, HD), F32)],
        scratch_shapes=[pltpu.VMEM((3, tb + 8, w), F32), pltpu.VMEM((hps, HD, HD), F32)],
        compiler_params=_cparams(("parallel", "parallel", "arbitrary")),
        name="gdn_prompt",
    )(proj, proj, proj, proj, ba, conv_w, conv_w, conv_w, alog_b, dtb_b, gdn_norm)


def _hgrn_prompt_kernel(hq_ref, hf_ref, hi_ref, hg_ref, lb_ref, hn_ref, o_ref, s_ref,
                        g_scr, q_scr, k_scr, st_scr, *, tb, chunk, group, hps):
    t = pl.program_id(2)

    @pl.when(t == 0)
    def _():
        st_scr[...] = jnp.zeros((hps, HD, HD), F32)

    lb = lb_ref[...]
    hf = hf_ref[...]
    forget = lb + (1.0 - lb) * _sigmoid(hf)
    k_scr[...] = (1.0 - lb) * _sigmoid(-hf)
    q_scr[...] = _silu(hq_ref[...])
    ri = lax.broadcasted_iota(I32, (LANES, LANES), 0)
    ci = lax.broadcasted_iota(I32, (LANES, LANES), 1)
    lblk = jnp.where(((ri // chunk) == (ci // chunk)) & (ci <= ri), 1.0, 0.0).astype(BF16)
    lf = jnp.log(forget)
    for r in range(tb // LANES):
        g_scr[r * LANES:(r + 1) * LANES, :] = _mm_exact_lhs(lblk, lf[r * LANES:(r + 1) * LANES, :])

    half = chunk // 2
    row8 = lax.broadcasted_iota(I32, (half, HD), 0)
    lane8 = lax.broadcasted_iota(I32, (half, HD), 1)
    zpad = jnp.zeros((half, HD), F32)
    hn = hn_ref[...]

    def body(gi, sts_in):
        base = gi * (group * chunk)
        items = [(hh, cc) for cc in range(group) for hh in range(hps)]
        ni = range(len(items))
        rc = [(base + cc * chunk, hh * HD) for hh, cc in items]
        vs = [hi_ref[pl.ds(r0, chunk), c0:c0 + HD] for r0, c0 in rc]
        vts = [v.T for v in vs]
        blocks = []
        for r0, c0 in rc:
            for r in (r0, r0 + half):
                blocks.append((g_scr[pl.ds(r, half), c0:c0 + HD], q_scr[pl.ds(r, half), c0:c0 + HD],
                               k_scr[pl.ds(r, half), c0:c0 + HD]))
        accs = [zpad] * len(blocks)
        for j in range(half):
            for n, (Gb, qb, kb) in enumerate(blocks):
                e = jnp.exp(Gb - Gb[j:j + 1, :])
                col = jnp.sum(e * qb * kb[j:j + 1, :], axis=-1, keepdims=True)
                accs[n] = jnp.where(lane8 == (n % 2) * half + j, col, accs[n])
        accs = [jnp.where(row8 >= lane8 - (n % 2) * half, a, 0.0) for n, a in enumerate(accs)]
        Gs = [jnp.concatenate([blocks[2 * n][0], blocks[2 * n + 1][0]], axis=0) for n in ni]
        qs = [jnp.concatenate([blocks[2 * n][1], blocks[2 * n + 1][1]], axis=0) for n in ni]
        ks = [jnp.concatenate([blocks[2 * n][2], blocks[2 * n + 1][2]], axis=0) for n in ni]
        offd = []
        for n in ni:
            Ga, _, ka = blocks[2 * n]
            Gb, qb, _ = blocks[2 * n + 1]
            gmid = Ga[half - 1:half, :]
            kh = jnp.concatenate([ka * jnp.exp(gmid - Ga), zpad], axis=0)
            offd.append(_mm_nt(qb * jnp.exp(Gb - gmid), kh))
        glasts = [Gs[n][chunk - 1:chunk, :] for n in ni]
        incs = [_mm(vts[n], ks[n] * jnp.exp(glasts[n] - Gs[n])) for n in ni]
        intra = []
        for n in ni:
            s = jnp.concatenate([accs[2 * n][:, 0:chunk], accs[2 * n + 1][:, 0:chunk] + offd[n]], axis=0)
            intra.append(_mm(s, vs[n]))
        cur = list(sts_in)
        before = []
        for n, (hh, _) in enumerate(items):
            before.append(cur[hh])
            cur[hh] = cur[hh] * jnp.exp(glasts[n]) + incs[n]
        outs = [_mm_nt(qs[n] * jnp.exp(Gs[n]), before[n]) + intra[n] for n in ni]
        for n, (r0, c0) in enumerate(rc):
            o_ref[pl.ds(r0, chunk), c0:c0 + HD] = (
                _rms(outs[n], hn) * _silu(hg_ref[pl.ds(r0, chunk), c0:c0 + HD])).astype(o_ref.dtype)
        return cur

    sts = [st_scr[hh] for hh in range(hps)]
    for gi in range(tb // (group * chunk)):
        sts = body(gi, sts)
    for hh in range(hps):
        st_scr[hh] = sts[hh]

    @pl.when(t == pl.num_programs(2) - 1)
    def _():
        for hh in range(hps):
            s_ref[0, hh] = sts[hh].T


def _hgrn_prompt(proj, lb, hg_norm, B, L):
    tb, chunk, hps = HG_TB, HG_CHUNK, HEADS_PER_STEP
    nt = L // tb
    w = hps * HD
    row = lambda b, h, t: b * nt + t
    colspec = lambda base: pl.BlockSpec((tb, w), lambda b, h, t: (row(b, h, t), base // hps + h))
    return pl.pallas_call(
        functools.partial(_hgrn_prompt_kernel, tb=tb, chunk=chunk, group=HG_GROUP, hps=hps),
        grid=(B, HEADS // hps, nt),
        in_specs=[
            colspec(HQB), colspec(HFB), colspec(HIB), colspec(HGB),
            pl.BlockSpec((1, w), lambda b, h, t: (0, h)),
            pl.BlockSpec((1, HD), lambda b, h, t: (0, 0)),
        ],
        out_specs=[
            pl.BlockSpec((tb, w), lambda b, h, t: (row(b, h, t), h)),
            pl.BlockSpec((1, hps, HD, HD), lambda b, h, t: (b, h, 0, 0)),
        ],
        out_shape=[jax.ShapeDtypeStruct((B * L, HEADS * HD), BF16),
                   jax.ShapeDtypeStruct((B, HEADS, HD, HD), F32)],
        scratch_shapes=[pltpu.VMEM((tb, w), F32), pltpu.VMEM((tb, w), F32), pltpu.VMEM((tb, w), F32),
                        pltpu.VMEM((hps, HD, HD), F32)],
        compiler_params=_cparams(("parallel", "parallel", "arbitrary")),
        name="hgrn_prompt",
    )(proj, proj, proj, proj, lb, hg_norm)


def _column_broadcasts(x, sel):
    d = lambda y: lax.dot_general(y, sel, (((0,), (0,)), ((), ())), preferred_element_type=F32)
    hi, lo = _split(x)
    return d(hi) + d(lo)


def _decode_kernel(qkv_ref, z_ref, hq_ref, hf_ref, hi_ref, hg_ref, ba_ref, cs_ref, sg_ref, sh_ref,
                   cw_ref, alog_ref, dtb_ref, lb_ref, gn_ref, hn_ref,
                   oa_ref, ob_ref, sgo_ref, sho_ref, *, nb):
    cd = cw_ref.shape[1]
    cw = cw_ref[...]
    cs = cs_ref[...]
    acc = cs[:, 0:cd] * cw[0:1, :]
    acc = acc + cs[:, cd:2 * cd] * cw[1:2, :]
    acc = acc + cs[:, 2 * cd:3 * cd] * cw[2:3, :]
    acc = acc + qkv_ref[...] * cw[3:4, :]
    conv = _silu(acc)
    nqk = HEADS * HD
    ba = ba_ref[...]
    gn = gn_ref[...]
    hn = hn_ref[...]
    z = z_ref[...]
    hgate = hg_ref[...]

    qs, ks, vs = [], [], []
    for h in range(HEADS):
        qs.append(_l2n(conv[:, h * HD:(h + 1) * HD]) * (HD ** -0.5))
        ks.append(_l2n(conv[:, nqk + h * HD:nqk + (h + 1) * HD]))
        vs.append(conv[:, 2 * nqk + h * HD:2 * nqk + (h + 1) * HD])
    lb = lb_ref[...]
    hf = hf_ref[...]
    forget = lb + (1.0 - lb) * _sigmoid(hf)
    hk = (1.0 - lb) * _sigmoid(-hf)
    hq = _silu(hq_ref[...])
    hv = hi_ref[...]

    sel_r = lax.broadcasted_iota(I32, (nb, nb * HD), 0)
    sel_c = lax.broadcasted_iota(I32, (nb, nb * HD), 1)
    sel = jnp.where(sel_c // HD == sel_r, 1.0, 0.0).astype(BF16)

    for h in range(HEADS):
        hs = slice(h * HD, (h + 1) * HD)
        beta = _sigmoid(ba[:, h:h + 1])
        g = -jnp.exp(alog_ref[0:1, h:h + 1]) * _softplus(ba[:, HEADS + h:HEADS + h + 1] + dtb_ref[0:1, h:h + 1])
        eg = jnp.exp(g)
        qk = jnp.sum(qs[h] * ks[h], axis=-1, keepdims=True)
        kB = _column_broadcasts(ks[h], sel)
        qB = _column_broadcasts(qs[h], sel)
        fB = _column_broadcasts(forget[:, hs], sel)
        hkB = _column_broadcasts(hk[:, hs], sel)
        hqB = _column_broadcasts(hq[:, hs], sel)
        for b in range(nb):
            bs = slice(b * HD, (b + 1) * HD)
            S = sg_ref[b, h]
            kS = jnp.sum(S * kB[:, bs], axis=0, keepdims=True)
            qS = jnp.sum(S * qB[:, bs], axis=0, keepdims=True)
            egb = eg[b:b + 1, :]
            v_new = beta[b:b + 1, :] * (vs[h][b:b + 1, :] - egb * kS)
            o = egb * qS + qk[b:b + 1, :] * v_new
            sgo_ref[b, h] = S * egb + kB[:, bs] * v_new
            oa_ref[b:b + 1, hs] = _rms(o, gn) * _silu(z[b:b + 1, hs])
            Sh = sh_ref[b, h] * fB[:, bs] + hkB[:, bs] * hv[b:b + 1, hs]
            sho_ref[b, h] = Sh
            ob = jnp.sum(Sh * hqB[:, bs], axis=0, keepdims=True)
            ob_ref[b:b + 1, hs] = _rms(ob, hn) * _silu(hgate[b:b + 1, hs])


def _decode(proj, ba, conv_state2d, s_gdn, s_hg, conv_w, alog_row, dtb_row, lb, gdn_norm, hg_norm):
    nb = DEC_TB
    T = proj.shape[0]
    cd = conv_w.shape[1]
    w = HEADS * HD
    cblk = lambda width, idx: pl.BlockSpec((nb, width), lambda i: (i, idx))
    sblk = pl.BlockSpec((nb, HEADS, HD, HD), lambda i: (i, 0, 0, 0))
    full = lambda shape: pl.BlockSpec(shape, lambda i: (0,) * len(shape))
    return pl.pallas_call(
        functools.partial(_decode_kernel, nb=nb),
        grid=(T // nb,),
        in_specs=[
            cblk(cd, 0), cblk(w, ZB // HEADS), cblk(w, HQB // HEADS), cblk(w, HFB // HEADS),
            cblk(w, HIB // HEADS), cblk(w, HGB // HEADS),
            cblk(LANES, 0), cblk(3 * cd, 0), sblk, sblk,
            full((CONV_K, cd)), full((1, LANES)), full((1, LANES)), full((1, w)), full((1, HD)), full((1, HD)),
        ],
        out_specs=[cblk(w, 0), cblk(w, 0), sblk, sblk],
        out_shape=[jax.ShapeDtypeStruct((T, w), F32), jax.ShapeDtypeStruct((T, w), F32),
                   jax.ShapeDtypeStruct(s_gdn.shape, F32), jax.ShapeDtypeStruct(s_hg.shape, F32)],
        compiler_params=_cparams(("parallel",)),
        name="decode",
    )(proj, proj, proj, proj, proj, proj, ba, conv_state2d, s_gdn, s_hg,
      conv_w, alog_row, dtb_row, lb, gdn_norm, hg_norm)


def _out_router_kernel(x_ref, oa_ref, ob_ref, woa_ref, wob_ref, n2_ref, wrh_ref, wrl_ref, br_ref,
                       x1_ref, h2_ref, ti_ref, tw_ref, *, nvalid):
    i = pl.program_id(0)

    @pl.when(i >= nvalid)
    def _():
        h2_ref[...] = jnp.zeros(h2_ref.shape, h2_ref.dtype)
        ti_ref[...] = jnp.zeros(ti_ref.shape, ti_ref.dtype)

    @pl.when(i < nvalid)
    def _():
        _out_router_tile(x_ref, oa_ref, ob_ref, woa_ref, wob_ref, n2_ref, wrh_ref, wrl_ref, br_ref,
                         x1_ref, h2_ref, ti_ref, tw_ref)


def _out_router_tile(x_ref, oa_ref, ob_ref, woa_ref, wob_ref, n2_ref, wrh_ref, wrl_ref, br_ref,
                     x1_ref, h2_ref, ti_ref, tw_ref):
    y = x_ref[...] + jnp.dot(oa_ref[...].astype(BF16), woa_ref[...], preferred_element_type=F32)
    y = y + jnp.dot(ob_ref[...].astype(BF16), wob_ref[...], preferred_element_type=F32)
    x1_ref[...] = y
    h2 = _rms(y, n2_ref[...])
    h2_ref[...] = h2
    logits = _mm3(_split(h2), (wrh_ref[...], wrl_ref[...])) + br_ref[...]
    lane = lax.broadcasted_iota(I32, logits.shape, 1)
    logits = jnp.where(lane < N_EXPERTS, logits, NEG)
    ti = jnp.zeros(logits.shape, I32)
    tw = jnp.zeros(logits.shape, F32)
    m0 = None
    for kk in range(TOP_K):
        m = jnp.max(logits, axis=-1, keepdims=True)
        idx = jnp.min(jnp.where(logits == m, lane, LANES), axis=-1, keepdims=True)
        if m0 is None:
            m0 = m
        ti = jnp.where(lane == kk, idx, ti)
        tw = jnp.where(lane == kk, jnp.exp(m - m0), tw)
        logits = jnp.where(lane == idx, NEG * 2.0, logits)
    tw_ref[...] = tw / jnp.sum(tw, axis=-1, keepdims=True)
    ti_ref[...] = ti


def _out_router_into_kernel(h2_all_ref, ti_all_ref, *refs, nvalid):
    del h2_all_ref, ti_all_ref
    _out_router_kernel(*refs, nvalid=nvalid)


def _out_router(x, oa, ob, wo_a, wo_b, norm2, wr_hi, wr_lo, b_router_p, tm, total_rows, row_off, into=None):
    T, D = x.shape
    w = oa.shape[1]
    off = row_off // tm
    nvalid = T // tm
    steps = nvalid if into is not None else -(-total_rows // tm)
    rowblk = lambda width: pl.BlockSpec((tm, width), lambda i: (jnp.minimum(i, nvalid - 1), 0))
    allblk = lambda width: pl.BlockSpec((tm, width), lambda i: (i + off, 0))
    full = lambda shape: pl.BlockSpec(shape, lambda i: (0,) * len(shape))
    in_specs = [rowblk(D), rowblk(w), rowblk(w), full((w, D)), full((w, D)), full((1, D)),
                full((D, LANES)), full((D, LANES)), full((1, LANES))]
    args = (x, oa, ob, wo_a, wo_b, norm2, wr_hi, wr_lo, b_router_p)
    body, aliases = _out_router_kernel, {}
    if into is not None:
        in_specs = [pl.BlockSpec(memory_space=pl.ANY)] * 2 + in_specs
        args = tuple(into) + args
        body, aliases = _out_router_into_kernel, {0: 1, 1: 2}
    return pl.pallas_call(
        functools.partial(body, nvalid=nvalid),
        grid=(steps,),
        in_specs=in_specs,
        out_specs=[rowblk(D), allblk(D), allblk(LANES), rowblk(LANES)],
        out_shape=[jax.ShapeDtypeStruct((T, D), F32), jax.ShapeDtypeStruct((total_rows, D), F32),
                   jax.ShapeDtypeStruct((total_rows, LANES), I32), jax.ShapeDtypeStruct((T, LANES), F32)],
        input_output_aliases=aliases,
        compiler_params=_cparams(("arbitrary",)),
        name="out_router",
    )(*args)


def _weight_ring_step(rt, w_hbm, wbuf, sem, tcols):
    te_ref, tv_ref, tf_ref, tnx_ref, tlast_ref, trun_ref, nr_ref = rt
    n = pl.program_id(0)
    m = pl.program_id(1)
    slot = lax.rem(n * nr_ref[0] + trun_ref[m], 2)

    def copy(e, nn, s):
        c0 = pl.multiple_of(nn * tcols, tcols)
        return pltpu.make_async_copy(w_hbm.at[e, :, pl.ds(c0, tcols)], wbuf.at[s], sem.at[s])

    @pl.when((n == 0) & (m == 0))
    def _():
        copy(te_ref[m], n, slot).start()

    copy(te_ref[m], n, slot).wait()
    last = tlast_ref[m] != 0

    @pl.when(jnp.logical_or(jnp.logical_not(last), n + 1 < pl.num_programs(0)))
    def _():
        copy(tnx_ref[m], jnp.where(last, n + 1, n), 1 - slot).start()

    return slot


def _tile_ring_step(x_hbm, xbuf, xsem):
    n = pl.program_id(0)
    m = pl.program_id(1)
    n_m = pl.num_programs(1)
    total = pl.num_programs(0) * n_m
    tm = xbuf.shape[1]
    nslot = xbuf.shape[0]
    s = n * n_m + m

    def copy(step):
        r0 = pl.multiple_of(lax.rem(step, n_m) * tm, tm)
        slot = lax.rem(step, nslot)
        return pltpu.make_async_copy(x_hbm.at[pl.ds(r0, tm), :], xbuf.at[slot], xsem.at[slot])

    @pl.when(s == 0)
    def _():
        for first in range(nslot - 1):
            copy(first).start()

    @pl.when(s + nslot - 1 < total)
    def _():
        copy(s + nslot - 1).start()

    copy(s).wait()
    return lax.rem(s, nslot)


def _moe_gate_up_kernel(te_ref, tv_ref, tf_ref, tnx_ref, tlast_ref, trun_ref, nr_ref,
                        x_hbm, w_hbm, bg_ref, bu_ref, act_ref, wbuf, wc_scr, sem, xbuf, xsem):
    rt = (te_ref, tv_ref, tf_ref, tnx_ref, tlast_ref, trun_ref, nr_ref)
    m = pl.program_id(1)
    tnw = wbuf.shape[2]
    xslot = _tile_ring_step(x_hbm, xbuf, xsem)
    grp = 2 * LANES
    ngrp = tnw // grp

    @pl.when(tv_ref[m] != 0)
    def _():
        @pl.when(tf_ref[m] != 0)
        def _():
            slot = _weight_ring_step(rt, w_hbm, wbuf, sem, tnw)
            src = lax.broadcasted_iota(I32, (grp, grp), 0)
            dst = lax.broadcasted_iota(I32, (grp, grp), 1)
            want = jnp.where(dst < LANES, 2 * dst, 2 * (dst - LANES) + 1)
            perm = jnp.where(src == want, 1.0, 0.0).astype(BF16)
            for g in range(ngrp):
                wt = wbuf[slot, :, g * grp:(g + 1) * grp].astype(BF16)
                wc_scr[:, g * grp:(g + 1) * grp] = jnp.dot(wt, perm, preferred_element_type=F32).astype(BF16)

        x = xbuf[xslot].astype(BF16)
        bg = bg_ref[0]
        bu = bu_ref[0]
        for g in range(ngrp):
            gu = jnp.dot(x, wc_scr[:, g * grp:(g + 1) * grp], preferred_element_type=F32)
            gate = jnp.minimum(gu[:, 0:LANES] + bg[:, g * LANES:(g + 1) * LANES], SWIGLU_LIMIT)
            up = jnp.clip(gu[:, LANES:grp] + bu[:, g * LANES:(g + 1) * LANES], -SWIGLU_LIMIT, SWIGLU_LIMIT)
            act_ref[:, g * LANES:(g + 1) * LANES] = (
                (up + 1.0) * (gate * _sigmoid_t(gate * SWIGLU_ALPHA))).astype(act_ref.dtype)

    @pl.when(tv_ref[m] == 0)
    def _():
        act_ref[...] = jnp.zeros(act_ref.shape, act_ref.dtype)


def _moe_gate_up(rt, xs, w_gate_up, bg, bu):
    tm, tnw = MOE_TM, MOE_TNW
    P, D = xs.shape
    F2 = w_gate_up.shape[2]
    imap_b = lambda n, m, te, *_: (te[m], 0, n)
    return pl.pallas_call(
        _moe_gate_up_kernel,
        grid_spec=pltpu.PrefetchScalarGridSpec(
            num_scalar_prefetch=len(rt),
            grid=(F2 // tnw, P // tm),
            in_specs=[
                pl.BlockSpec(memory_space=pl.ANY),
                pl.BlockSpec(memory_space=pl.ANY),
                pl.BlockSpec((1, 1, tnw // 2), imap_b),
                pl.BlockSpec((1, 1, tnw // 2), imap_b),
            ],
            out_specs=pl.BlockSpec((tm, tnw // 2), lambda n, m, *_: (m, n)),
            scratch_shapes=[pltpu.VMEM((2, D, tnw), F32), pltpu.VMEM((D, tnw), BF16),
                            pltpu.SemaphoreType.DMA((2,)),
                            pltpu.VMEM((MOE_XSLOTS, tm, D), xs.dtype), pltpu.SemaphoreType.DMA((MOE_XSLOTS,))],
        ),
        out_shape=jax.ShapeDtypeStruct((P, F2 // 2), BF16),
        compiler_params=_cparams(("arbitrary", "arbitrary")),
        name="moe_gate_up",
    )(*rt, xs, w_gate_up, bg, bu)


def _moe_down_kernel(te_ref, tv_ref, tf_ref, tnx_ref, tlast_ref, trun_ref, nr_ref,
                     a_hbm, w_hbm, bd_ref, y_ref, wbuf, wc_scr, sem, abuf, asem):
    rt = (te_ref, tv_ref, tf_ref, tnx_ref, tlast_ref, trun_ref, nr_ref)
    m = pl.program_id(1)
    aslot = _tile_ring_step(a_hbm, abuf, asem)

    @pl.when(tv_ref[m] != 0)
    def _():
        @pl.when(tf_ref[m] != 0)
        def _():
            slot = _weight_ring_step(rt, w_hbm, wbuf, sem, wbuf.shape[2])
            wc_scr[...] = wbuf[slot].astype(BF16)

        y_ref[...] = jnp.dot(abuf[aslot], wc_scr[...], preferred_element_type=F32) + bd_ref[0]

    @pl.when(tv_ref[m] == 0)
    def _():
        y_ref[...] = jnp.zeros(y_ref.shape, y_ref.dtype)


def _moe_down(rt, act, w_down, bd):
    tm, tn = MOE_TM, MOE_TN
    P, F = act.shape
    D = w_down.shape[2]
    return pl.pallas_call(
        _moe_down_kernel,
        grid_spec=pltpu.PrefetchScalarGridSpec(
            num_scalar_prefetch=len(rt),
            grid=(D // tn, P // tm),
            in_specs=[
                pl.BlockSpec(memory_space=pl.ANY),
                pl.BlockSpec(memory_space=pl.ANY),
                pl.BlockSpec((1, 1, tn), lambda n, m, te, *_: (te[m], 0, n)),
            ],
            out_specs=pl.BlockSpec((tm, tn), lambda n, m, *_: (m, n)),
            scratch_shapes=[pltpu.VMEM((2, F, tn), F32), pltpu.VMEM((F, tn), BF16),
                            pltpu.SemaphoreType.DMA((2,)),
                            pltpu.VMEM((MOE_XSLOTS, tm, F), act.dtype), pltpu.SemaphoreType.DMA((MOE_XSLOTS,))],
        ),
        out_shape=jax.ShapeDtypeStruct((P, D), F32),
        compiler_params=_cparams(("arbitrary", "arbitrary")),
        name="moe_down",
    )(*rt, act, w_down, bd)


def _combine_kernel(x1_ref, yg_ref, tw_ref, fn_ref, y_ref):
    tw = tw_ref[...]
    moe = tw[:, 0:1] * yg_ref[0]
    for kk in range(1, TOP_K):
        moe = moe + tw[:, kk:kk + 1] * yg_ref[kk]
    y_ref[...] = _rms(x1_ref[...] + moe, fn_ref[...])


def _combine(x1, yg, tw, final_norm, tm, row_off):
    T, D = x1.shape
    off = row_off // tm
    return pl.pallas_call(
        _combine_kernel,
        grid=(T // tm,),
        in_specs=[
            pl.BlockSpec((tm, D), lambda i: (i, 0)),
            pl.BlockSpec((TOP_K, tm, D), lambda i: (0, i + off, 0)),
            pl.BlockSpec((tm, LANES), lambda i: (i, 0)),
            pl.BlockSpec((1, D), lambda i: (0, 0)),
        ],
        out_specs=pl.BlockSpec((tm, D), lambda i: (i, 0)),
        out_shape=jax.ShapeDtypeStruct((T, D), F32),
        compiler_params=_cparams(("parallel",)),
        name="combine",
    )(x1, yg, tw, final_norm)


def _routing(top_i):
    T = top_i.shape[0]
    P = T * TOP_K
    tm = MOE_TM
    n_tiles = -(-P // tm) + N_EXPERTS
    e_flat = top_i.reshape(P)
    onehot = (e_flat[:, None] == jnp.arange(N_EXPERTS, dtype=I32)[None, :]).astype(I32)
    csum = jnp.cumsum(onehot, axis=0)
    rank = jnp.sum((csum - 1) * onehot, axis=1)
    counts = csum[-1]
    padded = ((counts + tm - 1) // tm) * tm
    ends = jnp.cumsum(padded)
    starts = ends - padded
    pos = starts[e_flat] + rank
    tile_start = jnp.arange(n_tiles, dtype=I32) * tm
    tile_valid = (tile_start < ends[-1]).astype(I32)
    tile_expert = jnp.minimum(jnp.sum((tile_start[:, None] >= ends[None, :]).astype(I32), axis=1), N_EXPERTS - 1)
    tile_expert = tile_expert.astype(I32)
    order = jnp.argsort(e_flat, stable=True).astype(I32)
    tok_sorted = jnp.concatenate([order // TOP_K, jnp.zeros((tm,), I32)])
    group_start = jnp.cumsum(counts) - counts
    tile_src = jnp.where(tile_valid != 0, group_start[tile_expert] + tile_start - starts[tile_expert], 0)
    slot_src = (tile_src[:, None] + jnp.arange(tm, dtype=I32)[None, :]).reshape(-1)
    slot_token = tok_sorted.at[slot_src].get(mode="promise_in_bounds")
    tile_first = jnp.concatenate([jnp.ones((1,), I32), (tile_expert[1:] != tile_expert[:-1]).astype(I32)])
    tile_first = tile_first * tile_valid
    tile_run = jnp.cumsum(tile_first) - 1
    n_runs = jnp.sum(tile_first).reshape(1)
    experts = jnp.arange(N_EXPERTS, dtype=I32)
    later = (counts > 0)[None, :] & (experts[None, :] > tile_expert[:, None])
    nxt = jnp.min(jnp.where(later, experts[None, :], N_EXPERTS), axis=1)
    tile_last = (nxt == N_EXPERTS).astype(I32)
    tile_next = jnp.where(nxt == N_EXPERTS, tile_expert[0], nxt).astype(I32)
    tables = (tile_expert, tile_valid, tile_first, tile_next, tile_last, tile_run.astype(I32), n_runs.astype(I32))
    return pos, slot_token, tables


def kernel(x_prompt, x_sample, state_gdn, state_conv, state_hgrn, lb_table, norm1, w_in, conv_w, A_log, dt_bias,
           gdn_norm, hg_norm, w_out, norm2, w_router, b_router, w_gate_up, b_gate_up, w_down, b_down, final_norm):
    depth = w_in.shape[0]
    assert depth == 1
    B, L, D = x_prompt.shape
    SB = x_sample.shape[0]
    assert x_sample.shape[1] == 1
    nqk = HEADS * HD
    cd = 3 * nqk

    lbs = jnp.cumsum(jax.nn.softmax(lb_table.astype(F32), axis=0), axis=0)
    lb = lbs[0:1]

    assert w_in.shape[2] == MAIN_COLS + 2 * HEADS and cd + nqk == MAIN_COLS // 2
    ba0 = cd + nqk
    w_t = jnp.swapaxes(w_in[0], 0, 1).astype(BF16)
    wo_a = w_out[0, :nqk].astype(BF16)
    wo_b = w_out[0, nqk:].astype(BF16)
    wr_hi, wr_lo = _split(jnp.pad(w_router[0], ((0, 0), (0, LANES - N_EXPERTS))))
    br_p = jnp.pad(b_router[0], (0, LANES - N_EXPERTS))[None, :]
    bg = b_gate_up[0, :, None, 0::2]
    bu = b_gate_up[0, :, None, 1::2]
    bd = b_down[0][:, None, :]
    alog_b = jnp.pad(A_log[0], (HEADS, LANES - 2 * HEADS))[None, :]
    dtb_b = jnp.pad(dt_bias[0], (HEADS, LANES - 2 * HEADS))[None, :]
    alog_row = jnp.pad(A_log[0], (0, LANES - HEADS))[None, :]
    dtb_row = jnp.pad(dt_bias[0], (0, LANES - HEADS))[None, :]
    n1 = norm1[0][None, :]
    n2 = norm2[0][None, :]
    gn = gdn_norm[0][None, :]
    hn = hg_norm[0][None, :]
    fnw = final_norm[None, :]
    cw = conv_w[0]

    xp = x_prompt.reshape(B * L, D)
    xs = x_sample.reshape(SB, D)

    proj_p, ba_p = _in_proj(xp, n1, w_t, ba0, 2 * HEADS, tm=1024, tn=1024)
    proj_s, ba_s = _in_proj(xs, n1, w_t, ba0, 2 * HEADS, tm=SB, tn=1024)

    oa_p, sg_p = _gdn_prompt(proj_p, ba_p, cw, alog_b, dtb_b, gn, B, L)
    ob_p, sh_p = _hgrn_prompt(proj_p, lb, hn, B, L)
    oa_s, ob_s, sg_s, sh_s = _decode(proj_s, ba_s, state_conv[0].reshape(SB, (CONV_K - 1) * cd), state_gdn[0],
                                     state_hgrn[0], cw, alog_row, dtb_row, lb, gn, hn)

    ntok = B * L + SB
    x1_p, h2, ti, tw_p = _out_router(xp, oa_p, ob_p, wo_a, wo_b, n2, wr_hi, wr_lo, br_p,
                                     tm=512, total_rows=ntok, row_off=0)
    x1_s, h2, ti, tw_s = _out_router(xs, oa_s, ob_s, wo_a, wo_b, n2, wr_hi, wr_lo, br_p,
                                     tm=SB, total_rows=ntok, row_off=B * L, into=(h2, ti))

    top_i = ti[:, :TOP_K]
    pos, slot_token, tables = _routing(top_i)
    xs_sorted = h2.at[slot_token].get(mode="promise_in_bounds")
    act = _moe_gate_up(tables, xs_sorted, w_gate_up[0], bg, bu)
    yslots = _moe_down(tables, act, w_down[0], bd)
    pos_kmajor = pos.reshape(B * L + SB, TOP_K).T.reshape(-1)
    yg = yslots.at[pos_kmajor].get(mode="promise_in_bounds").reshape(TOP_K, B * L + SB, D)

    y_p = _combine(x1_p, yg, tw_p, fnw, tm=256, row_off=0)
    y_s = _combine(x1_s, yg, tw_s, fnw, tm=SB, row_off=B * L)

    conv_p = proj_p.reshape(B, L, MAIN_COLS)[:, L - (CONV_K - 1):, :cd]
    conv_s = jnp.concatenate([state_conv[0][:, 1:, :], proj_s[:, None, :cd]], axis=1)
    return (y_p.reshape(B, L, D), y_s.reshape(SB, 1, D),
            sg_p[None], conv_p[None].astype(state_conv.dtype), sh_p[None],
            sg_s[None], conv_s[None].astype(state_conv.dtype), sh_s[None])
```
